```python
import math
import jax
import jax.numpy as jnp
from jax import lax
import numpy as np

D_MODEL = 2048
BATCH = 2
SEQ = 4096
DEPTH = 4

GRID_W = 64
CTX_LEN = 256
N_MIXERS = 3
EPS = 1e-6
D_FF = ((8 * D_MODEL // 3 + 255) // 256) * 256

S5_GROUP = 16
S5_STATE = 64
S5_GROUPS = D_MODEL // S5_GROUP

HY_ORDER = 2
HY_CONV_W = 3
HY_BANDS = 16
HY_EMB = 1 + 2 * HY_BANDS
HY_FILTER_HIDDEN = 64
HY_DECAY_TARGET = 1e-2
HY_FAST_DECAY_PCT = 0.3
HY_SLOW_DECAY_PCT = 1.5
HY_MAX_DECAY = math.log(HY_DECAY_TARGET) / HY_FAST_DECAY_PCT
HY_MIN_DECAY = math.log(HY_DECAY_TARGET) / HY_SLOW_DECAY_PCT

GDN_K_HEADS = 16
GDN_V_HEADS = 32
GDN_HEAD_K = D_MODEL // GDN_K_HEADS
GDN_HEAD_V = 128
GDN_QK = GDN_K_HEADS * GDN_HEAD_K
GDN_V = GDN_V_HEADS * GDN_HEAD_V
GDN_CONV_DIM = 2 * GDN_QK + GDN_V
GDN_PROJ = GDN_CONV_DIM + GDN_V + 4 * GDN_V_HEADS
GDN_CONV_W = 5
GDN_CHUNK = 64

N_S5 = (DEPTH + 2) // 3
N_HY = (DEPTH + 1) // 3
N_GDN = DEPTH // 3

kernel_name = 'hybrid_s5_hyena_gdn_prefix_dit'

F32 = jnp.float32


def _rmsnorm(x, g):
    xf = x.astype(F32)
    y = xf * lax.rsqrt(jnp.mean(xf * xf, axis=-1, keepdims=True) + EPS)
    return (y * g.astype(F32)).astype(x.dtype)


def _l2norm(x):
    return x * lax.rsqrt(jnp.sum(x * x, axis=-1, keepdims=True) + 1e-6)


def _flip_time(t, rev):
    return jnp.flip(t, axis=1) if rev else t


def _grid_order(x, col_major):
    if not col_major:
        return x
    b, n, d = x.shape
    rows = n // GRID_W
    return x.reshape(b, rows, GRID_W, d).transpose(0, 2, 1, 3).reshape(b, n, d)


def _grid_unorder(x, col_major):
    if not col_major:
        return x
    b, n, d = x.shape
    rows = n // GRID_W
    return x.reshape(b, GRID_W, rows, d).transpose(0, 2, 1, 3).reshape(b, n, d)


def _dwconv(x, w):
    k_w = w.shape[0]
    n = x.shape[1]
    p = k_w // 2
    xp = jnp.pad(x, ((0, 0), (p, p), (0, 0)))
    y = xp[:, 0:n] * w[0]
    for t in range(1, k_w):
        y = y + xp[:, t:t + n] * w[t]
    return y


def _swiglu(h, wg, wu, wd):
    return (jax.nn.silu(h @ wg) * (h @ wu)) @ wd


def _s5_discretise(a_re, a_im, log_dt, b_re, b_im):
    lam = lax.complex(a_re.astype(F32), a_im.astype(F32))
    dt = jnp.exp(log_dt.astype(F32))[:, None]
    lam_bar = jnp.exp(lam * dt)
    b = lax.complex(b_re.astype(F32), b_im.astype(F32))
    b_bar = ((lam_bar - 1.0) / lam)[..., None] * b
    return lam_bar, b_bar


def _s5_scan(u, lam_bar, b_bar, h0):
    bu = jnp.einsum('blgc,gpc->lbgp', u.astype(jnp.complex64), b_bar)
    a = jnp.broadcast_to(lam_bar[None, None], (u.shape[1], 1) + lam_bar.shape)

    def combine(e1, e2):
        a1, b1 = e1
        a2, b2 = e2
        return a2 * a1, a2 * b1 + b2

    a_cum, hs = lax.associative_scan(combine, (a, bu), axis=0)
    hs = hs + a_cum * h0[None]
    return hs, hs[-1]


def _s5_readout(hs, cmat):
    y = jnp.einsum('lbgp,gcp->blgc', hs, cmat).real
    return y.reshape(y.shape[0], y.shape[1], -1)


def _s5_mixer(h_ctx, h_lat, a_re, a_im, log_dt, b_re, b_im, c_re, c_im, d, glu_w, glu_b):
    def groups(t):
        return t.astype(F32).reshape(t.shape[0], t.shape[1], S5_GROUPS, S5_GROUP)

    u_ctx, u_lat = groups(h_ctx), groups(h_lat)
    y_ctx = h_ctx.astype(F32) * d.astype(F32)
    y_lat = h_lat.astype(F32) * d.astype(F32)
    h0 = jnp.zeros((h_lat.shape[0], S5_GROUPS, S5_STATE), jnp.complex64)
    for direction in range(2):
        rev = direction == 1
        lam_bar, b_bar = _s5_discretise(a_re[direction], a_im[direction], log_dt[direction],
                                        b_re[direction], b_im[direction])
        cmat = lax.complex(c_re[direction].astype(F32), c_im[direction].astype(F32))
        hs_ctx, h_ctx_last = _s5_scan(_flip_time(u_ctx, rev), lam_bar, b_bar, h0)
        hs_lat, _ = _s5_scan(_flip_time(u_lat, rev), lam_bar, b_bar, h_ctx_last)
        y_ctx = y_ctx + _flip_time(_s5_readout(hs_ctx, cmat), rev)
        y_lat = y_lat + _flip_time(_s5_readout(hs_lat, cmat), rev)

    def glu(y):
        z = jax.nn.gelu(y)
        gz = z @ glu_w.astype(F32) + glu_b.astype(F32)
        return gz[..., :D_MODEL] * jax.nn.sigmoid(gz[..., D_MODEL:])

    return glu(y_ctx).astype(h_ctx.dtype), glu(y_lat).astype(h_lat.dtype)


def _hyena_filters(n, w1, b1, w2, b2, w3, b3, w4, freq):
    t = jnp.arange(n, dtype=F32)
    t_unit = t / max(n - 1, 1)
    bands = jnp.linspace(1e-4, HY_BANDS - 1, HY_BANDS, dtype=F32)
    ang = (2.0 * math.pi / n) * t[:, None] * bands[None, :]
    feats = jnp.concatenate([t_unit[:, None], jnp.cos(ang), -jnp.sin(ang)], axis=-1)
    fq = freq.astype(F32)
    hdn = jnp.sin(fq * (feats @ w1.astype(F32) + b1.astype(F32)))
    hdn = jnp.sin(fq * (hdn @ w2.astype(F32) + b2.astype(F32)))
    hdn = jnp.sin(fq * (hdn @ w3.astype(F32) + b3.astype(F32)))
    filt = (hdn @ w4.astype(F32)).reshape(n, HY_ORDER, 2, D_MODEL)
    deltas = jnp.abs(jnp.linspace(HY_MIN_DECAY, HY_MAX_DECAY, D_MODEL, dtype=F32))
    filt = filt * jnp.exp(-t_unit[:, None] * deltas[None, :])[:, None, None, :]
    two_sided = jnp.concatenate([filt[:, :, 0],
                                 jnp.zeros((1, HY_ORDER, D_MODEL), F32),
                                 jnp.flip(filt[1:, :, 1], axis=0)], axis=0)
    two_sided = two_sided / jnp.sum(jnp.abs(two_sided), axis=0, keepdims=True)
    return jnp.fft.rfft(two_sided, axis=0)


def _fft_conv(z, hf, bias):
    n = z.shape[1]
    zf = jnp.fft.rfft(z, n=2 * n, axis=1)
    y = jnp.fft.irfft(zf * hf[None], n=2 * n, axis=1)[:, :n]
    return y + z * bias.astype(F32)


def _hyena_mixer(h_ctx, h_lat, in_w, in_b, conv_w, conv_b, f_w1, f_b1, f_w2, f_b2, f_w3, f_b3,
                 f_w4, f_freq, bias, out_w, out_b):
    def one_sequence(h):
        n = h.shape[1]
        u = (h @ in_w + in_b).astype(F32)
        u = _dwconv(u, conv_w.astype(F32)) + conv_b.astype(F32)
        v, x1, x2 = u[..., :D_MODEL], u[..., D_MODEL:2 * D_MODEL], u[..., 2 * D_MODEL:]
        hf = _hyena_filters(n, f_w1, f_b1, f_w2, f_b2, f_w3, f_b3, f_w4, f_freq)
        z = x1 * _fft_conv(v, hf[:, 0], bias[0])
        z = x2 * _fft_conv(z, hf[:, 1], bias[1])
        return (z @ out_w.astype(F32) + out_b.astype(F32)).astype(h.dtype)

    return one_sequence(h_ctx), one_sequence(h_lat)


def _gdn_features(h, in_w, conv_w, a_log, dt_bias):
    bsz, n, _ = h.shape
    proj = (h @ in_w).astype(F32)
    qkv = jax.nn.silu(_dwconv(proj[..., :GDN_CONV_DIM], conv_w.astype(F32)))
    q = qkv[..., :GDN_QK].reshape(bsz, n, GDN_K_HEADS, GDN_HEAD_K)
    k = qkv[..., GDN_QK:2 * GDN_QK].reshape(bsz, n, GDN_K_HEADS, GDN_HEAD_K)
    v = qkv[..., 2 * GDN_QK:].reshape(bsz, n, GDN_V_HEADS, GDN_HEAD_V)
    rep = GDN_V_HEADS // GDN_K_HEADS
    q = jnp.repeat(_l2norm(q), rep, axis=2)
    k = jnp.repeat(_l2norm(k), rep, axis=2)
    z = proj[..., GDN_CONV_DIM:GDN_CONV_DIM + GDN_V].reshape(bsz, n, GDN_V_HEADS, GDN_HEAD_V)
    ab = proj[..., GDN_CONV_DIM + GDN_V:].reshape(bsz, n, 2, 2, GDN_V_HEADS)
    g = -jnp.exp(a_log.astype(F32)) * jax.nn.softplus(ab[:, :, 0] + dt_bias.astype(F32))
    beta = jax.nn.sigmoid(ab[:, :, 1])
    return q, k, v, z, g, beta


def _gdn_chunked(q, k, v, g, beta, h0):
    bsz, n, heads, dk = k.shape
    dv = v.shape[-1]
    nc = n // GDN_CHUNK

    def chunks(t):
        t = t.reshape((bsz, nc, GDN_CHUNK, heads) + t.shape[3:])
        return jnp.moveaxis(t, 3, 1)

    q = chunks(q) * (dk ** -0.5)
    k = chunks(k)
    v = chunks(v)
    g = chunks(g)
    beta = chunks(beta)
    gc = jnp.cumsum(g, axis=-1)
    idx = jnp.arange(GDN_CHUNK)
    incl = idx[:, None] >= idx[None, :]
    strict = idx[:, None] > idx[None, :]
    decay = jnp.exp(jnp.where(incl, gc[..., :, None] - gc[..., None, :], -jnp.inf))
    kb = k * beta[..., None]
    lower = jnp.where(strict, jnp.einsum('bhnid,bhnjd->bhnij', kb, k) * decay, 0.0)
    tri = lower + jnp.eye(GDN_CHUNK, dtype=F32)
    u = lax.linalg.triangular_solve(tri, v * beta[..., None], left_side=True, lower=True,
                                    unit_diagonal=True)
    w = lax.linalg.triangular_solve(tri, kb * jnp.exp(gc)[..., None], left_side=True, lower=True,
                                    unit_diagonal=True)
    attn = jnp.where(incl, jnp.einsum('bhnid,bhnjd->bhnij', q, k) * decay, 0.0)
    qg = q * jnp.exp(gc)[..., None]
    kg = k * jnp.exp(gc[..., -1:] - gc)[..., None]
    gl = jnp.exp(gc[..., -1])
    xs = tuple(jnp.moveaxis(t, 2, 0) for t in (qg, kg, u, w, attn, gl))

    def step(state, inp):
        qg_i, kg_i, u_i, w_i, attn_i, gl_i = inp
        v_new = u_i - jnp.einsum('bhcd,bhde->bhce', w_i, state)
        o_i = (jnp.einsum('bhcd,bhde->bhce', qg_i, state)
               + jnp.einsum('bhcs,bhse->bhce', attn_i, v_new))
        state = state * gl_i[..., None, None] + jnp.einsum('bhcd,bhce->bhde', kg_i, v_new)
        return state, o_i

    h_last, o = lax.scan(step, h0, xs)
    o = jnp.transpose(o, (1, 0, 3, 2, 4)).reshape(bsz, n, heads, dv)
    return o, h_last


def _gdn_mixer(h_ctx, h_lat, in_w, conv_w, a_log, dt_bias, norm_g, out_w):
    qc, kc, vc, zc, gc, bc = _gdn_features(h_ctx, in_w, conv_w, a_log, dt_bias)
    ql, kl, vl, zl, gl, bl = _gdn_features(h_lat, in_w, conv_w, a_log, dt_bias)
    h0 = jnp.zeros((h_lat.shape[0], GDN_V_HEADS, GDN_HEAD_K, GDN_HEAD_V), F32)
    o_ctx = jnp.zeros_like(vc)
    o_lat = jnp.zeros_like(vl)
    for direction in range(2):
        rev = direction == 1
        oc, h_ctx_last = _gdn_chunked(_flip_time(qc, rev), _flip_time(kc, rev), _flip_time(vc, rev),
                                      _flip_time(gc[:, :, direction], rev),
                                      _flip_time(bc[:, :, direction], rev), h0)
        ol, _ = _gdn_chunked(_flip_time(ql, rev), _flip_time(kl, rev), _flip_time(vl, rev),
                             _flip_time(gl[:, :, direction], rev),
                             _flip_time(bl[:, :, direction], rev), h_ctx_last)
        o_ctx = o_ctx + _flip_time(oc, rev)
        o_lat = o_lat + _flip_time(ol, rev)

    def gated_out(o, z):
        o = o * lax.rsqrt(jnp.mean(o * o, axis=-1, keepdims=True) + EPS)
        o = o * norm_g.astype(F32) * jax.nn.silu(z)
        return o.reshape(o.shape[0], o.shape[1], GDN_V) @ out_w.astype(F32)

    return gated_out(o_ctx, zc).astype(h_ctx.dtype), gated_out(o_lat, zl).astype(h_lat.dtype)


def setup_inputs(seed: int = 0) -> dict:
    key = jax.random.key(seed)
    ks = iter(jax.random.split(key, 48))

    def nrm(shape, std):
        return jax.random.normal(next(ks), shape, F32) * std

    def uni(shape, lo, hi):
        return jax.random.uniform(next(ks), shape, F32, lo, hi)

    d = D_MODEL
    g_, p_, hg = S5_GROUPS, S5_STATE, S5_GROUP
    inp = {}
    inp['x'] = nrm((BATCH, SEQ, d), 1.0)
    inp['c'] = nrm((BATCH, d), 1.0)
    inp['ctx'] = nrm((BATCH, CTX_LEN, d), 1.0)
    inp['c_ctx'] = nrm((d,), 1.0)
    inp['ada_w'] = nrm((DEPTH, d, 6 * d), 0.5 * d ** -0.5)
    inp['ada_b'] = nrm((DEPTH, 6 * d), 0.01)
    inp['norm_g'] = 1.0 + nrm((DEPTH, 2, d), 0.01)
    inp['final_g'] = 1.0 + nrm((d,), 0.01)
    inp['ffn_w_gate'] = nrm((DEPTH, d, D_FF), d ** -0.5)
    inp['ffn_w_up'] = nrm((DEPTH, d, D_FF), d ** -0.5)
    inp['ffn_w_down'] = nrm((DEPTH, D_FF, d), D_FF ** -0.5)
    inp['s5_a_re'] = -0.5 + nrm((N_S5, 2, g_, p_), 0.01)
    inp['s5_a_im'] = jnp.pi * jnp.arange(p_, dtype=F32) + nrm((N_S5, 2, g_, p_), 0.01)
    inp['s5_log_dt'] = uni((N_S5, 2, g_), math.log(1e-3), math.log(1e-1))
    inp['s5_b_re'] = nrm((N_S5, 2, g_, p_, hg), (2 * hg) ** -0.5)
    inp['s5_b_im'] = nrm((N_S5, 2, g_, p_, hg), (2 * hg) ** -0.5)
    inp['s5_c_re'] = nrm((N_S5, 2, g_, hg, p_), p_ ** -0.5)
    inp['s5_c_im'] = nrm((N_S5, 2, g_, hg, p_), p_ ** -0.5)
    inp['s5_d'] = nrm((N_S5, d), 0.5)
    inp['s5_glu_w'] = nrm((N_S5, d, 2 * d), d ** -0.5)
    inp['s5_glu_b'] = nrm((N_S5, 2 * d), 0.01)
    inp['hy_in_w'] = nrm((N_HY, d, 3 * d), d ** -0.5)
    inp['hy_in_b'] = nrm((N_HY, 3 * d), 0.01)
    inp['hy_conv_w'] = nrm((N_HY, HY_CONV_W, 3 * d), HY_CONV_W ** -0.5)
    inp['hy_conv_b'] = nrm((N_HY, 3 * d), 0.01)
    inp['hy_f_w1'] = nrm((N_HY, HY_EMB, HY_FILTER_HIDDEN), HY_EMB ** -0.5)
    inp['hy_f_b1'] = nrm((N_HY, HY_FILTER_HIDDEN), 0.02)
    inp['hy_f_w2'] = nrm((N_HY, HY_FILTER_HIDDEN, HY_FILTER_HIDDEN), HY_FILTER_HIDDEN ** -0.5)
    inp['hy_f_b2'] = nrm((N_HY, HY_FILTER_HIDDEN), 0.02)
    inp['hy_f_w3'] = nrm((N_HY, HY_FILTER_HIDDEN, HY_FILTER_HIDDEN), HY_FILTER_HIDDEN ** -0.5)
    inp['hy_f_b3'] = nrm((N_HY, HY_FILTER_HIDDEN), 0.02)
    inp['hy_f_w4'] = nrm((N_HY, HY_FILTER_HIDDEN, HY_ORDER * 2 * d), HY_FILTER_HIDDEN ** -0.5)
    inp['hy_f_freq'] = 1.0 + nrm((N_HY, HY_FILTER_HIDDEN), 0.01)
    inp['hy_bias'] = nrm((N_HY, HY_ORDER, d), 1.0)
    inp['hy_out_w'] = nrm((N_HY, d, d), d ** -0.5)
    inp['hy_out_b'] = nrm((N_HY, d), 0.01)
    inp['gdn_in_w'] = nrm((N_GDN, d, GDN_PROJ), d ** -0.5)
    inp['gdn_conv_w'] = nrm((N_GDN, GDN_CONV_W, GDN_CONV_DIM), GDN_CONV_W ** -0.5)
    inp['gdn_a_log'] = jnp.log(uni((N_GDN, 2, GDN_V_HEADS), 1.0, 16.0))
    dt = jnp.exp(uni((N_GDN, 2, GDN_V_HEADS), math.log(1e-3), math.log(1e-1)))
    inp['gdn_dt_bias'] = dt + jnp.log(-jnp.expm1(-dt))
    inp['gdn_norm_g'] = 1.0 + nrm((N_GDN, GDN_HEAD_V), 0.01)
    inp['gdn_out_w'] = nrm((N_GDN, GDN_V, d), GDN_V ** -0.5)
    return inp


def reference(x, c, ctx, c_ctx, ada_w, ada_b, norm_g, final_g, ffn_w_gate, ffn_w_up, ffn_w_down,
              s5_a_re, s5_a_im, s5_log_dt, s5_b_re, s5_b_im, s5_c_re, s5_c_im, s5_d, s5_glu_w,
              s5_glu_b, hy_in_w, hy_in_b, hy_conv_w, hy_conv_b, hy_f_w1, hy_f_b1, hy_f_w2, hy_f_b2,
              hy_f_w3, hy_f_b3, hy_f_w4, hy_f_freq, hy_bias, hy_out_w, hy_out_b, gdn_in_w,
              gdn_conv_w, gdn_a_log, gdn_dt_bias, gdn_norm_g, gdn_out_w):
    silu_c = jax.nn.silu(c)
    silu_cc = jax.nn.silu(c_ctx)
    for i in range(DEPTH):
        kind, j = i % N_MIXERS, i // N_MIXERS
        col_major = (j % 2) == 1
        sh1, sc1, gt1, sh2, sc2, gt2 = jnp.split((silu_c @ ada_w[i] + ada_b[i])[:, None, :], 6, axis=-1)
        csh1, csc1, cgt1, csh2, csc2, cgt2 = jnp.split(silu_cc @ ada_w[i] + ada_b[i], 6, axis=-1)
        h_lat = _grid_order(_rmsnorm(x, norm_g[i, 0]) * (1.0 + sc1) + sh1, col_major)
        h_ctx = _rmsnorm(ctx, norm_g[i, 0]) * (1.0 + csc1) + csh1
        if kind == 0:
            o_ctx, o_lat = _s5_mixer(h_ctx, h_lat, s5_a_re[j], s5_a_im[j], s5_log_dt[j], s5_b_re[j],
                                     s5_b_im[j], s5_c_re[j], s5_c_im[j], s5_d[j], s5_glu_w[j],
                                     s5_glu_b[j])
        elif kind == 1:
            o_ctx, o_lat = _hyena_mixer(h_ctx, h_lat, hy_in_w[j], hy_in_b[j], hy_conv_w[j],
                                        hy_conv_b[j], hy_f_w1[j], hy_f_b1[j], hy_f_w2[j], hy_f_b2[j],
                                        hy_f_w3[j], hy_f_b3[j], hy_f_w4[j], hy_f_freq[j], hy_bias[j],
                                        hy_out_w[j], hy_out_b[j])
        else:
            o_ctx, o_lat = _gdn_mixer(h_ctx, h_lat, gdn_in_w[j], gdn_conv_w[j], gdn_a_log[j],
                                      gdn_dt_bias[j], gdn_norm_g[j], gdn_out_w[j])
        x = x + gt1 * _grid_unorder(o_lat, col_major)
        x = x + gt2 * _swiglu(_rmsnorm(x, norm_g[i, 1]) * (1.0 + sc2) + sh2,
                              ffn_w_gate[i], ffn_w_up[i], ffn_w_down[i])
        if i < DEPTH - 1:
            ctx = ctx + cgt1 * o_ctx
            ctx = ctx + cgt2 * _swiglu(_rmsnorm(ctx, norm_g[i, 1]) * (1.0 + csc2) + csh2,
                                       ffn_w_gate[i], ffn_w_up[i], ffn_w_down[i])
    return _rmsnorm(x, final_g)
```

```python
import functools
import math

import jax
import jax.numpy as jnp
from jax import lax
from jax.experimental import pallas as pl
from jax.experimental.pallas import tpu as pltpu

F32 = jnp.float32
BF16 = jnp.bfloat16
HIGHEST = lax.Precision.HIGHEST

D_MODEL = 2048
GRID_W = 64
EPS = 1e-6
N_MIXERS = 3

S5_GROUP = 16
S5_STATE = 64
S5_GROUPS = D_MODEL // S5_GROUP
S5_CHUNK = 16
S5_ROW = S5_CHUNK * S5_GROUP
S5_GROUPS_PER_STEP = 4

HY_ORDER = 2
HY_BANDS = 16
HY_EMB = 1 + 2 * HY_BANDS
HY_EMB_PAD = 128
HY_FILTER_HIDDEN = 64
HY_DECAY_TARGET = 1e-2
HY_MAX_DECAY = math.log(HY_DECAY_TARGET) / 0.3
HY_MIN_DECAY = math.log(HY_DECAY_TARGET) / 1.5

GDN_K_HEADS = 16
GDN_V_HEADS = 32
GDN_HEAD = 128
GDN_QK = GDN_K_HEADS * GDN_HEAD
GDN_V = GDN_V_HEADS * GDN_HEAD
GDN_CONV_DIM = 2 * GDN_QK + GDN_V
GDN_MAIN = GDN_CONV_DIM + GDN_V
GDN_CHUNK = 64

VMEM_LIMIT_BYTES = 56 * 1024 * 1024


def _params(*sem):
    return pltpu.CompilerParams(dimension_semantics=sem, vmem_limit_bytes=VMEM_LIMIT_BYTES)


def _dot(a, b):
    return jnp.dot(a, b, preferred_element_type=F32)


def _dot_nt(a, b):
    return lax.dot_general(a, b, (((1,), (1,)), ((), ())), preferred_element_type=F32)


def _dot_tn(a, b):
    return lax.dot_general(a, b, (((0,), (0,)), ((), ())), preferred_element_type=F32)


def _sigmoid(x):
    return 1.0 / (1.0 + jnp.exp(-x))


def _silu(x):
    return x * _sigmoid(x)


def _norm_mod(x, g, sc, sh):
    y = x * lax.rsqrt(jnp.mean(x * x, axis=-1, keepdims=True) + EPS) * g
    return y * (1.0 + sc) + sh


def _mod_spec(layer, tm, lat_len, batch, width, col_block):
    tiles_per_seq = lat_len // tm

    def index(i, *_):
        return (layer * 8 + jnp.minimum(i // tiles_per_seq, batch), 0, col_block)

    return pl.BlockSpec((None, 1, width), index)


def _ada_kernel(c_ref, w_ref, b_ref, o_ref):
    c = c_ref[...]
    s = _silu(c).astype(BF16)
    o_ref[...] = _dot(s, w_ref[...].astype(BF16)) + b_ref[...]


def _ada_all(cvec, ada_w, ada_b):
    depth, d, n6 = ada_w.shape
    tn = 1024
    return pl.pallas_call(
        _ada_kernel,
        grid=(depth, n6 // tn),
        in_specs=[pl.BlockSpec((8, d), lambda l, j: (0, 0)),
                  pl.BlockSpec((None, d, tn), lambda l, j: (l, 0, j)),
                  pl.BlockSpec((None, 1, tn), lambda l, j: (l, 0, j))],
        out_specs=pl.BlockSpec((None, 8, tn), lambda l, j: (l, 0, j)),
        out_shape=jax.ShapeDtypeStruct((depth, 8, n6), F32),
        compiler_params=_params("parallel", "parallel"),
    )(cvec, ada_w, ada_b.reshape(depth, 1, n6))


def _prenorm_kernel(x_ref, mod_ref, g_ref, o_ref):
    d = D_MODEL
    o_ref[...] = _norm_mod(x_ref[...], g_ref[...], mod_ref[:, d:2 * d], mod_ref[:, 0:d])


def _prenorm(x, mods, g, layer, dims):
    batch, lat_len, _, tm = dims
    ntok, d = x.shape
    return pl.pallas_call(
        _prenorm_kernel,
        grid=(ntok // tm,),
        in_specs=[pl.BlockSpec((tm, d), lambda i: (i, 0)),
                  _mod_spec(layer, tm, lat_len, batch, 6 * d, 0),
                  pl.BlockSpec((1, d), lambda i: (0, 0))],
        out_specs=pl.BlockSpec((tm, d), lambda i: (i, 0)),
        out_shape=jax.ShapeDtypeStruct((ntok, d), F32),
        compiler_params=_params("parallel"),
    )(x, mods, g.reshape(1, d))


def _mm_norm_kernel(x_ref, mod_ref, g_ref, w_ref, b_ref, o_ref, h_scr):
    d = D_MODEL

    @pl.when(pl.program_id(1) == 0)
    def _():
        h_scr[...] = _norm_mod(x_ref[...], g_ref[...], mod_ref[:, d:2 * d], mod_ref[:, 0:d]).astype(BF16)

    o_ref[...] = _dot(h_scr[...], w_ref[...]) + b_ref[...]


def _mm_norm(x, mods, g, w, b, layer, dims, tn):
    batch, lat_len, _, tm = dims
    ntok, d = x.shape
    n = w.shape[1]
    return pl.pallas_call(
        _mm_norm_kernel,
        grid=(ntok // tm, n // tn),
        in_specs=[pl.BlockSpec((tm, d), lambda i, j: (i, 0)),
                  _mod_spec(layer, tm, lat_len, batch, 6 * d, 0),
                  pl.BlockSpec((1, d), lambda i, j: (0, 0)),
                  pl.BlockSpec((d, tn), lambda i, j: (0, j)),
                  pl.BlockSpec((1, tn), lambda i, j: (0, j))],
        out_specs=pl.BlockSpec((tm, tn), lambda i, j: (i, j)),
        out_shape=jax.ShapeDtypeStruct((ntok, n), F32),
        scratch_shapes=[pltpu.VMEM((tm, d), BF16)],
        compiler_params=_params("parallel", "arbitrary"),
    )(x, mods, g.reshape(1, d), w, b.reshape(1, n))


def _mm_res_kernel(a_ref, w_ref, b_ref, x_ref, gate_ref, o_ref):
    y = _dot(a_ref[...], w_ref[...]) + b_ref[...]
    o_ref[...] = x_ref[...] + gate_ref[...] * y


def _mm_res(a, w, b, x, mods, layer, dims, tn):
    batch, lat_len, _, tm = dims
    ntok, d = x.shape
    k = a.shape[1]
    gate_blk = 2 * d // tn
    tiles_per_seq = lat_len // tm
    return pl.pallas_call(
        _mm_res_kernel,
        grid=(ntok // tm, d // tn),
        in_specs=[pl.BlockSpec((tm, k), lambda i, j: (i, 0)),
                  pl.BlockSpec((k, tn), lambda i, j: (0, j)),
                  pl.BlockSpec((1, tn), lambda i, j: (0, j)),
                  pl.BlockSpec((tm, tn), lambda i, j: (i, j)),
                  pl.BlockSpec((None, 1, tn), lambda i, j: (
                      layer * 8 + jnp.minimum(i // tiles_per_seq, batch), 0, gate_blk + j))],
        out_specs=pl.BlockSpec((tm, tn), lambda i, j: (i, j)),
        out_shape=jax.ShapeDtypeStruct((ntok, d), F32),
        compiler_params=_params("parallel", "parallel"),
    )(a, w, b.reshape(1, d), x, mods)


def _ffn_kernel(x_ref, mod_ref, g_ref, wg_ref, wu_ref, wd_ref, o_ref, h_scr, acc_scr):
    d = D_MODEL
    f = pl.program_id(1)

    @pl.when(f == 0)
    def _():
        h_scr[...] = _norm_mod(x_ref[...], g_ref[...], mod_ref[:, 4 * d:5 * d],
                               mod_ref[:, 3 * d:4 * d]).astype(BF16)
        acc_scr[...] = jnp.zeros_like(acc_scr)

    h = h_scr[...]
    gate = _dot(h, wg_ref[...])
    up = _dot(h, wu_ref[...])
    act = (_silu(gate) * up).astype(BF16)
    acc_scr[...] += _dot(act, wd_ref[...])

    @pl.when(f == pl.num_programs(1) - 1)
    def _():
        o_ref[...] = x_ref[...] + mod_ref[:, 5 * d:6 * d] * acc_scr[...]


def _ffn(x, mods, g, wg, wu, wd, layer, dims):
    batch, lat_len, _, tm = dims
    ntok, d = x.shape
    dff = wg.shape[1]
    tf = 512
    return pl.pallas_call(
        _ffn_kernel,
        grid=(ntok // tm, dff // tf),
        in_specs=[pl.BlockSpec((tm, d), lambda i, f: (i, 0)),
                  _mod_spec(layer, tm, lat_len, batch, 6 * d, 0),
                  pl.BlockSpec((1, d), lambda i, f: (0, 0)),
                  pl.BlockSpec((d, tf), lambda i, f: (0, f)),
                  pl.BlockSpec((d, tf), lambda i, f: (0, f)),
                  pl.BlockSpec((tf, d), lambda i, f: (f, 0))],
        out_specs=pl.BlockSpec((tm, d), lambda i, f: (i, 0)),
        out_shape=jax.ShapeDtypeStruct((ntok, d), F32),
        scratch_shapes=[pltpu.VMEM((tm, d), BF16), pltpu.VMEM((tm, d), F32)],
        compiler_params=_params("parallel", "arbitrary"),
    )(x, mods, g.reshape(1, d), wg, wu, wd)


def _final_norm_kernel(x_ref, g_ref, o_ref):
    x = x_ref[...]
    o_ref[...] = x * lax.rsqrt(jnp.mean(x * x, axis=-1, keepdims=True) + EPS) * g_ref[...]


def _final_norm(x, g, nrows, tm):
    d = x.shape[1]
    return pl.pallas_call(
        _final_norm_kernel,
        grid=(nrows // tm,),
        in_specs=[pl.BlockSpec((tm, d), lambda i: (i, 0)), pl.BlockSpec((1, d), lambda i: (0, 0))],
        out_specs=pl.BlockSpec((tm, d), lambda i: (i, 0)),
        out_shape=jax.ShapeDtypeStruct((nrows, d), F32),
        compiler_params=_params("parallel"),
    )(x, g.reshape(1, d))


def _s5_tables(a_re, a_im, log_dt, b_re, b_im, c_re, c_im):
    t_len = S5_CHUNK
    dt = jnp.exp(log_dt)[..., None]
    lr, li = a_re * dt, a_im * dt
    er = jnp.exp(lr)
    nr, ni = er * jnp.cos(li) - 1.0, er * jnp.sin(li)
    den = a_re * a_re + a_im * a_im
    qr, qi = (nr * a_re + ni * a_im) / den, (ni * a_re - nr * a_im) / den
    bbr = qr[..., None] * b_re - qi[..., None] * b_im
    bbi = qr[..., None] * b_im + qi[..., None] * b_re
    taus = jnp.arange(t_len + 1, dtype=F32)[:, None, None, None]
    mag = jnp.exp(lr[None] * taus)
    pr, pi = mag * jnp.cos(li[None] * taus), mag * jnp.sin(li[None] * taus)
    clr = c_re[None] * pr[:, :, :, None, :] - c_im[None] * pi[:, :, :, None, :]
    cli = c_re[None] * pi[:, :, :, None, :] + c_im[None] * pr[:, :, :, None, :]
    kern = (jnp.einsum('tdgop,dgpi->dgtoi', clr, bbr, precision=HIGHEST)
            - jnp.einsum('tdgop,dgpi->dgtoi', cli, bbi, precision=HIGHEST))
    ti = jnp.arange(t_len)
    ngroups = a_re.shape[1]
    m_intra, w_sr, w_si, w_or, w_oi = [], [], [], [], []
    for direction in range(2):
        lag = (ti[None, :] - ti[:, None]) if direction == 0 else (ti[:, None] - ti[None, :])
        kd = kern[direction][:, jnp.clip(lag, 0, t_len)]
        kd = jnp.where((lag >= 0)[None, :, :, None, None], kd, 0.0)
        m_intra.append(kd.transpose(0, 1, 4, 2, 3).reshape(ngroups, S5_ROW, S5_ROW))
        st_pow = (t_len - 1 - ti) if direction == 0 else ti
        sr = pr[st_pow, direction][..., None] * bbr[direction][None] - pi[st_pow, direction][..., None] * bbi[direction][None]
        si = pr[st_pow, direction][..., None] * bbi[direction][None] + pi[st_pow, direction][..., None] * bbr[direction][None]
        w_sr.append(sr.transpose(1, 0, 3, 2).reshape(ngroups, S5_ROW, S5_STATE))
        w_si.append(si.transpose(1, 0, 3, 2).reshape(ngroups, S5_ROW, S5_STATE))
        out_pow = (ti + 1) if direction == 0 else (t_len - ti)
        w_or.append(clr[out_pow, direction].transpose(1, 3, 0, 2).reshape(ngroups, S5_STATE, S5_ROW))
        w_oi.append((-cli[out_pow, direction]).transpose(1, 3, 0, 2).reshape(ngroups, S5_STATE, S5_ROW))
    stack = lambda xs: jnp.stack(xs).astype(BF16)
    return (stack(m_intra), stack(w_sr), stack(w_si), stack(w_or), stack(w_oi),
            pr[t_len], pi[t_len])


def _s5_kernel(u_ref, mi_ref, wsr_ref, wsi_ref, wor_ref, woi_ref, lr_ref, li_ref, y_ref,
               sr_scr, si_scr, hr_scr, hi_scr, *, nck, ncc, rows):
    groups = u_ref.shape[0]
    nvec = lr_ref.shape[1]
    for direction in range(2):
        for g in range(groups):
            u = u_ref[g]
            sr_scr[g * rows:(g + 1) * rows, :] = _dot(u, wsr_ref[direction, g])
            si_scr[g * rows:(g + 1) * rows, :] = _dot(u, wsi_ref[direction, g])
        lam_r = lr_ref[direction]
        lam_i = li_ref[direction]

        def step(j, carry, direction=direction, lam_r=lam_r, lam_i=lam_i):
            h_r, h_i = carry
            if direction == 0:
                k = j
            else:
                k = jnp.where(j < ncc, ncc - 1 - j, nck - 1 - j + ncc)
            idx = pl.ds(k, nvec, stride=nck)
            hr_scr[idx, :] = h_r
            hi_scr[idx, :] = h_i
            s_r = sr_scr[idx, :]
            s_i = si_scr[idx, :]
            return (lam_r * h_r - lam_i * h_i + s_r, lam_r * h_i + lam_i * h_r + s_i)

        zero = jnp.zeros((nvec, S5_STATE), F32)
        lax.fori_loop(0, nck, step, (zero, zero))
        for g in range(groups):
            y = (_dot(u_ref[g], mi_ref[direction, g])
                 + _dot(hr_scr[g * rows:(g + 1) * rows, :].astype(BF16), wor_ref[direction, g])
                 + _dot(hi_scr[g * rows:(g + 1) * rows, :].astype(BF16), woi_ref[direction, g]))
            if direction == 0:
                y_ref[g] = y
            else:
                y_ref[g] += y


def _s5_core(u, tables, batch, nck, ncc):
    mi, wsr, wsi, wor, woi, l16r, l16i = tables
    ngroups, rows, _ = u.shape
    gs = S5_GROUPS_PER_STEP
    nvec = gs * batch
    lam_r = jnp.repeat(l16r, batch, axis=1).reshape(2, ngroups // gs, nvec, S5_STATE)
    lam_i = jnp.repeat(l16i, batch, axis=1).reshape(2, ngroups // gs, nvec, S5_STATE)
    lam_r = lam_r.transpose(1, 0, 2, 3)
    lam_i = lam_i.transpose(1, 0, 2, 3)
    kern = functools.partial(_s5_kernel, nck=nck, ncc=ncc, rows=rows)
    wspec = lambda a, b: pl.BlockSpec((2, gs, a, b), lambda i: (0, i, 0, 0))
    return pl.pallas_call(
        kern,
        grid=(ngroups // gs,),
        in_specs=[pl.BlockSpec((gs, rows, S5_ROW), lambda i: (i, 0, 0)),
                  wspec(S5_ROW, S5_ROW), wspec(S5_ROW, S5_STATE), wspec(S5_ROW, S5_STATE),
                  wspec(S5_STATE, S5_ROW), wspec(S5_STATE, S5_ROW),
                  pl.BlockSpec((None, 2, nvec, S5_STATE), lambda i: (i, 0, 0, 0)),
                  pl.BlockSpec((None, 2, nvec, S5_STATE), lambda i: (i, 0, 0, 0))],
        out_specs=pl.BlockSpec((gs, rows, S5_ROW), lambda i: (i, 0, 0)),
        out_shape=jax.ShapeDtypeStruct((ngroups, rows, S5_ROW), F32),
        scratch_shapes=[pltpu.VMEM((gs * rows, S5_STATE), F32) for _ in range(4)],
        compiler_params=_params("parallel"),
    )(u, mi, wsr, wsi, wor, woi, lam_r, lam_i)


def _s5_pack(h, dims, col_major):
    batch, lat_len, ctx_len, _ = dims
    lat = h[:batch * lat_len].reshape(batch, lat_len, D_MODEL)
    ctx = h[batch * lat_len:].reshape(batch, ctx_len, D_MODEL)
    if col_major:
        lat = lat.reshape(batch, lat_len // GRID_W, GRID_W, D_MODEL).transpose(0, 2, 1, 3).reshape(batch, lat_len, D_MODEL)

    def chunkify(t, n):
        t = t.reshape(batch, n // S5_CHUNK, S5_CHUNK, S5_GROUPS, S5_GROUP)
        return t.transpose(3, 0, 1, 2, 4).reshape(S5_GROUPS, batch, n // S5_CHUNK, S5_ROW)

    u = jnp.concatenate([chunkify(ctx, ctx_len), chunkify(lat, lat_len)], axis=2)
    return u.reshape(S5_GROUPS, -1, S5_ROW).astype(BF16)


def _s5_unpack(y, dims, col_major):
    batch, lat_len, ctx_len, _ = dims
    ncc = ctx_len // S5_CHUNK
    y = y.reshape(S5_GROUPS, batch, -1, S5_CHUNK, S5_GROUP)

    def unchunk(t, n):
        return t.transpose(1, 2, 3, 0, 4).reshape(batch, n, D_MODEL)

    ctx = unchunk(y[:, :, :ncc], ctx_len)
    lat = unchunk(y[:, :, ncc:], lat_len)
    if col_major:
        lat = lat.reshape(batch, GRID_W, lat_len // GRID_W, D_MODEL).transpose(0, 2, 1, 3).reshape(batch, lat_len, D_MODEL)
    return jnp.concatenate([lat.reshape(-1, D_MODEL), ctx.reshape(-1, D_MODEL)], axis=0)


def _s5_glu_kernel(h_ref, y_ref, dskip_ref, w1_ref, w2_ref, b1_ref, b2_ref, x_ref, gate_ref, o_ref, z_scr):
    @pl.when(pl.program_id(1) == 0)
    def _():
        y = h_ref[...] * dskip_ref[...] + y_ref[...]
        z = 0.5 * y * (1.0 + jnp.tanh(math.sqrt(2.0 / math.pi) * (y + 0.044715 * (y * y * y))))
        z_scr[...] = z.astype(BF16)

    z = z_scr[...]
    lin = _dot(z, w1_ref[...]) + b1_ref[...]
    gat = _dot(z, w2_ref[...]) + b2_ref[...]
    o_ref[...] = x_ref[...] + gate_ref[...] * (lin * _sigmoid(gat))


def _s5_glu(h, y, dskip, w, b, x, mods, layer, dims):
    batch, lat_len, _, tm = dims
    ntok, d = x.shape
    tn = 512
    nblk = d // tn
    tiles_per_seq = lat_len // tm
    b = b.reshape(1, 2 * d)
    return pl.pallas_call(
        _s5_glu_kernel,
        grid=(ntok // tm, nblk),
        in_specs=[pl.BlockSpec((tm, d), lambda i, j: (i, 0)),
                  pl.BlockSpec((tm, d), lambda i, j: (i, 0)),
                  pl.BlockSpec((1, d), lambda i, j: (0, 0)),
                  pl.BlockSpec((d, tn), lambda i, j: (0, j)),
                  pl.BlockSpec((d, tn), lambda i, j: (0, nblk + j)),
                  pl.BlockSpec((1, tn), lambda i, j: (0, j)),
                  pl.BlockSpec((1, tn), lambda i, j: (0, nblk + j)),
                  pl.BlockSpec((tm, tn), lambda i, j: (i, j)),
                  pl.BlockSpec((None, 1, tn), lambda i, j: (
                      layer * 8 + jnp.minimum(i // tiles_per_seq, batch), 0, 2 * nblk + j))],
        out_specs=pl.BlockSpec((tm, tn), lambda i, j: (i, j)),
        out_shape=jax.ShapeDtypeStruct((ntok, d), F32),
        scratch_shapes=[pltpu.VMEM((tm, d), BF16)],
        compiler_params=_params("parallel", "arbitrary"),
    )(h, y, dskip.reshape(1, d), w, w, b, b, x, mods)


def _s5_layer(x, mods, layer, j, dims, p):
    batch, lat_len, ctx_len, _ = dims
    col_major = (j % 2) == 1
    h = _prenorm(x, mods, p['norm_g'][layer, 0], layer, dims)
    tables = _s5_tables(p['s5_a_re'][j], p['s5_a_im'][j], p['s5_log_dt'][j], p['s5_b_re'][j],
                        p['s5_b_im'][j], p['s5_c_re'][j], p['s5_c_im'][j])
    ncc = ctx_len // S5_CHUNK
    nck = ncc + lat_len // S5_CHUNK
    y = _s5_core(_s5_pack(h, dims, col_major), tables, batch, nck, ncc)
    y = _s5_unpack(y, dims, col_major)
    return _s5_glu(h, y, p['s5_d'][j], p['s5_glu_w'][j].astype(BF16), p['s5_glu_b'][j], x, mods, layer, dims)


def _seq_blocks(dims):
    batch, lat_len, ctx_len, _ = dims
    return [(lat_len, 0), (ctx_len, batch * lat_len // ctx_len)]


def _dwconv3_kernel(u_ref, w_ref, b_ref, o_ref, *maybe_bf16_ref):
    x = u_ref[...]
    n = x.shape[0]
    row = lax.broadcasted_iota(jnp.int32, x.shape, 0)
    prev = jnp.where(row == 0, 0.0, pltpu.roll(x, 1, 0))
    nxt = jnp.where(row == n - 1, 0.0, pltpu.roll(x, n - 1, 0))
    y = prev * w_ref[0:1, :] + x * w_ref[1:2, :] + nxt * w_ref[2:3, :] + b_ref[...]
    o_ref[...] = y
    for r in maybe_bf16_ref:
        r[...] = y.astype(BF16)


def _dwconv3(u, w, b, dims, col0, ncols, with_bf16):
    batch = dims[0]
    ntok = u.shape[0]
    tc = 256
    cb0 = col0 // tc
    w8 = jnp.zeros((8, w.shape[1]), F32).at[:3].set(w)
    outs = []
    for n, off in _seq_blocks(dims):
        out_shape = [jax.ShapeDtypeStruct((batch * n, ncols), F32)]
        out_specs = [pl.BlockSpec((n, tc), lambda s, j: (s, j))]
        if with_bf16:
            out_shape.append(jax.ShapeDtypeStruct((batch * n, ncols), BF16))
            out_specs.append(pl.BlockSpec((n, tc), lambda s, j: (s, j)))
        outs.append(pl.pallas_call(
            _dwconv3_kernel,
            grid=(batch, ncols // tc),
            in_specs=[pl.BlockSpec((n, tc), lambda s, j, off=off: (off + s, cb0 + j)),
                      pl.BlockSpec((8, tc), lambda s, j: (0, cb0 + j)),
                      pl.BlockSpec((1, tc), lambda s, j: (0, cb0 + j))],
            out_specs=out_specs,
            out_shape=out_shape,
            compiler_params=_params("parallel", "parallel"),
        )(u, w8, b.reshape(1, -1)))
    return outs


def _dft_tables(n):
    k = jnp.arange(n, dtype=jnp.int32)[:, None]
    t = jnp.arange(n, dtype=jnp.int32)[None, :]
    m = ((2 * k + 1) * t) % (4 * n)
    ang = m.astype(F32) * (2.0 * math.pi / (4 * n))
    c, s = jnp.cos(ang), jnp.sin(ang)
    return c.astype(BF16), s.astype(BF16), c.T.astype(BF16), s.T.astype(BF16)


def _hy_filter_kernel(feat_ref, w1_ref, b1_ref, w2_ref, b2_ref, w3_ref, b3_ref, fq_ref, tu_ref, dl_ref,
                      w4_ref, hs_ref, hd_ref, hdn_scr):
    @pl.when(pl.program_id(0) == 0)
    def _():
        fq = fq_ref[...]
        h = jnp.sin(fq * (jnp.dot(feat_ref[...], w1_ref[...], precision=HIGHEST, preferred_element_type=F32) + b1_ref[...]))
        h = jnp.sin(fq * (jnp.dot(h, w2_ref[...], precision=HIGHEST, preferred_element_type=F32) + b2_ref[...]))
        h = jnp.sin(fq * (jnp.dot(h, w3_ref[...], precision=HIGHEST, preferred_element_type=F32) + b3_ref[...]))
        hdn_scr[...] = h

    hdn = hdn_scr[...]
    decay = jnp.exp(-tu_ref[...] * dl_ref[...])
    row = lax.broadcasted_iota(jnp.int32, decay.shape, 0)
    for order in range(HY_ORDER):
        fwd = jnp.dot(hdn, w4_ref[2 * order], precision=HIGHEST, preferred_element_type=F32) * decay
        bwd = jnp.dot(hdn, w4_ref[2 * order + 1], precision=HIGHEST, preferred_element_type=F32) * decay
        bwd = jnp.where(row == 0, 0.0, bwd)
        norm = jnp.sum(jnp.abs(fwd), axis=0, keepdims=True) + jnp.sum(jnp.abs(bwd), axis=0, keepdims=True)
        hs_ref[order] = ((fwd + bwd) / norm).astype(BF16)
        hd_ref[order] = ((fwd - bwd) / norm).astype(BF16)


def _hy_filters(n, p, j):
    d = D_MODEL
    t = jnp.arange(n, dtype=F32)
    t_unit = t / max(n - 1, 1)
    bands = jnp.linspace(1e-4, HY_BANDS - 1, HY_BANDS, dtype=F32)
    ang = (2.0 * math.pi / n) * t[:, None] * bands[None, :]
    feats = jnp.concatenate([t_unit[:, None], jnp.cos(ang), -jnp.sin(ang)], axis=-1)
    feats = jnp.pad(feats, ((0, 0), (0, HY_EMB_PAD - HY_EMB)))
    w1 = jnp.pad(p['hy_f_w1'][j], ((0, HY_EMB_PAD - HY_EMB), (0, 0)))
    deltas = jnp.abs(jnp.linspace(HY_MIN_DECAY, HY_MAX_DECAY, d, dtype=F32)).reshape(1, d)
    w4 = p['hy_f_w4'][j].reshape(HY_FILTER_HIDDEN, 2 * HY_ORDER, d).transpose(1, 0, 2)
    hid = HY_FILTER_HIDDEN
    tc = 128
    full = lambda shape: pl.BlockSpec(shape, lambda c: tuple(0 for _ in shape))
    row = lambda v: v.reshape(1, hid)
    return pl.pallas_call(
        _hy_filter_kernel,
        grid=(d // tc,),
        in_specs=[full((n, HY_EMB_PAD)), full((HY_EMB_PAD, hid)), full((1, hid)), full((hid, hid)), full((1, hid)),
                  full((hid, hid)), full((1, hid)), full((1, hid)), full((n, 1)),
                  pl.BlockSpec((1, tc), lambda c: (0, c)),
                  pl.BlockSpec((2 * HY_ORDER, hid, tc), lambda c: (0, 0, c))],
        out_specs=[pl.BlockSpec((HY_ORDER, n, tc), lambda c: (0, 0, c)),
                   pl.BlockSpec((HY_ORDER, n, tc), lambda c: (0, 0, c))],
        out_shape=[jax.ShapeDtypeStruct((HY_ORDER, n, d), BF16), jax.ShapeDtypeStruct((HY_ORDER, n, d), BF16)],
        scratch_shapes=[pltpu.VMEM((n, hid), F32)],
        compiler_params=_params("arbitrary"),
    )(feats, w1, row(p['hy_f_b1'][j]), p['hy_f_w2'][j], row(p['hy_f_b2'][j]), p['hy_f_w3'][j],
      row(p['hy_f_b3'][j]), row(p['hy_f_freq'][j]), t_unit.reshape(n, 1), deltas, w4)


def _dft_filter_kernel(c_ref, s_ref, hs_ref, hd_ref, hr_ref, hi_ref):
    hr_ref[...] = _dot(c_ref[...], hs_ref[...])
    hi_ref[...] = -_dot(s_ref[...], hd_ref[...])


def _dft_filter(cmat, smat, hs, hd, tk, td):
    n, d = hs.shape[1], hs.shape[2]
    return pl.pallas_call(
        _dft_filter_kernel,
        grid=(n // tk, HY_ORDER, d // td),
        in_specs=[pl.BlockSpec((tk, n), lambda i, o, j: (i, 0)),
                  pl.BlockSpec((tk, n), lambda i, o, j: (i, 0)),
                  pl.BlockSpec((None, n, td), lambda i, o, j: (o, 0, j)),
                  pl.BlockSpec((None, n, td), lambda i, o, j: (o, 0, j))],
        out_specs=[pl.BlockSpec((None, tk, td), lambda i, o, j: (o, i, j)),
                   pl.BlockSpec((None, tk, td), lambda i, o, j: (o, i, j))],
        out_shape=[jax.ShapeDtypeStruct((HY_ORDER, n, d), F32), jax.ShapeDtypeStruct((HY_ORDER, n, d), F32)],
        compiler_params=_params("parallel", "parallel", "parallel"),
    )(cmat, smat, hs, hd)


def _dft_fwd_kernel(c_ref, s_ref, z_ref, hr_ref, hi_ref, yr_ref, yi_ref):
    z = z_ref[...]
    zc = _dot(c_ref[...], z)
    zs = _dot(s_ref[...], z)
    hr, hi = hr_ref[...], hi_ref[...]
    yr_ref[...] = (zc * hr + zs * hi).astype(BF16)
    yi_ref[...] = (zc * hi - zs * hr).astype(BF16)


def _dft_fwd(cmat, smat, z, hr, hi, order, batch, tk, td):
    n = cmat.shape[0]
    d = z.shape[1]
    return pl.pallas_call(
        _dft_fwd_kernel,
        grid=(n // tk, batch, d // td),
        in_specs=[pl.BlockSpec((tk, n), lambda i, s, j: (i, 0)),
                  pl.BlockSpec((tk, n), lambda i, s, j: (i, 0)),
                  pl.BlockSpec((n, td), lambda i, s, j: (s, j)),
                  pl.BlockSpec((None, tk, td), lambda i, s, j: (order, i, j)),
                  pl.BlockSpec((None, tk, td), lambda i, s, j: (order, i, j))],
        out_specs=[pl.BlockSpec((None, tk, td), lambda i, s, j: (s, i, j)),
                   pl.BlockSpec((None, tk, td), lambda i, s, j: (s, i, j))],
        out_shape=[jax.ShapeDtypeStruct((batch, n, d), BF16), jax.ShapeDtypeStruct((batch, n, d), BF16)],
        compiler_params=_params("parallel", "parallel", "parallel"),
    )(cmat, smat, z, hr, hi)


def _dft_inv_kernel(ct_ref, st_ref, yr_ref, yi_ref, xg_ref, zp_ref, bias_ref, o_ref, obf_ref, *, inv_n):
    y = (_dot(ct_ref[...], yr_ref[...]) - _dot(st_ref[...], yi_ref[...])) * inv_n
    out = xg_ref[...] * (y + zp_ref[...] * bias_ref[...])
    o_ref[...] = out
    obf_ref[...] = out.astype(BF16)


def _dft_inv(ctm, stm, yr, yi, xg, xg_col0, zprev, bias, batch, tt, td):
    n = ctm.shape[0]
    d = zprev.shape[1]
    cb0 = xg_col0 // td
    kern = functools.partial(_dft_inv_kernel, inv_n=1.0 / n)
    return pl.pallas_call(
        kern,
        grid=(n // tt, batch, d // td),
        in_specs=[pl.BlockSpec((tt, n), lambda i, s, j: (i, 0)),
                  pl.BlockSpec((tt, n), lambda i, s, j: (i, 0)),
                  pl.BlockSpec((None, n, td), lambda i, s, j: (s, 0, j)),
                  pl.BlockSpec((None, n, td), lambda i, s, j: (s, 0, j)),
                  pl.BlockSpec((tt, td), lambda i, s, j: (s * (n // tt) + i, cb0 + j)),
                  pl.BlockSpec((tt, td), lambda i, s, j: (s * (n // tt) + i, j)),
                  pl.BlockSpec((1, td), lambda i, s, j: (0, j))],
        out_specs=[pl.BlockSpec((tt, td), lambda i, s, j: (s * (n // tt) + i, j)),
                   pl.BlockSpec((tt, td), lambda i, s, j: (s * (n // tt) + i, j))],
        out_shape=[jax.ShapeDtypeStruct((batch * n, d), F32), jax.ShapeDtypeStruct((batch * n, d), BF16)],
        compiler_params=_params("parallel", "parallel", "parallel"),
    )(ctm, stm, yr, yi, xg, zprev, bias.reshape(1, d))


def _hyena_layer(x, mods, layer, j, dims, p):
    batch, lat_len, ctx_len, _ = dims
    d = D_MODEL
    u = _mm_norm(x, mods, p['norm_g'][layer, 0], p['hy_in_w'][j].astype(BF16), p['hy_in_b'][j], layer, dims, 512)
    v_parts = _dwconv3(u, p['hy_conv_w'][j], p['hy_conv_b'][j], dims, 0, d, True)
    g_parts = _dwconv3(u, p['hy_conv_w'][j], p['hy_conv_b'][j], dims, d, 2 * d, False)
    z_out = []
    for (n, _), (v32, vbf), (gates,) in zip(_seq_blocks(dims), v_parts, g_parts):
        tk = min(512, n)
        td = 512
        cmat, smat, ctm, stm = _dft_tables(n)
        hs, hd = _hy_filters(n, p, j)
        hr, hi = _dft_filter(cmat, smat, hs, hd, tk, td)
        zprev32, zprevbf = v32, vbf
        for order in range(HY_ORDER):
            yr, yi = _dft_fwd(cmat, smat, zprevbf, hr, hi, order, batch, tk, td)
            zprev32, zprevbf = _dft_inv(ctm, stm, yr, yi, gates, order * d, zprev32, p['hy_bias'][j, order],
                                        batch, tk, td)
        z_out.append(zprevbf)
    z = jnp.concatenate(z_out, axis=0)
    return _mm_res(z, p['hy_out_w'][j].astype(BF16), p['hy_out_b'][j], x, mods, layer, dims, 512)


def _gdn_conv_kernel(p_ref, w_ref, o_ref):
    x = p_ref[...]
    n = x.shape[0]
    row = lax.broadcasted_iota(jnp.int32, x.shape, 0)
    acc = x * w_ref[2:3, :]
    for s in (1, 2):
        prev = jnp.where(row < s, 0.0, pltpu.roll(x, s, 0))
        nxt = jnp.where(row >= n - s, 0.0, pltpu.roll(x, n - s, 0))
        acc = acc + prev * w_ref[2 - s:3 - s, :] + nxt * w_ref[2 + s:3 + s, :]
    y = _silu(acc)
    head = pl.program_id(1)
    inv = lax.rsqrt(jnp.sum(y * y, axis=-1, keepdims=True) + 1e-6)
    inv = inv * jnp.where(head < GDN_K_HEADS, GDN_HEAD ** -0.5, 1.0)
    o_ref[...] = y * jnp.where(head < 2 * GDN_K_HEADS, inv, 1.0)


def _gdn_conv(proj, w, dims):
    batch = dims[0]
    ntok = proj.shape[0]
    nheads = GDN_CONV_DIM // GDN_HEAD
    w8 = jnp.zeros((8, GDN_CONV_DIM), F32).at[:5].set(w)
    out = None
    for n, off in _seq_blocks(dims):
        args = [proj, w8]
        in_specs = [pl.BlockSpec((n, GDN_HEAD), lambda s, h, off=off: (off + s, h)),
                    pl.BlockSpec((8, GDN_HEAD), lambda s, h: (0, h))]
        aliases = {}
        if out is not None:
            args.append(out)
            in_specs.append(pl.BlockSpec(memory_space=pl.ANY))
            aliases = {2: 0}
        kern = _gdn_conv_kernel if out is None else (lambda p_ref, w_ref, _, o_ref: _gdn_conv_kernel(p_ref, w_ref, o_ref))
        out = pl.pallas_call(
            kern,
            grid=(batch, nheads),
            in_specs=in_specs,
            out_specs=pl.BlockSpec((None, n, GDN_HEAD), lambda s, h, off=off: (h, off + s, 0)),
            out_shape=jax.ShapeDtypeStruct((nheads, ntok, GDN_HEAD), F32),
            input_output_aliases=aliases,
            compiler_params=_params("parallel", "parallel"),
        )(*args)
    return out


def _gdn_gate_kernel(ab_ref, alog_ref, dtb_ref, o_ref):
    c = GDN_CHUNK
    row = lax.broadcasted_iota(jnp.int32, (c, c), 0)
    col = lax.broadcasted_iota(jnp.int32, (c, c), 1)
    lower = jnp.where(row >= col, 1.0, 0.0).astype(F32)
    upper = jnp.where(row <= col, 1.0, 0.0).astype(F32)
    lane = lax.broadcasted_iota(jnp.int32, (c, 128), 1)
    for r in range(ab_ref.shape[0] // c):
        ab = ab_ref[r * c:(r + 1) * c, :]
        xs = ab + dtb_ref[...]
        softplus = jnp.maximum(xs, 0.0) + jnp.log(1.0 + jnp.exp(-jnp.abs(xs)))
        g = -jnp.exp(alog_ref[...]) * softplus
        cum_f = jnp.dot(lower, g, precision=HIGHEST, preferred_element_type=F32)
        cum_b = jnp.dot(upper, g, precision=HIGHEST, preferred_element_type=F32)
        o_ref[r * c:(r + 1) * c, :] = jnp.where(lane < GDN_V_HEADS, cum_f,
                                                jnp.where(lane < 2 * GDN_V_HEADS, cum_b, _sigmoid(ab)))


def _gdn_gates(ab, a_log, dt_bias, tm):
    ntok = ab.shape[0]
    pad = jnp.zeros((2 * GDN_V_HEADS,), F32)
    alog = jnp.concatenate([a_log.reshape(-1), pad]).reshape(1, 128)
    dtb = jnp.concatenate([dt_bias.reshape(-1), pad]).reshape(1, 128)
    return pl.pallas_call(
        _gdn_gate_kernel,
        grid=(ntok // tm,),
        in_specs=[pl.BlockSpec((tm, 128), lambda i: (i, 0)),
                  pl.BlockSpec((1, 128), lambda i: (0, 0)),
                  pl.BlockSpec((1, 128), lambda i: (0, 0))],
        out_specs=pl.BlockSpec((tm, 128), lambda i: (i, 0)),
        out_shape=jax.ShapeDtypeStruct((ntok, 128), F32),
        compiler_params=_params("parallel"),
    )(ab, alog, dtb)


def _unit_triangular_inverse(a):
    c = a.shape[0]
    row = lax.broadcasted_iota(jnp.int32, (c, c), 0)
    col = lax.broadcasted_iota(jnp.int32, (c, c), 1)
    eye = jnp.where(row == col, 1.0, 0.0).astype(F32)
    t = eye - a
    pw = a
    for _ in range(int(math.log2(c)) - 1):
        pwb = pw.astype(BF16)
        pw = _dot(pwb, pwb)
        t = t + _dot(t.astype(BF16), pw.astype(BF16))
    m = eye + a
    m_hi = m.astype(BF16)
    m_lo = (m - m_hi.astype(F32)).astype(BF16)
    t_hi = t.astype(BF16)
    t_lo = (t - t_hi.astype(F32)).astype(BF16)
    resid = eye - (_dot(m_hi, t_hi) + (_dot(m_hi, t_lo) + _dot(m_lo, t_hi)))
    return t + _dot(t_hi, resid.astype(BF16))


def _gdn_chunk_kernel(qf_ref, kf_ref, vf_ref, gcf_ref, grf_ref, qb_ref, kb_ref, vb_ref, gcb_ref, grb_ref,
                      of_ref, ob_ref, state_scr):
    c = GDN_CHUNK

    @pl.when(pl.program_id(2) == 0)
    def _():
        state_scr[...] = jnp.zeros_like(state_scr)

    row = lax.broadcasted_iota(jnp.int32, (c, c), 0)
    col = lax.broadcasted_iota(jnp.int32, (c, c), 1)
    blocks = ((qf_ref, kf_ref, vf_ref, gcf_ref, grf_ref, of_ref), (qb_ref, kb_ref, vb_ref, gcb_ref, grb_ref, ob_ref))
    for direction, (q_ref, k_ref, v_ref, gc_ref, gr_ref, o_ref) in enumerate(blocks):
        q = q_ref[...].astype(BF16)
        k32 = k_ref[...]
        k = k32.astype(BF16)
        gates_c = gc_ref[...]
        gates_r = gr_ref[...]
        if direction == 0:
            incl, strict, last = row >= col, row > col, c - 1
        else:
            incl, strict, last = row <= col, row < col, 0
        kk = _dot_nt(k, k)
        qk = _dot_nt(q, k)
        for e in range(2):
            ch_g, ch_b = direction * 2 + e, 4 + direction * 2 + e
            gcc, gcr = gates_c[:, ch_g:ch_g + 1], gates_r[ch_g:ch_g + 1, :]
            beta_c, beta_r = gates_c[:, ch_b:ch_b + 1], gates_r[ch_b:ch_b + 1, :]
            decay = jnp.where(incl, jnp.exp(jnp.where(incl, gcc - gcr, 0.0)), 0.0)
            a = jnp.where(strict, kk * beta_c * decay, 0.0)
            t = _unit_triangular_inverse(a)
            attn = (qk * decay).astype(BF16)
            v = v_ref[e].astype(BF16)
            u = _dot((t * beta_r).astype(BF16), v)
            w = _dot((t * (beta_r * jnp.exp(gcr))).astype(BF16), k)
            slot = direction * 2 + e
            state = state_scr[slot]
            state_b = state.astype(BF16)
            v_new = u - _dot(w.astype(BF16), state_b)
            v_new_b = v_new.astype(BF16)
            o_ref[e] = jnp.exp(gcc) * _dot(q, state_b) + _dot(attn, v_new_b)
            g_last = gcr[:, last:last + 1]
            kg = (k32 * jnp.exp(g_last - gcc)).astype(BF16)
            state_scr[slot] = state * jnp.exp(g_last) + _dot_tn(kg, v_new_b)


def _gdn_chunks(qkvh, gates, dims):
    batch, lat_len, ctx_len, _ = dims
    c = GDN_CHUNK
    ntok = qkvh.shape[1]
    ncc, ncl = ctx_len // c, lat_len // c
    nck = ncc + ncl
    ctx0 = batch * lat_len // c
    gk = gates.reshape(ntok // c, c, 4, GDN_K_HEADS, 2).transpose(3, 0, 1, 2, 4).reshape(GDN_K_HEADS, ntok // c, c, 8)
    gk_t = gk.transpose(0, 1, 3, 2)

    def fwd_blk(b, j):
        return jnp.where(j < ncc, ctx0 + b * ncc + j, b * ncl + j - ncc)

    def bwd_blk(b, j):
        return jnp.where(j < ncc, ctx0 + b * ncc + ncc - 1 - j, b * ncl + ncl - 1 - (j - ncc))

    def specs(blk):
        return [pl.BlockSpec((None, c, GDN_HEAD), lambda b, h, j: (h, blk(b, j), 0)),
                pl.BlockSpec((None, c, GDN_HEAD), lambda b, h, j: (GDN_K_HEADS + h, blk(b, j), 0)),
                pl.BlockSpec((2, c, GDN_HEAD), lambda b, h, j: (GDN_K_HEADS + h, blk(b, j), 0)),
                pl.BlockSpec((None, None, c, 8), lambda b, h, j: (h, blk(b, j), 0, 0)),
                pl.BlockSpec((None, None, 8, c), lambda b, h, j: (h, blk(b, j), 0, 0))]

    out_sd = jax.ShapeDtypeStruct((GDN_V_HEADS, ntok, GDN_HEAD), F32)
    return pl.pallas_call(
        _gdn_chunk_kernel,
        grid=(batch, GDN_K_HEADS, nck),
        in_specs=specs(fwd_blk) + specs(bwd_blk),
        out_specs=[pl.BlockSpec((2, c, GDN_HEAD), lambda b, h, j: (h, fwd_blk(b, j), 0)),
                   pl.BlockSpec((2, c, GDN_HEAD), lambda b, h, j: (h, bwd_blk(b, j), 0))],
        out_shape=[out_sd, out_sd],
        scratch_shapes=[pltpu.VMEM((4, GDN_HEAD, GDN_HEAD), F32)],
        compiler_params=_params("parallel", "parallel", "arbitrary"),
    )(qkvh, qkvh, qkvh, gk, gk_t, qkvh, qkvh, qkvh, gk, gk_t)


def _gdn_out_kernel(of_ref, ob_ref, z_ref, ng_ref, w_ref, x_ref, gate_ref, o_ref, a_scr):
    @pl.when(pl.program_id(1) == 0)
    def _():
        ng = ng_ref[...]
        for h in range(GDN_V_HEADS):
            o = of_ref[h] + ob_ref[h]
            o = o * lax.rsqrt(jnp.mean(o * o, axis=-1, keepdims=True) + EPS)
            z = z_ref[:, h * GDN_HEAD:(h + 1) * GDN_HEAD]
            a_scr[:, h * GDN_HEAD:(h + 1) * GDN_HEAD] = (o * ng * _silu(z)).astype(BF16)

    o_ref[...] = x_ref[...] + gate_ref[...] * _dot(a_scr[...], w_ref[...])


def _gdn_out(o_f, o_b, proj, norm_g, w, x, mods, layer, dims):
    batch, lat_len, _, _ = dims
    tm = min(256, dims[3])
    ntok, d = x.shape
    tn = 512
    gate_blk = 2 * d // tn
    tiles_per_seq = lat_len // tm
    zblk = GDN_CONV_DIM // GDN_V
    return pl.pallas_call(
        _gdn_out_kernel,
        grid=(ntok // tm, d // tn),
        in_specs=[pl.BlockSpec((GDN_V_HEADS, tm, GDN_HEAD), lambda i, j: (0, i, 0)),
                  pl.BlockSpec((GDN_V_HEADS, tm, GDN_HEAD), lambda i, j: (0, i, 0)),
                  pl.BlockSpec((tm, GDN_V), lambda i, j: (i, zblk)),
                  pl.BlockSpec((1, GDN_HEAD), lambda i, j: (0, 0)),
                  pl.BlockSpec((GDN_V, tn), lambda i, j: (0, j)),
                  pl.BlockSpec((tm, tn), lambda i, j: (i, j)),
                  pl.BlockSpec((None, 1, tn), lambda i, j: (
                      layer * 8 + jnp.minimum(i // tiles_per_seq, batch), 0, gate_blk + j))],
        out_specs=pl.BlockSpec((tm, tn), lambda i, j: (i, j)),
        out_shape=jax.ShapeDtypeStruct((ntok, d), F32),
        scratch_shapes=[pltpu.VMEM((tm, GDN_V), BF16)],
        compiler_params=_params("parallel", "arbitrary"),
    )(o_f, o_b, proj, norm_g.reshape(1, GDN_HEAD), w, x, mods)


def _gdn_layer(x, mods, layer, j, dims, p):
    in_w = p['gdn_in_w'][j]
    g = p['norm_g'][layer, 0]
    proj = _mm_norm(x, mods, g, in_w[:, :GDN_MAIN].astype(BF16), jnp.zeros((GDN_MAIN,), F32), layer, dims, 512)
    ab = _mm_norm(x, mods, g, in_w[:, GDN_MAIN:].astype(BF16), jnp.zeros((128,), F32), layer, dims, 128)
    qkvh = _gdn_conv(proj, p['gdn_conv_w'][j], dims)
    gates = _gdn_gates(ab, p['gdn_a_log'][j], p['gdn_dt_bias'][j], dims[3])
    o_f, o_b = _gdn_chunks(qkvh, gates, dims)
    return _gdn_out(o_f, o_b, proj, p['gdn_norm_g'][j], p['gdn_out_w'][j].astype(BF16), x, mods, layer, dims)


def kernel(x, c, ctx, c_ctx, ada_w, ada_b, norm_g, final_g, ffn_w_gate, ffn_w_up, ffn_w_down, s5_a_re, s5_a_im, s5_log_dt, s5_b_re, s5_b_im, s5_c_re, s5_c_im, s5_d, s5_glu_w, s5_glu_b, hy_in_w, hy_in_b, hy_conv_w, hy_conv_b, hy_f_w1, hy_f_b1, hy_f_w2, hy_f_b2, hy_f_w3, hy_f_b3, hy_f_w4, hy_f_freq, hy_bias, hy_out_w, hy_out_b, gdn_in_w, gdn_conv_w, gdn_a_log, gdn_dt_bias, gdn_norm_g, gdn_out_w):
    p = dict(locals())
    batch, lat_len, d = x.shape
    ctx_len = ctx.shape[1]
    depth = ada_w.shape[0]
    tm = min(512, batch * ctx_len)
    dims = (batch, lat_len, ctx_len, tm)
    assert d == D_MODEL and lat_len % tm == 0 and (batch * ctx_len) % tm == 0 and batch + 1 <= 8

    cvec = jnp.zeros((8, d), F32).at[:batch].set(c).at[batch].set(c_ctx)
    mods = _ada_all(cvec, ada_w, ada_b).reshape(depth * 8, 1, 6 * d)
    tok = jnp.concatenate([x.reshape(batch * lat_len, d), ctx.reshape(batch * ctx_len, d)], axis=0)
    for i in range(depth):
        kind, j = i % N_MIXERS, i // N_MIXERS
        if kind == 0:
            tok = _s5_layer(tok, mods, i, j, dims, p)
        elif kind == 1:
            tok = _hyena_layer(tok, mods, i, j, dims, p)
        else:
            tok = _gdn_layer(tok, mods, i, j, dims, p)
        tok = _ffn(tok, mods, norm_g[i, 1], ffn_w_gate[i].astype(BF16), ffn_w_up[i].astype(BF16),
                   ffn_w_down[i].astype(BF16), i, dims)
    out = _final_norm(tok, final_g, batch * lat_len, tm)
    return out.reshape(batch, lat_len, d)
```

```python
import functools
import math

import jax
import jax.numpy as jnp
from jax import lax
from jax.experimental import pallas as pl
from jax.experimental.pallas import tpu as pltpu

F32 = jnp.float32
BF16 = jnp.bfloat16
HIGHEST = lax.Precision.HIGHEST

D_MODEL = 2048
GRID_W = 64
EPS = 1e-6
N_MIXERS = 3

S5_GROUP = 16
S5_STATE = 64
S5_GROUPS = D_MODEL // S5_GROUP
S5_CHUNK = 16
S5_ROW = S5_CHUNK * S5_GROUP
S5_SLAB_GROUPS = 128 // S5_GROUP
S5_SLAB = S5_SLAB_GROUPS * S5_ROW

HY_ORDER = 2
HY_BANDS = 16
HY_EMB = 1 + 2 * HY_BANDS
HY_EMB_PAD = 128
HY_FILTER_HIDDEN = 64
HY_DECAY_TARGET = 1e-2
HY_MAX_DECAY = math.log(HY_DECAY_TARGET) / 0.3
HY_MIN_DECAY = math.log(HY_DECAY_TARGET) / 1.5

GDN_K_HEADS = 16
GDN_V_HEADS = 32
GDN_HEAD = 128
GDN_QK = GDN_K_HEADS * GDN_HEAD
GDN_V = GDN_V_HEADS * GDN_HEAD
GDN_CONV_DIM = 2 * GDN_QK + GDN_V
GDN_MAIN = GDN_CONV_DIM + GDN_V
GDN_CHUNK = 64
GDN_KHEADS_PER_STEP = 2

VMEM_LIMIT_BYTES = 56 * 1024 * 1024


def _params(*sem):
    return pltpu.CompilerParams(dimension_semantics=sem, vmem_limit_bytes=VMEM_LIMIT_BYTES)


def _dot(a, b):
    return jnp.dot(a, b, preferred_element_type=F32)


def _dot_nt(a, b):
    return lax.dot_general(a, b, (((1,), (1,)), ((), ())), preferred_element_type=F32)


def _dot_tn(a, b):
    return lax.dot_general(a, b, (((0,), (0,)), ((), ())), preferred_element_type=F32)


def _sigmoid(x):
    return 1.0 / (1.0 + jnp.exp(-x))


def _silu(x):
    return x * _sigmoid(x)


def _norm_mod(x, g, sc, sh):
    y = x * lax.rsqrt(jnp.mean(x * x, axis=-1, keepdims=True) + EPS) * g
    return y * (1.0 + sc) + sh


def _mod_spec(layer, tm, lat_len, batch, width, col_block):
    tiles_per_seq = lat_len // tm

    def index(i, *_):
        return (layer * 8 + jnp.minimum(i // tiles_per_seq, batch), 0, col_block)

    return pl.BlockSpec((None, 1, width), index)


def _ada_kernel(c_ref, w_ref, b_ref, o_ref):
    c = c_ref[...]
    s = _silu(c).astype(BF16)
    o_ref[...] = _dot(s, w_ref[...].astype(BF16)) + b_ref[...]


def _ada_all(cvec, ada_w, ada_b):
    depth, d, n6 = ada_w.shape
    tn = 1024
    return pl.pallas_call(
        _ada_kernel,
        grid=(depth, n6 // tn),
        in_specs=[pl.BlockSpec((8, d), lambda l, j: (0, 0)),
                  pl.BlockSpec((None, d, tn), lambda l, j: (l, 0, j)),
                  pl.BlockSpec((None, 1, tn), lambda l, j: (l, 0, j))],
        out_specs=pl.BlockSpec((None, 8, tn), lambda l, j: (l, 0, j)),
        out_shape=jax.ShapeDtypeStruct((depth, 8, n6), F32),
        compiler_params=_params("parallel", "parallel"),
    )(cvec, ada_w, ada_b.reshape(depth, 1, n6))


def _prenorm_kernel(x_ref, mod_ref, g_ref, o_ref):
    d = D_MODEL
    o_ref[...] = _norm_mod(x_ref[...], g_ref[...], mod_ref[:, d:2 * d], mod_ref[:, 0:d])


def _prenorm(x, mods, g, layer, dims):
    batch, lat_len, _, tm = dims
    ntok, d = x.shape
    return pl.pallas_call(
        _prenorm_kernel,
        grid=(ntok // tm,),
        in_specs=[pl.BlockSpec((tm, d), lambda i: (i, 0)),
                  _mod_spec(layer, tm, lat_len, batch, 6 * d, 0),
                  pl.BlockSpec((1, d), lambda i: (0, 0))],
        out_specs=pl.BlockSpec((tm, d), lambda i: (i, 0)),
        out_shape=jax.ShapeDtypeStruct((ntok, d), F32),
        compiler_params=_params("parallel"),
    )(x, mods, g.reshape(1, d))


def _mm_norm_kernel(x_ref, mod_ref, g_ref, w_ref, b_ref, o_ref, h_scr):
    d = D_MODEL

    @pl.when(pl.program_id(1) == 0)
    def _():
        h_scr[...] = _norm_mod(x_ref[...], g_ref[...], mod_ref[:, d:2 * d], mod_ref[:, 0:d]).astype(BF16)

    o_ref[...] = _dot(h_scr[...], w_ref[...]) + b_ref[...]


def _mm_norm(x, mods, g, w, b, layer, dims, tn):
    batch, lat_len, _, tm = dims
    ntok, d = x.shape
    n = w.shape[1]
    return pl.pallas_call(
        _mm_norm_kernel,
        grid=(ntok // tm, n // tn),
        in_specs=[pl.BlockSpec((tm, d), lambda i, j: (i, 0)),
                  _mod_spec(layer, tm, lat_len, batch, 6 * d, 0),
                  pl.BlockSpec((1, d), lambda i, j: (0, 0)),
                  pl.BlockSpec((d, tn), lambda i, j: (0, j)),
                  pl.BlockSpec((1, tn), lambda i, j: (0, j))],
        out_specs=pl.BlockSpec((tm, tn), lambda i, j: (i, j)),
        out_shape=jax.ShapeDtypeStruct((ntok, n), F32),
        scratch_shapes=[pltpu.VMEM((tm, d), BF16)],
        compiler_params=_params("parallel", "arbitrary"),
    )(x, mods, g.reshape(1, d), w, b.reshape(1, n))


def _mm_res_kernel(a_ref, w_ref, b_ref, x_ref, gate_ref, o_ref):
    y = _dot(a_ref[...], w_ref[...]) + b_ref[...]
    o_ref[...] = x_ref[...] + gate_ref[...] * y


def _mm_res(a, w, b, x, mods, layer, dims, tn):
    batch, lat_len, _, tm = dims
    ntok, d = x.shape
    k = a.shape[1]
    gate_blk = 2 * d // tn
    tiles_per_seq = lat_len // tm
    return pl.pallas_call(
        _mm_res_kernel,
        grid=(ntok // tm, d // tn),
        in_specs=[pl.BlockSpec((tm, k), lambda i, j: (i, 0)),
                  pl.BlockSpec((k, tn), lambda i, j: (0, j)),
                  pl.BlockSpec((1, tn), lambda i, j: (0, j)),
                  pl.BlockSpec((tm, tn), lambda i, j: (i, j)),
                  pl.BlockSpec((None, 1, tn), lambda i, j: (
                      layer * 8 + jnp.minimum(i // tiles_per_seq, batch), 0, gate_blk + j))],
        out_specs=pl.BlockSpec((tm, tn), lambda i, j: (i, j)),
        out_shape=jax.ShapeDtypeStruct((ntok, d), F32),
        compiler_params=_params("parallel", "parallel"),
    )(a, w, b.reshape(1, d), x, mods)


def _ffn_kernel(x_ref, mod_ref, g_ref, wg_ref, wu_ref, wd_ref, o_ref, h_scr, acc_scr):
    d = D_MODEL
    f = pl.program_id(1)

    @pl.when(f == 0)
    def _():
        h_scr[...] = _norm_mod(x_ref[...], g_ref[...], mod_ref[:, 4 * d:5 * d],
                               mod_ref[:, 3 * d:4 * d]).astype(BF16)
        acc_scr[...] = jnp.zeros_like(acc_scr)

    h = h_scr[...]
    gate = _dot(h, wg_ref[...])
    up = _dot(h, wu_ref[...])
    act = (_silu(gate) * up).astype(BF16)
    acc_scr[...] += _dot(act, wd_ref[...])

    @pl.when(f == pl.num_programs(1) - 1)
    def _():
        o_ref[...] = x_ref[...] + mod_ref[:, 5 * d:6 * d] * acc_scr[...]


def _ffn(x, mods, g, wg, wu, wd, layer, dims):
    batch, lat_len, _, tm = dims
    ntok, d = x.shape
    dff = wg.shape[1]
    tf = 512
    return pl.pallas_call(
        _ffn_kernel,
        grid=(ntok // tm, dff // tf),
        in_specs=[pl.BlockSpec((tm, d), lambda i, f: (i, 0)),
                  _mod_spec(layer, tm, lat_len, batch, 6 * d, 0),
                  pl.BlockSpec((1, d), lambda i, f: (0, 0)),
                  pl.BlockSpec((d, tf), lambda i, f: (0, f)),
                  pl.BlockSpec((d, tf), lambda i, f: (0, f)),
                  pl.BlockSpec((tf, d), lambda i, f: (f, 0))],
        out_specs=pl.BlockSpec((tm, d), lambda i, f: (i, 0)),
        out_shape=jax.ShapeDtypeStruct((ntok, d), F32),
        scratch_shapes=[pltpu.VMEM((tm, d), BF16), pltpu.VMEM((tm, d), F32)],
        compiler_params=_params("parallel", "arbitrary"),
    )(x, mods, g.reshape(1, d), wg, wu, wd)


def _final_norm_kernel(x_ref, g_ref, o_ref):
    x = x_ref[...]
    o_ref[...] = x * lax.rsqrt(jnp.mean(x * x, axis=-1, keepdims=True) + EPS) * g_ref[...]


def _final_norm(x, g, nrows, tm):
    d = x.shape[1]
    return pl.pallas_call(
        _final_norm_kernel,
        grid=(nrows // tm,),
        in_specs=[pl.BlockSpec((tm, d), lambda i: (i, 0)), pl.BlockSpec((1, d), lambda i: (0, 0))],
        out_specs=pl.BlockSpec((tm, d), lambda i: (i, 0)),
        out_shape=jax.ShapeDtypeStruct((nrows, d), F32),
        compiler_params=_params("parallel"),
    )(x, g.reshape(1, d))


def _s5_tables(a_re, a_im, log_dt, b_re, b_im, c_re, c_im):
    t_len = S5_CHUNK
    dt = jnp.exp(log_dt)[..., None]
    lr, li = a_re * dt, a_im * dt
    er = jnp.exp(lr)
    nr, ni = er * jnp.cos(li) - 1.0, er * jnp.sin(li)
    den = a_re * a_re + a_im * a_im
    qr, qi = (nr * a_re + ni * a_im) / den, (ni * a_re - nr * a_im) / den
    bbr = qr[..., None] * b_re - qi[..., None] * b_im
    bbi = qr[..., None] * b_im + qi[..., None] * b_re
    tau0 = t_len - 1
    taus = jnp.arange(-tau0, t_len + 1, dtype=F32)[:, None, None, None]
    mag = jnp.exp(lr[None] * taus)
    pr, pi = mag * jnp.cos(li[None] * taus), mag * jnp.sin(li[None] * taus)
    clr = c_re[None] * pr[:, :, :, None, :] - c_im[None] * pi[:, :, :, None, :]
    cli = c_re[None] * pi[:, :, :, None, :] + c_im[None] * pr[:, :, :, None, :]
    ti = jnp.arange(t_len)
    ngroups = a_re.shape[1]
    w_sr, w_si, w_or, w_oi, w_nr, w_ni = [], [], [], [], [], []
    for direction in range(2):
        st_pow = tau0 + ((t_len - 1 - ti) if direction == 0 else ti)
        sr = pr[st_pow, direction][..., None] * bbr[direction][None] - pi[st_pow, direction][..., None] * bbi[direction][None]
        si = pr[st_pow, direction][..., None] * bbi[direction][None] + pi[st_pow, direction][..., None] * bbr[direction][None]
        w_sr.append(sr.transpose(1, 0, 3, 2).reshape(ngroups, S5_ROW, S5_STATE))
        w_si.append(si.transpose(1, 0, 3, 2).reshape(ngroups, S5_ROW, S5_STATE))
        out_pow = tau0 + ((ti + 1) if direction == 0 else (t_len - ti))
        neg_pow = tau0 + ((ti - (t_len - 1)) if direction == 0 else -ti)
        relay = lambda t: t.transpose(1, 3, 0, 2).reshape(ngroups, S5_STATE, S5_ROW)
        w_or.append(relay(clr[out_pow, direction]))
        w_oi.append(relay(-cli[out_pow, direction]))
        w_nr.append(relay(clr[neg_pow, direction]))
        w_ni.append(relay(-cli[neg_pow, direction]))
    return (jnp.stack(w_sr), jnp.stack(w_si), jnp.stack(w_nr), jnp.stack(w_ni),
            jnp.stack(w_or).astype(BF16), jnp.stack(w_oi).astype(BF16),
            pr[tau0 + t_len], pi[tau0 + t_len])


def _s5_perm():
    src = jnp.arange(S5_SLAB, dtype=jnp.int32)
    t, g, c = src // 128, (src % 128) // S5_GROUP, src % S5_GROUP
    dst = g * S5_ROW + t * S5_GROUP + c
    return (dst[:, None] == jnp.arange(S5_SLAB, dtype=jnp.int32)[None, :]).astype(BF16)


def _s5_kernel(*refs, nparts, batch, ncc, ncl, col_chunks):
    x_refs = refs[:nparts]
    perm_ref, wsr_ref, wsi_ref, wnr_ref, wni_ref, wor_ref, woi_ref, lr_ref, li_ref = refs[nparts:nparts + 9]
    y_refs = refs[nparts + 9:2 * nparts + 9]
    sr_scr, si_scr, hr_scr, hi_scr, y_scr = refs[2 * nparts + 9:]
    rows = y_scr.shape[0]
    groups = S5_SLAB_GROUPS

    def load(t):
        parts = [r[:, t].reshape(-1, 128) for r in x_refs]
        return (parts[0] if nparts == 1 else jnp.concatenate(parts, axis=0)).astype(BF16)

    xcat = jnp.concatenate([load(t) for t in range(S5_CHUNK)], axis=1)
    u_all = _dot(xcat, perm_ref[...]).astype(BF16)
    t_in = lax.broadcasted_iota(jnp.int32, (S5_ROW, S5_ROW), 0) // S5_GROUP
    t_out = lax.broadcasted_iota(jnp.int32, (S5_ROW, S5_ROW), 1) // S5_GROUP
    hdot = functools.partial(jnp.dot, precision=HIGHEST, preferred_element_type=F32)

    def row_of(b, j, direction):
        if direction == 0:
            kc, kl = j, j - ncc
        else:
            kc, kl = ncc - 1 - j, ncl - 1 - (j - ncc)
        if col_chunks:
            kl = (kl % col_chunks) * GRID_W + kl // col_chunks
        return jnp.where(j < ncc, batch * ncl + b * ncc + kc, b * ncl + kl)

    for direction in range(2):
        for g in range(groups):
            u = u_all[:, g * S5_ROW:(g + 1) * S5_ROW]
            sr_scr[g * rows:(g + 1) * rows, :] = _dot(u, wsr_ref[direction, g].astype(BF16))
            si_scr[g * rows:(g + 1) * rows, :] = _dot(u, wsi_ref[direction, g].astype(BF16))
        lam_r = lr_ref[direction]
        lam_i = li_ref[direction]

        def step(j, carry, direction=direction, lam_r=lam_r, lam_i=lam_i):
            out = []
            for b in range(batch):
                h_r, h_i = carry[2 * b], carry[2 * b + 1]
                idx = pl.ds(row_of(b, j, direction), groups, stride=rows)
                hr_scr[idx, :] = h_r
                hi_scr[idx, :] = h_i
                s_r = sr_scr[idx, :]
                s_i = si_scr[idx, :]
                out += [lam_r * h_r - lam_i * h_i + s_r, lam_r * h_i + lam_i * h_r + s_i]
            return tuple(out)

        zero = jnp.zeros((groups, S5_STATE), F32)
        lax.fori_loop(0, ncc + ncl, step, (zero,) * (2 * batch))
        causal = (t_out >= t_in) if direction == 0 else (t_in >= t_out)
        for g in range(groups):
            u = u_all[:, g * S5_ROW:(g + 1) * S5_ROW]
            m_intra = jnp.where(causal, hdot(wsr_ref[direction, g], wnr_ref[direction, g])
                                + hdot(wsi_ref[direction, g], wni_ref[direction, g]), 0.0).astype(BF16)
            y = (_dot(u, m_intra)
                 + _dot(hr_scr[g * rows:(g + 1) * rows, :].astype(BF16), wor_ref[direction, g])
                 + _dot(hi_scr[g * rows:(g + 1) * rows, :].astype(BF16), woi_ref[direction, g]))
            if direction == 0:
                y_scr[:, g * S5_ROW:(g + 1) * S5_ROW] = y
            else:
                y_scr[:, g * S5_ROW:(g + 1) * S5_ROW] += y

    z = _dot_nt(y_scr[...].astype(BF16), perm_ref[...])
    for t in range(S5_CHUNK):
        zt = z[:, t * 128:(t + 1) * 128]
        r0 = 0
        for y_ref in y_refs:
            n = y_ref.shape[0] * (y_ref.shape[2] if len(y_ref.shape) == 4 else 1)
            y_ref[:, t] = zt[r0:r0 + n].reshape(y_ref.shape[:1] + y_ref.shape[2:])
            r0 += n


def _s5_core(h, tables, dims, col_major):
    batch, lat_len, ctx_len, _ = dims
    wsr, wsi, wnr, wni, wor, woi, lam_r, lam_i = tables
    d = D_MODEL
    ncc, ncl = ctx_len // S5_CHUNK, lat_len // S5_CHUNK
    rows = batch * (ncc + ncl)
    gs = S5_SLAB_GROUPS
    if col_major:
        grid_rows = lat_len // GRID_W
        assert grid_rows % S5_CHUNK == 0
        col_chunks = grid_rows // S5_CHUNK
        nlat = batch * lat_len
        xs = [h[:nlat].reshape(batch * col_chunks, S5_CHUNK, GRID_W, d),
              h[nlat:].reshape(batch * ncc, S5_CHUNK, d)]
        blocks = [pl.BlockSpec((batch * col_chunks, S5_CHUNK, GRID_W, 128), lambda q: (0, 0, 0, q)),
                  pl.BlockSpec((batch * ncc, S5_CHUNK, 128), lambda q: (0, 0, q))]
    else:
        col_chunks = 0
        xs = [h.reshape(rows, S5_CHUNK, d)]
        blocks = [pl.BlockSpec((rows, S5_CHUNK, 128), lambda q: (0, 0, q))]
    kern = functools.partial(_s5_kernel, nparts=len(xs), batch=batch, ncc=ncc, ncl=ncl, col_chunks=col_chunks)
    wspec = lambda a, b: pl.BlockSpec((2, gs, a, b), lambda q: (0, q, 0, 0))
    ys = pl.pallas_call(
        kern,
        grid=(S5_GROUPS // gs,),
        in_specs=blocks + [pl.BlockSpec((S5_SLAB, S5_SLAB), lambda q: (0, 0)),
                           wspec(S5_ROW, S5_STATE), wspec(S5_ROW, S5_STATE),
                           wspec(S5_STATE, S5_ROW), wspec(S5_STATE, S5_ROW),
                           wspec(S5_STATE, S5_ROW), wspec(S5_STATE, S5_ROW),
                           pl.BlockSpec((2, gs, S5_STATE), lambda q: (0, q, 0)),
                           pl.BlockSpec((2, gs, S5_STATE), lambda q: (0, q, 0))],
        out_specs=blocks,
        out_shape=[jax.ShapeDtypeStruct(x.shape, F32) for x in xs],
        scratch_shapes=[pltpu.VMEM((gs * rows, S5_STATE), F32) for _ in range(4)]
        + [pltpu.VMEM((rows, S5_SLAB), F32)],
        compiler_params=_params("parallel"),
    )(*xs, _s5_perm(), wsr, wsi, wnr, wni, wor, woi, lam_r, lam_i)
    if col_major:
        return jnp.concatenate([ys[0].reshape(-1, d), ys[1].reshape(-1, d)], axis=0)
    return ys[0].reshape(-1, d)


def _s5_glu_kernel(h_ref, y_ref, dskip_ref, w1_ref, w2_ref, b1_ref, b2_ref, x_ref, gate_ref, o_ref, z_scr):
    @pl.when(pl.program_id(1) == 0)
    def _():
        y = h_ref[...] * dskip_ref[...] + y_ref[...]
        z = 0.5 * y * (1.0 + jnp.tanh(math.sqrt(2.0 / math.pi) * (y + 0.044715 * (y * y * y))))
        z_scr[...] = z.astype(BF16)

    z = z_scr[...]
    lin = _dot(z, w1_ref[...]) + b1_ref[...]
    gat = _dot(z, w2_ref[...]) + b2_ref[...]
    o_ref[...] = x_ref[...] + gate_ref[...] * (lin * _sigmoid(gat))


def _s5_glu(h, y, dskip, w, b, x, mods, layer, dims):
    batch, lat_len, _, tm = dims
    ntok, d = x.shape
    tn = 512
    nblk = d // tn
    tiles_per_seq = lat_len // tm
    b = b.reshape(1, 2 * d)
    return pl.pallas_call(
        _s5_glu_kernel,
        grid=(ntok // tm, nblk),
        in_specs=[pl.BlockSpec((tm, d), lambda i, j: (i, 0)),
                  pl.BlockSpec((tm, d), lambda i, j: (i, 0)),
                  pl.BlockSpec((1, d), lambda i, j: (0, 0)),
                  pl.BlockSpec((d, tn), lambda i, j: (0, j)),
                  pl.BlockSpec((d, tn), lambda i, j: (0, nblk + j)),
                  pl.BlockSpec((1, tn), lambda i, j: (0, j)),
                  pl.BlockSpec((1, tn), lambda i, j: (0, nblk + j)),
                  pl.BlockSpec((tm, tn), lambda i, j: (i, j)),
                  pl.BlockSpec((None, 1, tn), lambda i, j: (
                      layer * 8 + jnp.minimum(i // tiles_per_seq, batch), 0, 2 * nblk + j))],
        out_specs=pl.BlockSpec((tm, tn), lambda i, j: (i, j)),
        out_shape=jax.ShapeDtypeStruct((ntok, d), F32),
        scratch_shapes=[pltpu.VMEM((tm, d), BF16)],
        compiler_params=_params("parallel", "arbitrary"),
    )(h, y, dskip.reshape(1, d), w, w, b, b, x, mods)


def _s5_layer(x, mods, layer, j, dims, p):
    batch, lat_len, ctx_len, _ = dims
    col_major = (j % 2) == 1
    h = _prenorm(x, mods, p['norm_g'][layer, 0], layer, dims)
    tables = _s5_tables(p['s5_a_re'][j], p['s5_a_im'][j], p['s5_log_dt'][j], p['s5_b_re'][j],
                        p['s5_b_im'][j], p['s5_c_re'][j], p['s5_c_im'][j])
    y = _s5_core(h, tables, dims, col_major)
    return _s5_glu(h, y, p['s5_d'][j], p['s5_glu_w'][j].astype(BF16), p['s5_glu_b'][j], x, mods, layer, dims)


def _seq_blocks(dims):
    batch, lat_len, ctx_len, _ = dims
    return [(lat_len, 0), (ctx_len, batch * lat_len // ctx_len)]


def _dwconv3_kernel(u_ref, w_ref, b_ref, o_ref, *maybe_bf16_ref):
    x = u_ref[...]
    n = x.shape[0]
    row = lax.broadcasted_iota(jnp.int32, x.shape, 0)
    prev = jnp.where(row == 0, 0.0, pltpu.roll(x, 1, 0))
    nxt = jnp.where(row == n - 1, 0.0, pltpu.roll(x, n - 1, 0))
    y = prev * w_ref[0:1, :] + x * w_ref[1:2, :] + nxt * w_ref[2:3, :] + b_ref[...]
    o_ref[...] = y
    for r in maybe_bf16_ref:
        r[...] = y.astype(BF16)


def _dwconv3(u, w, b, dims, col0, ncols, with_bf16):
    batch = dims[0]
    ntok = u.shape[0]
    tc = 256
    cb0 = col0 // tc
    w8 = jnp.zeros((8, w.shape[1]), F32).at[:3].set(w)
    outs = []
    for n, off in _seq_blocks(dims):
        out_shape = [jax.ShapeDtypeStruct((batch * n, ncols), F32)]
        out_specs = [pl.BlockSpec((n, tc), lambda s, j: (s, j))]
        if with_bf16:
            out_shape.append(jax.ShapeDtypeStruct((batch * n, ncols), BF16))
            out_specs.append(pl.BlockSpec((n, tc), lambda s, j: (s, j)))
        outs.append(pl.pallas_call(
            _dwconv3_kernel,
            grid=(batch, ncols // tc),
            in_specs=[pl.BlockSpec((n, tc), lambda s, j, off=off: (off + s, cb0 + j)),
                      pl.BlockSpec((8, tc), lambda s, j: (0, cb0 + j)),
                      pl.BlockSpec((1, tc), lambda s, j: (0, cb0 + j))],
            out_specs=out_specs,
            out_shape=out_shape,
            compiler_params=_params("parallel", "parallel"),
        )(u, w8, b.reshape(1, -1)))
    return outs


def _dft_tables(n):
    k = jnp.arange(n, dtype=jnp.int32)[:, None]
    t = jnp.arange(n, dtype=jnp.int32)[None, :]
    m = ((2 * k + 1) * t) % (4 * n)
    ang = m.astype(F32) * (2.0 * math.pi / (4 * n))
    c, s = jnp.cos(ang), jnp.sin(ang)
    return c.astype(BF16), s.astype(BF16), c.T.astype(BF16), s.T.astype(BF16)


def _hy_filter_kernel(feat_ref, w1_ref, b1_ref, w2_ref, b2_ref, w3_ref, b3_ref, fq_ref, tu_ref, dl_ref,
                      w4_ref, hs_ref, hd_ref, hdn_scr):
    @pl.when(pl.program_id(0) == 0)
    def _():
        fq = fq_ref[...]
        h = jnp.sin(fq * (jnp.dot(feat_ref[...], w1_ref[...], precision=HIGHEST, preferred_element_type=F32) + b1_ref[...]))
        h = jnp.sin(fq * (jnp.dot(h, w2_ref[...], precision=HIGHEST, preferred_element_type=F32) + b2_ref[...]))
        h = jnp.sin(fq * (jnp.dot(h, w3_ref[...], precision=HIGHEST, preferred_element_type=F32) + b3_ref[...]))
        hdn_scr[...] = h

    hdn = hdn_scr[...]
    decay = jnp.exp(-tu_ref[...] * dl_ref[...])
    row = lax.broadcasted_iota(jnp.int32, decay.shape, 0)
    for order in range(HY_ORDER):
        fwd = jnp.dot(hdn, w4_ref[2 * order], precision=HIGHEST, preferred_element_type=F32) * decay
        bwd = jnp.dot(hdn, w4_ref[2 * order + 1], precision=HIGHEST, preferred_element_type=F32) * decay
        bwd = jnp.where(row == 0, 0.0, bwd)
        norm = jnp.sum(jnp.abs(fwd), axis=0, keepdims=True) + jnp.sum(jnp.abs(bwd), axis=0, keepdims=True)
        hs_ref[order] = ((fwd + bwd) / norm).astype(BF16)
        hd_ref[order] = ((fwd - bwd) / norm).astype(BF16)


def _hy_filters(n, p, j):
    d = D_MODEL
    t = jnp.arange(n, dtype=F32)
    t_unit = t / max(n - 1, 1)
    bands = jnp.linspace(1e-4, HY_BANDS - 1, HY_BANDS, dtype=F32)
    ang = (2.0 * math.pi / n) * t[:, None] * bands[None, :]
    feats = jnp.concatenate([t_unit[:, None], jnp.cos(ang), -jnp.sin(ang)], axis=-1)
    feats = jnp.pad(feats, ((0, 0), (0, HY_EMB_PAD - HY_EMB)))
    w1 = jnp.pad(p['hy_f_w1'][j], ((0, HY_EMB_PAD - HY_EMB), (0, 0)))
    deltas = jnp.abs(jnp.linspace(HY_MIN_DECAY, HY_MAX_DECAY, d, dtype=F32)).reshape(1, d)
    w4 = p['hy_f_w4'][j].reshape(HY_FILTER_HIDDEN, 2 * HY_ORDER, d).transpose(1, 0, 2)
    hid = HY_FILTER_HIDDEN
    tc = 128
    full = lambda shape: pl.BlockSpec(shape, lambda c: tuple(0 for _ in shape))
    row = lambda v: v.reshape(1, hid)
    return pl.pallas_call(
        _hy_filter_kernel,
        grid=(d // tc,),
        in_specs=[full((n, HY_EMB_PAD)), full((HY_EMB_PAD, hid)), full((1, hid)), full((hid, hid)), full((1, hid)),
                  full((hid, hid)), full((1, hid)), full((1, hid)), full((n, 1)),
                  pl.BlockSpec((1, tc), lambda c: (0, c)),
                  pl.BlockSpec((2 * HY_ORDER, hid, tc), lambda c: (0, 0, c))],
        out_specs=[pl.BlockSpec((HY_ORDER, n, tc), lambda c: (0, 0, c)),
                   pl.BlockSpec((HY_ORDER, n, tc), lambda c: (0, 0, c))],
        out_shape=[jax.ShapeDtypeStruct((HY_ORDER, n, d), BF16), jax.ShapeDtypeStruct((HY_ORDER, n, d), BF16)],
        scratch_shapes=[pltpu.VMEM((n, hid), F32)],
        compiler_params=_params("arbitrary"),
    )(feats, w1, row(p['hy_f_b1'][j]), p['hy_f_w2'][j], row(p['hy_f_b2'][j]), p['hy_f_w3'][j],
      row(p['hy_f_b3'][j]), row(p['hy_f_freq'][j]), t_unit.reshape(n, 1), deltas, w4)


def _dft_filter_kernel(c_ref, s_ref, hs_ref, hd_ref, hr_ref, hi_ref):
    hr_ref[...] = _dot(c_ref[...], hs_ref[...])
    hi_ref[...] = -_dot(s_ref[...], hd_ref[...])


def _dft_filter(cmat, smat, hs, hd, tk, td):
    n, d = hs.shape[1], hs.shape[2]
    return pl.pallas_call(
        _dft_filter_kernel,
        grid=(n // tk, HY_ORDER, d // td),
        in_specs=[pl.BlockSpec((tk, n), lambda i, o, j: (i, 0)),
                  pl.BlockSpec((tk, n), lambda i, o, j: (i, 0)),
                  pl.BlockSpec((None, n, td), lambda i, o, j: (o, 0, j)),
                  pl.BlockSpec((None, n, td), lambda i, o, j: (o, 0, j))],
        out_specs=[pl.BlockSpec((None, tk, td), lambda i, o, j: (o, i, j)),
                   pl.BlockSpec((None, tk, td), lambda i, o, j: (o, i, j))],
        out_shape=[jax.ShapeDtypeStruct((HY_ORDER, n, d), F32), jax.ShapeDtypeStruct((HY_ORDER, n, d), F32)],
        compiler_params=_params("parallel", "parallel", "parallel"),
    )(cmat, smat, hs, hd)


def _dft_fwd_kernel(c_ref, s_ref, z_ref, hr_ref, hi_ref, yr_ref, yi_ref):
    z = z_ref[...]
    zc = _dot(c_ref[...], z)
    zs = _dot(s_ref[...], z)
    hr, hi = hr_ref[...], hi_ref[...]
    yr_ref[...] = (zc * hr + zs * hi).astype(BF16)
    yi_ref[...] = (zc * hi - zs * hr).astype(BF16)


def _dft_fwd(cmat, smat, z, hr, hi, order, batch, tk, td):
    n = cmat.shape[0]
    d = z.shape[1]
    return pl.pallas_call(
        _dft_fwd_kernel,
        grid=(n // tk, batch, d // td),
        in_specs=[pl.BlockSpec((tk, n), lambda i, s, j: (i, 0)),
                  pl.BlockSpec((tk, n), lambda i, s, j: (i, 0)),
                  pl.BlockSpec((n, td), lambda i, s, j: (s, j)),
                  pl.BlockSpec((None, tk, td), lambda i, s, j: (order, i, j)),
                  pl.BlockSpec((None, tk, td), lambda i, s, j: (order, i, j))],
        out_specs=[pl.BlockSpec((None, tk, td), lambda i, s, j: (s, i, j)),
                   pl.BlockSpec((None, tk, td), lambda i, s, j: (s, i, j))],
        out_shape=[jax.ShapeDtypeStruct((batch, n, d), BF16), jax.ShapeDtypeStruct((batch, n, d), BF16)],
        compiler_params=_params("parallel", "parallel", "parallel"),
    )(cmat, smat, z, hr, hi)


def _dft_inv_kernel(ct_ref, st_ref, yr_ref, yi_ref, xg_ref, zp_ref, bias_ref, o_ref, obf_ref, *, inv_n):
    y = (_dot(ct_ref[...], yr_ref[...]) - _dot(st_ref[...], yi_ref[...])) * inv_n
    out = xg_ref[...] * (y + zp_ref[...] * bias_ref[...])
    o_ref[...] = out
    obf_ref[...] = out.astype(BF16)


def _dft_inv(ctm, stm, yr, yi, xg, xg_col0, zprev, bias, batch, tt, td):
    n = ctm.shape[0]
    d = zprev.shape[1]
    cb0 = xg_col0 // td
    kern = functools.partial(_dft_inv_kernel, inv_n=1.0 / n)
    return pl.pallas_call(
        kern,
        grid=(n // tt, batch, d // td),
        in_specs=[pl.BlockSpec((tt, n), lambda i, s, j: (i, 0)),
                  pl.BlockSpec((tt, n), lambda i, s, j: (i, 0)),
                  pl.BlockSpec((None, n, td), lambda i, s, j: (s, 0, j)),
                  pl.BlockSpec((None, n, td), lambda i, s, j: (s, 0, j)),
                  pl.BlockSpec((tt, td), lambda i, s, j: (s * (n // tt) + i, cb0 + j)),
                  pl.BlockSpec((tt, td), lambda i, s, j: (s * (n // tt) + i, j)),
                  pl.BlockSpec((1, td), lambda i, s, j: (0, j))],
        out_specs=[pl.BlockSpec((tt, td), lambda i, s, j: (s * (n // tt) + i, j)),
                   pl.BlockSpec((tt, td), lambda i, s, j: (s * (n // tt) + i, j))],
        out_shape=[jax.ShapeDtypeStruct((batch * n, d), F32), jax.ShapeDtypeStruct((batch * n, d), BF16)],
        compiler_params=_params("parallel", "parallel", "parallel"),
    )(ctm, stm, yr, yi, xg, zprev, bias.reshape(1, d))


def _hyena_layer(x, mods, layer, j, dims, p):
    batch, lat_len, ctx_len, _ = dims
    d = D_MODEL
    u = _mm_norm(x, mods, p['norm_g'][layer, 0], p['hy_in_w'][j].astype(BF16), p['hy_in_b'][j], layer, dims, 512)
    v_parts = _dwconv3(u, p['hy_conv_w'][j], p['hy_conv_b'][j], dims, 0, d, True)
    g_parts = _dwconv3(u, p['hy_conv_w'][j], p['hy_conv_b'][j], dims, d, 2 * d, False)
    z_out = []
    for (n, _), (v32, vbf), (gates,) in zip(_seq_blocks(dims), v_parts, g_parts):
        tk = min(512, n)
        td = 512
        cmat, smat, ctm, stm = _dft_tables(n)
        hs, hd = _hy_filters(n, p, j)
        hr, hi = _dft_filter(cmat, smat, hs, hd, tk, td)
        zprev32, zprevbf = v32, vbf
        for order in range(HY_ORDER):
            yr, yi = _dft_fwd(cmat, smat, zprevbf, hr, hi, order, batch, tk, td)
            zprev32, zprevbf = _dft_inv(ctm, stm, yr, yi, gates, order * d, zprev32, p['hy_bias'][j, order],
                                        batch, tk, td)
        z_out.append(zprevbf)
    z = jnp.concatenate(z_out, axis=0)
    return _mm_res(z, p['hy_out_w'][j].astype(BF16), p['hy_out_b'][j], x, mods, layer, dims, 512)


def _gdn_conv_kernel(p_ref, w_ref, o_ref):
    x = p_ref[...]
    n = x.shape[0]
    row = lax.broadcasted_iota(jnp.int32, x.shape, 0)
    acc = x * w_ref[2:3, :]
    for s in (1, 2):
        prev = jnp.where(row < s, 0.0, pltpu.roll(x, s, 0))
        nxt = jnp.where(row >= n - s, 0.0, pltpu.roll(x, n - s, 0))
        acc = acc + prev * w_ref[2 - s:3 - s, :] + nxt * w_ref[2 + s:3 + s, :]
    y = _silu(acc)
    head = pl.program_id(1)
    inv = lax.rsqrt(jnp.sum(y * y, axis=-1, keepdims=True) + 1e-6)
    inv = inv * jnp.where(head < GDN_K_HEADS, GDN_HEAD ** -0.5, 1.0)
    o_ref[...] = y * jnp.where(head < 2 * GDN_K_HEADS, inv, 1.0)


def _gdn_conv(proj, w, dims):
    batch = dims[0]
    ntok = proj.shape[0]
    nheads = GDN_CONV_DIM // GDN_HEAD
    w8 = jnp.zeros((8, GDN_CONV_DIM), F32).at[:5].set(w)
    out = None
    for n, off in _seq_blocks(dims):
        args = [proj, w8]
        in_specs = [pl.BlockSpec((n, GDN_HEAD), lambda s, h, off=off: (off + s, h)),
                    pl.BlockSpec((8, GDN_HEAD), lambda s, h: (0, h))]
        aliases = {}
        if out is not None:
            args.append(out)
            in_specs.append(pl.BlockSpec(memory_space=pl.ANY))
            aliases = {2: 0}
        kern = _gdn_conv_kernel if out is None else (lambda p_ref, w_ref, _, o_ref: _gdn_conv_kernel(p_ref, w_ref, o_ref))
        out = pl.pallas_call(
            kern,
            grid=(batch, nheads),
            in_specs=in_specs,
            out_specs=pl.BlockSpec((None, n, GDN_HEAD), lambda s, h, off=off: (h, off + s, 0)),
            out_shape=jax.ShapeDtypeStruct((nheads, ntok, GDN_HEAD), F32),
            input_output_aliases=aliases,
            compiler_params=_params("parallel", "parallel"),
        )(*args)
    return out


def _gdn_gate_kernel(ab_ref, alog_ref, dtb_ref, o_ref):
    c = GDN_CHUNK
    row = lax.broadcasted_iota(jnp.int32, (c, c), 0)
    col = lax.broadcasted_iota(jnp.int32, (c, c), 1)
    lower = jnp.where(row >= col, 1.0, 0.0).astype(F32)
    upper = jnp.where(row <= col, 1.0, 0.0).astype(F32)
    lane = lax.broadcasted_iota(jnp.int32, (c, 128), 1)
    for r in range(ab_ref.shape[0] // c):
        ab = ab_ref[r * c:(r + 1) * c, :]
        xs = ab + dtb_ref[...]
        softplus = jnp.maximum(xs, 0.0) + jnp.log(1.0 + jnp.exp(-jnp.abs(xs)))
        g = -jnp.exp(alog_ref[...]) * softplus
        cum_f = jnp.dot(lower, g, precision=HIGHEST, preferred_element_type=F32)
        cum_b = jnp.dot(upper, g, precision=HIGHEST, preferred_element_type=F32)
        o_ref[r * c:(r + 1) * c, :] = jnp.where(lane < GDN_V_HEADS, cum_f,
                                                jnp.where(lane < 2 * GDN_V_HEADS, cum_b, _sigmoid(ab)))


def _gdn_gates(ab, a_log, dt_bias, tm):
    ntok = ab.shape[0]
    pad = jnp.zeros((2 * GDN_V_HEADS,), F32)
    alog = jnp.concatenate([a_log.reshape(-1), pad]).reshape(1, 128)
    dtb = jnp.concatenate([dt_bias.reshape(-1), pad]).reshape(1, 128)
    return pl.pallas_call(
        _gdn_gate_kernel,
        grid=(ntok // tm,),
        in_specs=[pl.BlockSpec((tm, 128), lambda i: (i, 0)),
                  pl.BlockSpec((1, 128), lambda i: (0, 0)),
                  pl.BlockSpec((1, 128), lambda i: (0, 0))],
        out_specs=pl.BlockSpec((tm, 128), lambda i: (i, 0)),
        out_shape=jax.ShapeDtypeStruct((ntok, 128), F32),
        compiler_params=_params("parallel"),
    )(ab, alog, dtb)


def _unit_triangular_inverses(mats):
    c = mats[0].shape[0]
    row = lax.broadcasted_iota(jnp.int32, (c, c), 0)
    col = lax.broadcasted_iota(jnp.int32, (c, c), 1)
    eye = jnp.where(row == col, 1.0, 0.0).astype(F32)
    ts = [eye - a for a in mats]
    pws = list(mats)
    for _ in range(int(math.log2(c)) - 1):
        pwbs = [pw.astype(BF16) for pw in pws]
        pws = [_dot(pwb, pwb) for pwb in pwbs]
        ts = [t + _dot(t.astype(BF16), pw.astype(BF16)) for t, pw in zip(ts, pws)]
    ms = [eye + a for a in mats]
    m_his = [m.astype(BF16) for m in ms]
    m_los = [(m - m_hi.astype(F32)).astype(BF16) for m, m_hi in zip(ms, m_his)]
    t_his = [t.astype(BF16) for t in ts]
    t_los = [(t - t_hi.astype(F32)).astype(BF16) for t, t_hi in zip(ts, t_his)]
    resids = [eye - (_dot(m_hi, t_hi) + (_dot(m_hi, t_lo) + _dot(m_lo, t_hi)))
              for m_hi, m_lo, t_hi, t_lo in zip(m_his, m_los, t_his, t_los)]
    return [t + _dot(t_hi, r.astype(BF16)) for t, t_hi, r in zip(ts, t_his, resids)]


def _gdn_chunk_kernel(qf_ref, kf_ref, vf_ref, gcf_ref, grf_ref, qb_ref, kb_ref, vb_ref, gcb_ref, grb_ref,
                      of_ref, ob_ref, state_scr, *, nchunks):
    c = GDN_CHUNK

    @pl.when(pl.program_id(2) == 0)
    def _():
        state_scr[...] = jnp.zeros_like(state_scr)

    row = lax.broadcasted_iota(jnp.int32, (c, c), 0)
    col = lax.broadcasted_iota(jnp.int32, (c, c), 1)
    blocks = ((qf_ref, kf_ref, vf_ref, gcf_ref, grf_ref, of_ref), (qb_ref, kb_ref, vb_ref, gcb_ref, grb_ref, ob_ref))
    nkh = qf_ref.shape[0]
    keys, amats, part = [], [], {}
    for direction, (q_ref, k_ref, v_ref, gc_ref, gr_ref, o_ref) in enumerate(blocks):
        if direction == 0:
            incl, strict, last = row >= col, row > col, c - 1
        else:
            incl, strict, last = row <= col, row < col, 0
        for kh in range(nkh):
            for ci in range(nchunks):
                rows = slice(ci * c, (ci + 1) * c)
                q = q_ref[kh, rows, :].astype(BF16)
                k32 = k_ref[kh, rows, :]
                k = k32.astype(BF16)
                k_t = k32.T
                gates_c = gc_ref[kh, ci]
                gates_r = gr_ref[kh, ci]
                kk = _dot_nt(k, k)
                qk = _dot_nt(q, k)
                for e in range(2):
                    ch_g, ch_b = direction * 2 + e, 4 + direction * 2 + e
                    gcc, gcr = gates_c[:, ch_g:ch_g + 1], gates_r[ch_g:ch_g + 1, :]
                    beta_c, beta_r = gates_c[:, ch_b:ch_b + 1], gates_r[ch_b:ch_b + 1, :]
                    decay = jnp.where(incl, jnp.exp(jnp.where(incl, gcc - gcr, 0.0)), 0.0)
                    key = (direction, kh, ci, e)
                    keys.append(key)
                    amats.append(jnp.where(strict, kk * beta_c * decay, 0.0))
                    g_last = gcr[:, last:last + 1]
                    part[key] = dict(
                        q=q, k=k, v=v_ref[2 * kh + e, rows, :].astype(BF16), attn=(qk * decay).astype(BF16),
                        beta_r=beta_r, wscale=beta_r * jnp.exp(gcr), egc=jnp.exp(gcc),
                        kg_t=(k_t * jnp.exp(g_last - gcr)).astype(BF16), e_last=jnp.exp(g_last))
    tmats = _unit_triangular_inverses(amats)
    local = {}
    for key, t in zip(keys, tmats):
        p = part[key]
        u = _dot((t * p['beta_r']).astype(BF16), p['v'])
        w = _dot((t * p['wscale']).astype(BF16), p['k']).astype(BF16)
        local[key] = (p['q'], u, w, p['attn'], p['egc'], p['kg_t'], p['e_last'])

    nslots = state_scr.shape[0]
    states = [state_scr[slot] for slot in range(nslots)]
    streams = [(direction, kh, e) for direction in range(2) for kh in range(nkh) for e in range(2)]
    out_refs = (of_ref, ob_ref)
    for step in range(nchunks):
        cur = [local[d, kh, (step if d == 0 else nchunks - 1 - step), e] for d, kh, e in streams]
        state_bs = [s.astype(BF16) for s in states]
        ws = [_dot(p[2], sb) for p, sb in zip(cur, state_bs)]
        qs = [_dot(p[0], sb) for p, sb in zip(cur, state_bs)]
        v_news = [(p[1] - w).astype(BF16) for p, w in zip(cur, ws)]
        intra = [_dot(p[3], vn) for p, vn in zip(cur, v_news)]
        upd = [_dot(p[5], vn) for p, vn in zip(cur, v_news)]
        for (d, kh, e), p, q_s, o_in in zip(streams, cur, qs, intra):
            ci = step if d == 0 else nchunks - 1 - step
            out_refs[d][2 * kh + e, ci * c:(ci + 1) * c, :] = p[4] * q_s + o_in
        states = [s * p[6] + dlt for s, p, dlt in zip(states, cur, upd)]
    for slot in range(nslots):
        state_scr[slot] = states[slot]


def _gdn_chunks(qkvh, gates, dims):
    batch, lat_len, ctx_len, _ = dims
    c = GDN_CHUNK
    ntok = qkvh.shape[1]
    per_step = min(4, ctx_len // c)
    rows = per_step * c
    nbc, nbl = ctx_len // rows, lat_len // rows
    ctx0 = batch * lat_len // rows
    gk = gates.reshape(ntok // c, c, 4, GDN_K_HEADS, 2).transpose(3, 0, 1, 2, 4).reshape(GDN_K_HEADS, ntok // c, c, 8)
    gk_t = gk.transpose(0, 1, 3, 2)

    def fwd_blk(b, j):
        return jnp.where(j < nbc, ctx0 + b * nbc + j, b * nbl + j - nbc)

    def bwd_blk(b, j):
        return jnp.where(j < nbc, ctx0 + b * nbc + nbc - 1 - j, b * nbl + nbl - 1 - (j - nbc))

    nkh = GDN_KHEADS_PER_STEP
    hblocks = GDN_K_HEADS // nkh

    def specs(blk):
        return [pl.BlockSpec((nkh, rows, GDN_HEAD), lambda b, h, j: (h, blk(b, j), 0)),
                pl.BlockSpec((nkh, rows, GDN_HEAD), lambda b, h, j: (hblocks + h, blk(b, j), 0)),
                pl.BlockSpec((2 * nkh, rows, GDN_HEAD), lambda b, h, j: (hblocks + h, blk(b, j), 0)),
                pl.BlockSpec((nkh, per_step, c, 8), lambda b, h, j: (h, blk(b, j), 0, 0)),
                pl.BlockSpec((nkh, per_step, 8, c), lambda b, h, j: (h, blk(b, j), 0, 0))]

    out_sd = jax.ShapeDtypeStruct((GDN_V_HEADS, ntok, GDN_HEAD), F32)
    return pl.pallas_call(
        functools.partial(_gdn_chunk_kernel, nchunks=per_step),
        grid=(batch, hblocks, nbc + nbl),
        in_specs=specs(fwd_blk) + specs(bwd_blk),
        out_specs=[pl.BlockSpec((2 * nkh, rows, GDN_HEAD), lambda b, h, j: (h, fwd_blk(b, j), 0)),
                   pl.BlockSpec((2 * nkh, rows, GDN_HEAD), lambda b, h, j: (h, bwd_blk(b, j), 0))],
        out_shape=[out_sd, out_sd],
        scratch_shapes=[pltpu.VMEM((4 * nkh, GDN_HEAD, GDN_HEAD), F32)],
        compiler_params=_params("parallel", "parallel", "arbitrary"),
    )(qkvh, qkvh, qkvh, gk, gk_t, qkvh, qkvh, qkvh, gk, gk_t)


def _gdn_out_kernel(of_ref, ob_ref, z_ref, ng_ref, w_ref, x_ref, gate_ref, o_ref, a_scr):
    @pl.when(pl.program_id(1) == 0)
    def _():
        ng = ng_ref[...]
        for h in range(GDN_V_HEADS):
            o = of_ref[h] + ob_ref[h]
            o = o * lax.rsqrt(jnp.mean(o * o, axis=-1, keepdims=True) + EPS)
            z = z_ref[:, h * GDN_HEAD:(h + 1) * GDN_HEAD]
            a_scr[:, h * GDN_HEAD:(h + 1) * GDN_HEAD] = (o * ng * _silu(z)).astype(BF16)

    o_ref[...] = x_ref[...] + gate_ref[...] * _dot(a_scr[...], w_ref[...])


def _gdn_out(o_f, o_b, proj, norm_g, w, x, mods, layer, dims):
    batch, lat_len, _, _ = dims
    tm = min(256, dims[3])
    ntok, d = x.shape
    tn = 512
    gate_blk = 2 * d // tn
    tiles_per_seq = lat_len // tm
    zblk = GDN_CONV_DIM // GDN_V
    return pl.pallas_call(
        _gdn_out_kernel,
        grid=(ntok // tm, d // tn),
        in_specs=[pl.BlockSpec((GDN_V_HEADS, tm, GDN_HEAD), lambda i, j: (0, i, 0)),
                  pl.BlockSpec((GDN_V_HEADS, tm, GDN_HEAD), lambda i, j: (0, i, 0)),
                  pl.BlockSpec((tm, GDN_V), lambda i, j: (i, zblk)),
                  pl.BlockSpec((1, GDN_HEAD), lambda i, j: (0, 0)),
                  pl.BlockSpec((GDN_V, tn), lambda i, j: (0, j)),
                  pl.BlockSpec((tm, tn), lambda i, j: (i, j)),
                  pl.BlockSpec((None, 1, tn), lambda i, j: (
                      layer * 8 + jnp.minimum(i // tiles_per_seq, batch), 0, gate_blk + j))],
        out_specs=pl.BlockSpec((tm, tn), lambda i, j: (i, j)),
        out_shape=jax.ShapeDtypeStruct((ntok, d), F32),
        scratch_shapes=[pltpu.VMEM((tm, GDN_V), BF16)],
        compiler_params=_params("parallel", "arbitrary"),
    )(o_f, o_b, proj, norm_g.reshape(1, GDN_HEAD), w, x, mods)


def _gdn_layer(x, mods, layer, j, dims, p):
    in_w = p['gdn_in_w'][j]
    g = p['norm_g'][layer, 0]
    proj = _mm_norm(x, mods, g, in_w[:, :GDN_MAIN].astype(BF16), jnp.zeros((GDN_MAIN,), F32), layer, dims, 512)
    ab = _mm_norm(x, mods, g, in_w[:, GDN_MAIN:].astype(BF16), jnp.zeros((128,), F32), layer, dims, 128)
    qkvh = _gdn_conv(proj, p['gdn_conv_w'][j], dims)
    gates = _gdn_gates(ab, p['gdn_a_log'][j], p['gdn_dt_bias'][j], dims[3])
    o_f, o_b = _gdn_chunks(qkvh, gates, dims)
    return _gdn_out(o_f, o_b, proj, p['gdn_norm_g'][j], p['gdn_out_w'][j].astype(BF16), x, mods, layer, dims)


def kernel(x, c, ctx, c_ctx, ada_w, ada_b, norm_g, final_g, ffn_w_gate, ffn_w_up, ffn_w_down, s5_a_re, s5_a_im, s5_log_dt, s5_b_re, s5_b_im, s5_c_re, s5_c_im, s5_d, s5_glu_w, s5_glu_b, hy_in_w, hy_in_b, hy_conv_w, hy_conv_b, hy_f_w1, hy_f_b1, hy_f_w2, hy_f_b2, hy_f_w3, hy_f_b3, hy_f_w4, hy_f_freq, hy_bias, hy_out_w, hy_out_b, gdn_in_w, gdn_conv_w, gdn_a_log, gdn_dt_bias, gdn_norm_g, gdn_out_w):
    p = dict(locals())
    batch, lat_len, d = x.shape
    ctx_len = ctx.shape[1]
    depth = ada_w.shape[0]
    tm = min(512, batch * ctx_len)
    dims = (batch, lat_len, ctx_len, tm)
    assert d == D_MODEL and lat_len % tm == 0 and (batch * ctx_len) % tm == 0 and batch + 1 <= 8

    cvec = jnp.zeros((8, d), F32).at[:batch].set(c).at[batch].set(c_ctx)
    mods = _ada_all(cvec, ada_w, ada_b).reshape(depth * 8, 1, 6 * d)
    tok = jnp.concatenate([x.reshape(batch * lat_len, d), ctx.reshape(batch * ctx_len, d)], axis=0)
    for i in range(depth):
        kind, j = i % N_MIXERS, i // N_MIXERS
        if kind == 0:
            tok = _s5_layer(tok, mods, i, j, dims, p)
        elif kind == 1:
            tok = _hyena_layer(tok, mods, i, j, dims, p)
        else:
            tok = _gdn_layer(tok, mods, i, j, dims, p)
        tok = _ffn(tok, mods, norm_g[i, 1], ffn_w_gate[i].astype(BF16), ffn_w_up[i].astype(BF16),
                   ffn_w_down[i].astype(BF16), i, dims)
    out = _final_norm(tok, final_g, batch * lat_len, tm)
    return out.reshape(batch, lat_len, d)
```

```python
import functools
import math

import jax
import jax.numpy as jnp
from jax import lax
from jax.experimental import pallas as pl
from jax.experimental.pallas import tpu as pltpu

F32 = jnp.float32
BF16 = jnp.bfloat16
HIGHEST = lax.Precision.HIGHEST

D_MODEL = 2048
GRID_W = 64
EPS = 1e-6
N_MIXERS = 3

S5_GROUP = 16
S5_STATE = 64
S5_GROUPS = D_MODEL // S5_GROUP
S5_CHUNK = 16
S5_ROW = S5_CHUNK * S5_GROUP
S5_SLAB_GROUPS = 128 // S5_GROUP
S5_SLAB = S5_SLAB_GROUPS * S5_ROW

HY_ORDER = 2
HY_BANDS = 16
HY_EMB = 1 + 2 * HY_BANDS
HY_EMB_PAD = 128
HY_FILTER_HIDDEN = 64
HY_DECAY_TARGET = 1e-2
HY_MAX_DECAY = math.log(HY_DECAY_TARGET) / 0.3
HY_MIN_DECAY = math.log(HY_DECAY_TARGET) / 1.5

GDN_K_HEADS = 16
GDN_V_HEADS = 32
GDN_HEAD = 128
GDN_QK = GDN_K_HEADS * GDN_HEAD
GDN_V = GDN_V_HEADS * GDN_HEAD
GDN_CONV_DIM = 2 * GDN_QK + GDN_V
GDN_MAIN = GDN_CONV_DIM + GDN_V
GDN_CHUNK = 64
GDN_KHEADS_PER_STEP = 2

VMEM_LIMIT_BYTES = 56 * 1024 * 1024


def _params(*sem):
    return pltpu.CompilerParams(dimension_semantics=sem, vmem_limit_bytes=VMEM_LIMIT_BYTES)


def _dot(a, b):
    return jnp.dot(a, b, preferred_element_type=F32)


def _dot_nt(a, b):
    return lax.dot_general(a, b, (((1,), (1,)), ((), ())), preferred_element_type=F32)


def _dot_tn(a, b):
    return lax.dot_general(a, b, (((0,), (0,)), ((), ())), preferred_element_type=F32)


def _sigmoid(x):
    return 1.0 / (1.0 + jnp.exp(-x))


def _silu(x):
    return x * _sigmoid(x)


def _norm_mod(x, g, sc, sh):
    y = x * lax.rsqrt(jnp.mean(x * x, axis=-1, keepdims=True) + EPS) * g
    return y * (1.0 + sc) + sh


def _mod_spec(layer, tm, lat_len, batch, width, col_block):
    tiles_per_seq = lat_len // tm

    def index(i, *_):
        return (layer * 8 + jnp.minimum(i // tiles_per_seq, batch), 0, col_block)

    return pl.BlockSpec((None, 1, width), index)


def _ada_kernel(c_ref, w_ref, b_ref, o_ref):
    c = c_ref[...]
    s = _silu(c).astype(BF16)
    o_ref[...] = _dot(s, w_ref[...].astype(BF16)) + b_ref[...]


def _ada_all(cvec, ada_w, ada_b):
    depth, d, n6 = ada_w.shape
    tn = 1024
    return pl.pallas_call(
        _ada_kernel,
        grid=(depth, n6 // tn),
        in_specs=[pl.BlockSpec((8, d), lambda l, j: (0, 0)),
                  pl.BlockSpec((None, d, tn), lambda l, j: (l, 0, j)),
                  pl.BlockSpec((None, 1, tn), lambda l, j: (l, 0, j))],
        out_specs=pl.BlockSpec((None, 8, tn), lambda l, j: (l, 0, j)),
        out_shape=jax.ShapeDtypeStruct((depth, 8, n6), F32),
        compiler_params=_params("parallel", "parallel"),
    )(cvec, ada_w, ada_b.reshape(depth, 1, n6))


def _prenorm_kernel(x_ref, mod_ref, g_ref, o_ref, obf_ref):
    d = D_MODEL
    h = _norm_mod(x_ref[...], g_ref[...], mod_ref[:, d:2 * d], mod_ref[:, 0:d])
    o_ref[...] = h
    obf_ref[...] = h.astype(BF16)


def _prenorm(x, mods, g, layer, dims):
    batch, lat_len, _, tm = dims
    ntok, d = x.shape
    return pl.pallas_call(
        _prenorm_kernel,
        grid=(ntok // tm,),
        in_specs=[pl.BlockSpec((tm, d), lambda i: (i, 0)),
                  _mod_spec(layer, tm, lat_len, batch, 6 * d, 0),
                  pl.BlockSpec((1, d), lambda i: (0, 0))],
        out_specs=[pl.BlockSpec((tm, d), lambda i: (i, 0)), pl.BlockSpec((tm, d), lambda i: (i, 0))],
        out_shape=[jax.ShapeDtypeStruct((ntok, d), F32), jax.ShapeDtypeStruct((ntok, d), BF16)],
        compiler_params=_params("parallel"),
    )(x, mods, g.reshape(1, d))


def _mm_norm_kernel(x_ref, mod_ref, g_ref, w_ref, b_ref, o_ref, h_scr):
    d = D_MODEL

    @pl.when(pl.program_id(1) == 0)
    def _():
        h_scr[...] = _norm_mod(x_ref[...], g_ref[...], mod_ref[:, d:2 * d], mod_ref[:, 0:d]).astype(BF16)

    o_ref[...] = _dot(h_scr[...], w_ref[...]) + b_ref[...]


def _mm_norm(x, mods, g, w, b, layer, dims, tn):
    batch, lat_len, _, tm = dims
    ntok, d = x.shape
    n = w.shape[1]
    return pl.pallas_call(
        _mm_norm_kernel,
        grid=(ntok // tm, n // tn),
        in_specs=[pl.BlockSpec((tm, d), lambda i, j: (i, 0)),
                  _mod_spec(layer, tm, lat_len, batch, 6 * d, 0),
                  pl.BlockSpec((1, d), lambda i, j: (0, 0)),
                  pl.BlockSpec((d, tn), lambda i, j: (0, j)),
                  pl.BlockSpec((1, tn), lambda i, j: (0, j))],
        out_specs=pl.BlockSpec((tm, tn), lambda i, j: (i, j)),
        out_shape=jax.ShapeDtypeStruct((ntok, n), F32),
        scratch_shapes=[pltpu.VMEM((tm, d), BF16)],
        compiler_params=_params("parallel", "arbitrary"),
    )(x, mods, g.reshape(1, d), w, b.reshape(1, n))


def _mm_res_kernel(a_ref, w_ref, b_ref, x_ref, gate_ref, o_ref):
    y = _dot(a_ref[...], w_ref[...]) + b_ref[...]
    o_ref[...] = x_ref[...] + gate_ref[...] * y


def _mm_res(a, w, b, x, mods, layer, dims, tn):
    batch, lat_len, _, tm = dims
    ntok, d = x.shape
    k = a.shape[1]
    gate_blk = 2 * d // tn
    tiles_per_seq = lat_len // tm
    return pl.pallas_call(
        _mm_res_kernel,
        grid=(ntok // tm, d // tn),
        in_specs=[pl.BlockSpec((tm, k), lambda i, j: (i, 0)),
                  pl.BlockSpec((k, tn), lambda i, j: (0, j)),
                  pl.BlockSpec((1, tn), lambda i, j: (0, j)),
                  pl.BlockSpec((tm, tn), lambda i, j: (i, j)),
                  pl.BlockSpec((None, 1, tn), lambda i, j: (
                      layer * 8 + jnp.minimum(i // tiles_per_seq, batch), 0, gate_blk + j))],
        out_specs=pl.BlockSpec((tm, tn), lambda i, j: (i, j)),
        out_shape=jax.ShapeDtypeStruct((ntok, d), F32),
        compiler_params=_params("parallel", "parallel"),
    )(a, w, b.reshape(1, d), x, mods)


def _ffn_kernel(x_ref, mod_ref, g_ref, wg_ref, wu_ref, wd_ref, o_ref, h_scr, acc_scr):
    d = D_MODEL
    f = pl.program_id(1)

    @pl.when(f == 0)
    def _():
        h_scr[...] = _norm_mod(x_ref[...], g_ref[...], mod_ref[:, 4 * d:5 * d],
                               mod_ref[:, 3 * d:4 * d]).astype(BF16)
        acc_scr[...] = jnp.zeros_like(acc_scr)

    h = h_scr[...]
    gate = _dot(h, wg_ref[...])
    up = _dot(h, wu_ref[...])
    act = (_silu(gate) * up).astype(BF16)
    acc_scr[...] += _dot(act, wd_ref[...])

    @pl.when(f == pl.num_programs(1) - 1)
    def _():
        o_ref[...] = x_ref[...] + mod_ref[:, 5 * d:6 * d] * acc_scr[...]


def _ffn(x, mods, g, wg, wu, wd, layer, dims):
    batch, lat_len, _, tm = dims
    ntok, d = x.shape
    dff = wg.shape[1]
    tf = 512
    return pl.pallas_call(
        _ffn_kernel,
        grid=(ntok // tm, dff // tf),
        in_specs=[pl.BlockSpec((tm, d), lambda i, f: (i, 0)),
                  _mod_spec(layer, tm, lat_len, batch, 6 * d, 0),
                  pl.BlockSpec((1, d), lambda i, f: (0, 0)),
                  pl.BlockSpec((d, tf), lambda i, f: (0, f)),
                  pl.BlockSpec((d, tf), lambda i, f: (0, f)),
                  pl.BlockSpec((tf, d), lambda i, f: (f, 0))],
        out_specs=pl.BlockSpec((tm, d), lambda i, f: (i, 0)),
        out_shape=jax.ShapeDtypeStruct((ntok, d), F32),
        scratch_shapes=[pltpu.VMEM((tm, d), BF16), pltpu.VMEM((tm, d), F32)],
        compiler_params=_params("parallel", "arbitrary"),
    )(x, mods, g.reshape(1, d), wg, wu, wd)


def _final_norm_kernel(x_ref, g_ref, o_ref):
    x = x_ref[...]
    o_ref[...] = x * lax.rsqrt(jnp.mean(x * x, axis=-1, keepdims=True) + EPS) * g_ref[...]


def _final_norm(x, g, nrows, tm):
    d = x.shape[1]
    return pl.pallas_call(
        _final_norm_kernel,
        grid=(nrows // tm,),
        in_specs=[pl.BlockSpec((tm, d), lambda i: (i, 0)), pl.BlockSpec((1, d), lambda i: (0, 0))],
        out_specs=pl.BlockSpec((tm, d), lambda i: (i, 0)),
        out_shape=jax.ShapeDtypeStruct((nrows, d), F32),
        compiler_params=_params("parallel"),
    )(x, g.reshape(1, d))


def _s5_tables(a_re, a_im, log_dt, b_re, b_im, c_re, c_im):
    t_len = S5_CHUNK
    dt = jnp.exp(log_dt)[..., None]
    lr, li = a_re * dt, a_im * dt
    er = jnp.exp(lr)
    nr, ni = er * jnp.cos(li) - 1.0, er * jnp.sin(li)
    den = a_re * a_re + a_im * a_im
    qr, qi = (nr * a_re + ni * a_im) / den, (ni * a_re - nr * a_im) / den
    bbr = qr[..., None] * b_re - qi[..., None] * b_im
    bbi = qr[..., None] * b_im + qi[..., None] * b_re
    tau0 = t_len - 1
    taus = jnp.arange(-tau0, t_len + 1, dtype=F32)[:, None, None, None]
    mag = jnp.exp(lr[None] * taus)
    pr, pi = mag * jnp.cos(li[None] * taus), mag * jnp.sin(li[None] * taus)
    clr = c_re[None] * pr[:, :, :, None, :] - c_im[None] * pi[:, :, :, None, :]
    cli = c_re[None] * pi[:, :, :, None, :] + c_im[None] * pr[:, :, :, None, :]
    ti = jnp.arange(t_len)
    ngroups = a_re.shape[1]
    w_sr, w_si, w_or, w_oi, w_nr, w_ni = [], [], [], [], [], []
    for direction in range(2):
        st_pow = tau0 + ((t_len - 1 - ti) if direction == 0 else ti)
        sr = pr[st_pow, direction][..., None] * bbr[direction][None] - pi[st_pow, direction][..., None] * bbi[direction][None]
        si = pr[st_pow, direction][..., None] * bbi[direction][None] + pi[st_pow, direction][..., None] * bbr[direction][None]
        w_sr.append(sr.transpose(1, 0, 3, 2).reshape(ngroups, S5_ROW, S5_STATE))
        w_si.append(si.transpose(1, 0, 3, 2).reshape(ngroups, S5_ROW, S5_STATE))
        out_pow = tau0 + ((ti + 1) if direction == 0 else (t_len - ti))
        neg_pow = tau0 + ((ti - (t_len - 1)) if direction == 0 else -ti)
        relay = lambda t: t.transpose(1, 3, 0, 2).reshape(ngroups, S5_STATE, S5_ROW)
        w_or.append(relay(clr[out_pow, direction]))
        w_oi.append(relay(-cli[out_pow, direction]))
        w_nr.append(relay(clr[neg_pow, direction]))
        w_ni.append(relay(-cli[neg_pow, direction]))
    return (jnp.stack(w_sr), jnp.stack(w_si), jnp.stack(w_nr), jnp.stack(w_ni),
            jnp.stack(w_or).astype(BF16), jnp.stack(w_oi).astype(BF16),
            pr[tau0 + t_len], pi[tau0 + t_len])


def _s5_perm():
    src = jnp.arange(S5_SLAB, dtype=jnp.int32)
    t, g, c = src // 128, (src % 128) // S5_GROUP, src % S5_GROUP
    dst = g * S5_ROW + t * S5_GROUP + c
    return (dst[:, None] == jnp.arange(S5_SLAB, dtype=jnp.int32)[None, :]).astype(BF16)


def _s5_kernel(*refs, nparts, split_steps, batch, ncc, ncl, col_chunks):
    nin = S5_CHUNK if split_steps else nparts
    x_refs = refs[:nin]
    perm_ref, wsr_ref, wsi_ref, wnr_ref, wni_ref, wor_ref, woi_ref, lr_ref, li_ref = refs[nin:nin + 9]
    y_refs = refs[nin + 9:nin + 9 + nparts]
    sr_scr, si_scr, hr_scr, hi_scr, y_scr = refs[nin + 9 + nparts:]
    rows = y_scr.shape[0]
    groups = S5_SLAB_GROUPS

    def load(t):
        if split_steps:
            return x_refs[t][...]
        parts = [r[:, t].reshape(-1, 128) for r in x_refs]
        return (parts[0] if nparts == 1 else jnp.concatenate(parts, axis=0)).astype(BF16)

    xcat = jnp.concatenate([load(t) for t in range(S5_CHUNK)], axis=1)
    u_all = _dot(xcat, perm_ref[...]).astype(BF16)
    t_in = lax.broadcasted_iota(jnp.int32, (S5_ROW, S5_ROW), 0) // S5_GROUP
    t_out = lax.broadcasted_iota(jnp.int32, (S5_ROW, S5_ROW), 1) // S5_GROUP

    def hdot(a, b):
        a_hi, b_hi = a.astype(BF16), b.astype(BF16)
        a_lo, b_lo = (a - a_hi.astype(F32)).astype(BF16), (b - b_hi.astype(F32)).astype(BF16)
        return _dot(a_hi, b_hi) + (_dot(a_hi, b_lo) + _dot(a_lo, b_hi))

    def row_of(b, j, direction):
        if direction == 0:
            kc, kl = j, j - ncc
        else:
            kc, kl = ncc - 1 - j, ncl - 1 - (j - ncc)
        if col_chunks:
            kl = (kl % col_chunks) * GRID_W + kl // col_chunks
        return jnp.where(j < ncc, batch * ncl + b * ncc + kc, b * ncl + kl)

    for direction in range(2):
        for g in range(groups):
            u = u_all[:, g * S5_ROW:(g + 1) * S5_ROW]
            sr_scr[g * rows:(g + 1) * rows, :] = _dot(u, wsr_ref[direction, g].astype(BF16))
            si_scr[g * rows:(g + 1) * rows, :] = _dot(u, wsi_ref[direction, g].astype(BF16))
        lam_r = lr_ref[direction]
        lam_i = li_ref[direction]

        def step(j, carry, direction=direction, lam_r=lam_r, lam_i=lam_i):
            out = []
            for b in range(batch):
                h_r, h_i = carry[2 * b], carry[2 * b + 1]
                idx = pl.ds(row_of(b, j, direction), groups, stride=rows)
                hr_scr[idx, :] = h_r
                hi_scr[idx, :] = h_i
                s_r = sr_scr[idx, :]
                s_i = si_scr[idx, :]
                out += [lam_r * h_r - lam_i * h_i + s_r, lam_r * h_i + lam_i * h_r + s_i]
            return tuple(out)

        zero = jnp.zeros((groups, S5_STATE), F32)
        lax.fori_loop(0, ncc + ncl, step, (zero,) * (2 * batch))
        causal = (t_out >= t_in) if direction == 0 else (t_in >= t_out)
        for g in range(groups):
            u = u_all[:, g * S5_ROW:(g + 1) * S5_ROW]
            m_intra = jnp.where(causal, hdot(wsr_ref[direction, g], wnr_ref[direction, g])
                                + hdot(wsi_ref[direction, g], wni_ref[direction, g]), 0.0).astype(BF16)
            y = (_dot(u, m_intra)
                 + _dot(hr_scr[g * rows:(g + 1) * rows, :].astype(BF16), wor_ref[direction, g])
                 + _dot(hi_scr[g * rows:(g + 1) * rows, :].astype(BF16), woi_ref[direction, g]))
            if direction == 0:
                y_scr[:, g * S5_ROW:(g + 1) * S5_ROW] = y
            else:
                y_scr[:, g * S5_ROW:(g + 1) * S5_ROW] += y

    z = _dot_nt(y_scr[...].astype(BF16), perm_ref[...])
    for t in range(S5_CHUNK):
        zt = z[:, t * 128:(t + 1) * 128]
        r0 = 0
        for y_ref in y_refs:
            n = y_ref.shape[0] * (y_ref.shape[2] if len(y_ref.shape) == 4 else 1)
            y_ref[:, t] = zt[r0:r0 + n].reshape(y_ref.shape[:1] + y_ref.shape[2:])
            r0 += n


def _s5_core(h, hbf, tables, dims, col_major):
    batch, lat_len, ctx_len, _ = dims
    wsr, wsi, wnr, wni, wor, woi, lam_r, lam_i = tables
    d = D_MODEL
    ncc, ncl = ctx_len // S5_CHUNK, lat_len // S5_CHUNK
    rows = batch * (ncc + ncl)
    gs = S5_SLAB_GROUPS
    if col_major:
        grid_rows = lat_len // GRID_W
        assert grid_rows % S5_CHUNK == 0
        col_chunks = grid_rows // S5_CHUNK
        nlat = batch * lat_len
        xs = [h[:nlat].reshape(batch * col_chunks, S5_CHUNK, GRID_W, d),
              h[nlat:].reshape(batch * ncc, S5_CHUNK, d)]
        blocks = [pl.BlockSpec((batch * col_chunks, S5_CHUNK, GRID_W, 128), lambda q: (0, 0, 0, q)),
                  pl.BlockSpec((batch * ncc, S5_CHUNK, 128), lambda q: (0, 0, q))]
        ins, in_blocks = xs, blocks
    else:
        col_chunks = 0
        xs = [h.reshape(rows, S5_CHUNK, d)]
        blocks = [pl.BlockSpec((rows, S5_CHUNK, 128), lambda q: (0, 0, q))]
        ins = [hbf.reshape(rows, S5_CHUNK * d)] * S5_CHUNK
        in_blocks = [pl.BlockSpec((rows, 128), lambda q, t=t: (0, t * (d // 128) + q)) for t in range(S5_CHUNK)]
    kern = functools.partial(_s5_kernel, nparts=len(xs), split_steps=not col_major, batch=batch, ncc=ncc, ncl=ncl,
                             col_chunks=col_chunks)
    wspec = lambda a, b: pl.BlockSpec((2, gs, a, b), lambda q: (0, q, 0, 0))
    ys = pl.pallas_call(
        kern,
        grid=(S5_GROUPS // gs,),
        in_specs=in_blocks + [pl.BlockSpec((S5_SLAB, S5_SLAB), lambda q: (0, 0)),
                           wspec(S5_ROW, S5_STATE), wspec(S5_ROW, S5_STATE),
                           wspec(S5_STATE, S5_ROW), wspec(S5_STATE, S5_ROW),
                           wspec(S5_STATE, S5_ROW), wspec(S5_STATE, S5_ROW),
                           pl.BlockSpec((2, gs, S5_STATE), lambda q: (0, q, 0)),
                           pl.BlockSpec((2, gs, S5_STATE), lambda q: (0, q, 0))],
        out_specs=blocks,
        out_shape=[jax.ShapeDtypeStruct(x.shape, F32) for x in xs],
        scratch_shapes=[pltpu.VMEM((gs * rows, S5_STATE), F32) for _ in range(4)]
        + [pltpu.VMEM((rows, S5_SLAB), F32)],
        compiler_params=_params("parallel"),
    )(*ins, _s5_perm(), wsr, wsi, wnr, wni, wor, woi, lam_r, lam_i)
    if col_major:
        return jnp.concatenate([ys[0].reshape(-1, d), ys[1].reshape(-1, d)], axis=0)
    return ys[0].reshape(-1, d)


def _s5_glu_kernel(h_ref, y_ref, dskip_ref, w1_ref, w2_ref, b1_ref, b2_ref, x_ref, gate_ref, o_ref, z_scr):
    @pl.when(pl.program_id(1) == 0)
    def _():
        y = h_ref[...] * dskip_ref[...] + y_ref[...]
        z = 0.5 * y * (1.0 + jnp.tanh(math.sqrt(2.0 / math.pi) * (y + 0.044715 * (y * y * y))))
        z_scr[...] = z.astype(BF16)

    z = z_scr[...]
    lin = _dot(z, w1_ref[...]) + b1_ref[...]
    gat = _dot(z, w2_ref[...]) + b2_ref[...]
    o_ref[...] = x_ref[...] + gate_ref[...] * (lin * _sigmoid(gat))


def _s5_glu(h, y, dskip, w, b, x, mods, layer, dims):
    batch, lat_len, _, tm = dims
    ntok, d = x.shape
    tn = 512
    nblk = d // tn
    tiles_per_seq = lat_len // tm
    b = b.reshape(1, 2 * d)
    return pl.pallas_call(
        _s5_glu_kernel,
        grid=(ntok // tm, nblk),
        in_specs=[pl.BlockSpec((tm, d), lambda i, j: (i, 0)),
                  pl.BlockSpec((tm, d), lambda i, j: (i, 0)),
                  pl.BlockSpec((1, d), lambda i, j: (0, 0)),
                  pl.BlockSpec((d, tn), lambda i, j: (0, j)),
                  pl.BlockSpec((d, tn), lambda i, j: (0, nblk + j)),
                  pl.BlockSpec((1, tn), lambda i, j: (0, j)),
                  pl.BlockSpec((1, tn), lambda i, j: (0, nblk + j)),
                  pl.BlockSpec((tm, tn), lambda i, j: (i, j)),
                  pl.BlockSpec((None, 1, tn), lambda i, j: (
                      layer * 8 + jnp.minimum(i // tiles_per_seq, batch), 0, 2 * nblk + j))],
        out_specs=pl.BlockSpec((tm, tn), lambda i, j: (i, j)),
        out_shape=jax.ShapeDtypeStruct((ntok, d), F32),
        scratch_shapes=[pltpu.VMEM((tm, d), BF16)],
        compiler_params=_params("parallel", "arbitrary"),
    )(h, y, dskip.reshape(1, d), w, w, b, b, x, mods)


def _s5_layer(x, mods, layer, j, dims, p):
    batch, lat_len, ctx_len, _ = dims
    col_major = (j % 2) == 1
    h, hbf = _prenorm(x, mods, p['norm_g'][layer, 0], layer, dims)
    tables = _s5_tables(p['s5_a_re'][j], p['s5_a_im'][j], p['s5_log_dt'][j], p['s5_b_re'][j],
                        p['s5_b_im'][j], p['s5_c_re'][j], p['s5_c_im'][j])
    y = _s5_core(h, hbf, tables, dims, col_major)
    return _s5_glu(h, y, p['s5_d'][j], p['s5_glu_w'][j].astype(BF16), p['s5_glu_b'][j], x, mods, layer, dims)


def _seq_blocks(dims):
    batch, lat_len, ctx_len, _ = dims
    return [(lat_len, 0), (ctx_len, batch * lat_len // ctx_len)]


def _dwconv3_kernel(u_ref, w_ref, b_ref, o_ref, *maybe_bf16_ref):
    x = u_ref[...]
    n = x.shape[0]
    row = lax.broadcasted_iota(jnp.int32, x.shape, 0)
    prev = jnp.where(row == 0, 0.0, pltpu.roll(x, 1, 0))
    nxt = jnp.where(row == n - 1, 0.0, pltpu.roll(x, n - 1, 0))
    y = prev * w_ref[0:1, :] + x * w_ref[1:2, :] + nxt * w_ref[2:3, :] + b_ref[...]
    o_ref[...] = y
    for r in maybe_bf16_ref:
        r[...] = y.astype(BF16)


def _dwconv3(u, w, b, dims, col0, ncols, with_bf16):
    batch = dims[0]
    ntok = u.shape[0]
    tc = 256
    cb0 = col0 // tc
    w8 = jnp.zeros((8, w.shape[1]), F32).at[:3].set(w)
    outs = []
    for n, off in _seq_blocks(dims):
        out_shape = [jax.ShapeDtypeStruct((batch * n, ncols), F32)]
        out_specs = [pl.BlockSpec((n, tc), lambda s, j: (s, j))]
        if with_bf16:
            out_shape.append(jax.ShapeDtypeStruct((batch * n, ncols), BF16))
            out_specs.append(pl.BlockSpec((n, tc), lambda s, j: (s, j)))
        outs.append(pl.pallas_call(
            _dwconv3_kernel,
            grid=(batch, ncols // tc),
            in_specs=[pl.BlockSpec((n, tc), lambda s, j, off=off: (off + s, cb0 + j)),
                      pl.BlockSpec((8, tc), lambda s, j: (0, cb0 + j)),
                      pl.BlockSpec((1, tc), lambda s, j: (0, cb0 + j))],
            out_specs=out_specs,
            out_shape=out_shape,
            compiler_params=_params("parallel", "parallel"),
        )(u, w8, b.reshape(1, -1)))
    return outs


def _phase_table_kernel(ar_ref, ai_ref, br_ref, bi_ref, c_ref, s_ref):
    br, bi = br_ref[...], bi_ref[...]
    for r in range(ar_ref.shape[0]):
        ar, ai = ar_ref[r:r + 1, :], ai_ref[r:r + 1, :]
        c_ref[r * 16:(r + 1) * 16, :] = (ar * br - ai * bi).astype(BF16)
        s_ref[r * 16:(r + 1) * 16, :] = (ai * br + ar * bi).astype(BF16)


def _phase_tables(coarse, fine, period):
    def unit(m):
        ang = (m % period).astype(F32) * (2.0 * math.pi / period)
        return jnp.cos(ang), jnp.sin(ang)

    ar, ai = unit(coarse)
    br, bi = unit(fine)
    r1, ncol = coarse.shape
    rb = min(32, r1)
    out = jax.ShapeDtypeStruct((r1 * 16, ncol), BF16)
    return pl.pallas_call(
        _phase_table_kernel,
        grid=(r1 // rb,),
        in_specs=[pl.BlockSpec((rb, ncol), lambda i: (i, 0)), pl.BlockSpec((rb, ncol), lambda i: (i, 0)),
                  pl.BlockSpec((16, ncol), lambda i: (0, 0)), pl.BlockSpec((16, ncol), lambda i: (0, 0))],
        out_specs=[pl.BlockSpec((rb * 16, ncol), lambda i: (i, 0)), pl.BlockSpec((rb * 16, ncol), lambda i: (i, 0))],
        out_shape=[out, out],
        compiler_params=_params("parallel"),
    )(ar, ai, br, bi)


def _dft_tables(n):
    m = n // 2
    idx = jnp.arange(m, dtype=jnp.int32)[None, :]
    r1 = jnp.arange(m // 16, dtype=jnp.int32)[:, None]
    r0 = jnp.arange(16, dtype=jnp.int32)[:, None]
    fwd, inv = [], []
    for parity in range(2):
        tau = 2 * idx + parity
        fwd.append(_phase_tables(32 * r1 * tau, (2 * r0 + 1) * tau, 4 * n))
        odd = 2 * idx + 1
        inv.append(_phase_tables(32 * r1 * odd, (2 * r0 + parity) * odd, 4 * n))
    stack = lambda tabs, which: jnp.stack([t[which] for t in tabs])
    return stack(fwd, 0), stack(fwd, 1), stack(inv, 0), stack(inv, 1)


def _hy_filter_kernel(feat_ref, w1_ref, b1_ref, w2_ref, b2_ref, w3_ref, b3_ref, fq_ref, tu_ref, dl_ref,
                      w4_ref, hs_ref, hd_ref, hdn_scr):
    @pl.when(pl.program_id(0) == 0)
    def _():
        fq = fq_ref[...]
        h = jnp.sin(fq * (jnp.dot(feat_ref[...], w1_ref[...], precision=HIGHEST, preferred_element_type=F32) + b1_ref[...]))
        h = jnp.sin(fq * (jnp.dot(h, w2_ref[...], precision=HIGHEST, preferred_element_type=F32) + b2_ref[...]))
        h = jnp.sin(fq * (jnp.dot(h, w3_ref[...], precision=HIGHEST, preferred_element_type=F32) + b3_ref[...]))
        hdn_scr[...] = h

    hdn = hdn_scr[...]
    decay = jnp.exp(-tu_ref[...] * dl_ref[...])
    row = lax.broadcasted_iota(jnp.int32, decay.shape, 0)
    for order in range(HY_ORDER):
        fwd = jnp.dot(hdn, w4_ref[2 * order], precision=HIGHEST, preferred_element_type=F32) * decay
        bwd = jnp.dot(hdn, w4_ref[2 * order + 1], precision=HIGHEST, preferred_element_type=F32) * decay
        bwd = jnp.where(row == 0, 0.0, bwd)
        norm = jnp.sum(jnp.abs(fwd), axis=0, keepdims=True) + jnp.sum(jnp.abs(bwd), axis=0, keepdims=True)
        hs_ref[order] = ((fwd + bwd) / norm).astype(BF16)
        hd_ref[order] = ((fwd - bwd) / norm).astype(BF16)


def _hy_filters(n, p, j):
    d = D_MODEL
    t = jnp.arange(n, dtype=F32)
    t_unit = t / max(n - 1, 1)
    bands = jnp.linspace(1e-4, HY_BANDS - 1, HY_BANDS, dtype=F32)
    ang = (2.0 * math.pi / n) * t[:, None] * bands[None, :]
    feats = jnp.concatenate([t_unit[:, None], jnp.cos(ang), -jnp.sin(ang)], axis=-1)
    feats = jnp.pad(feats, ((0, 0), (0, HY_EMB_PAD - HY_EMB)))
    w1 = jnp.pad(p['hy_f_w1'][j], ((0, HY_EMB_PAD - HY_EMB), (0, 0)))
    deltas = jnp.abs(jnp.linspace(HY_MIN_DECAY, HY_MAX_DECAY, d, dtype=F32)).reshape(1, d)
    w4 = p['hy_f_w4'][j].reshape(HY_FILTER_HIDDEN, 2 * HY_ORDER, d).transpose(1, 0, 2)
    hid = HY_FILTER_HIDDEN
    tc = 128
    full = lambda shape: pl.BlockSpec(shape, lambda c: tuple(0 for _ in shape))
    row = lambda v: v.reshape(1, hid)
    return pl.pallas_call(
        _hy_filter_kernel,
        grid=(d // tc,),
        in_specs=[full((n, HY_EMB_PAD)), full((HY_EMB_PAD, hid)), full((1, hid)), full((hid, hid)), full((1, hid)),
                  full((hid, hid)), full((1, hid)), full((1, hid)), full((n, 1)),
                  pl.BlockSpec((1, tc), lambda c: (0, c)),
                  pl.BlockSpec((2 * HY_ORDER, hid, tc), lambda c: (0, 0, c))],
        out_specs=[pl.BlockSpec((HY_ORDER, n, tc), lambda c: (0, 0, c)),
                   pl.BlockSpec((HY_ORDER, n, tc), lambda c: (0, 0, c))],
        out_shape=[jax.ShapeDtypeStruct((HY_ORDER, n, d), BF16), jax.ShapeDtypeStruct((HY_ORDER, n, d), BF16)],
        scratch_shapes=[pltpu.VMEM((n, hid), F32)],
        compiler_params=_params("arbitrary"),
    )(feats, w1, row(p['hy_f_b1'][j]), p['hy_f_w2'][j], row(p['hy_f_b2'][j]), p['hy_f_w3'][j],
      row(p['hy_f_b3'][j]), row(p['hy_f_freq'][j]), t_unit.reshape(n, 1), deltas, w4)


def _half_spectra(c_ref, s_ref, xe_c, xo_c, xe_s, xo_s):
    a, b = _dot(c_ref[0], xe_c), _dot(c_ref[1], xo_c)
    cs, ds = _dot(s_ref[0], xe_s), _dot(s_ref[1], xo_s)
    return a + b, a - b, cs + ds, ds - cs


def _dft_filter_kernel(c_ref, s_ref, hse_ref, hso_ref, hde_ref, hdo_ref, hr_ref, hi_ref):
    pc, qc, ps, qs = _half_spectra(c_ref, s_ref, hse_ref[...], hso_ref[...], hde_ref[...], hdo_ref[...])
    hr_ref[0], hr_ref[1] = pc, qc
    hi_ref[0], hi_ref[1] = -ps, -qs


def _dft_filter(cmat, smat, hs, hd, tk, td):
    order, n, d = hs.shape
    m = n // 2
    nd = d // td
    hs2, hd2 = hs.reshape(order, m, 2 * d), hd.reshape(order, m, 2 * d)
    tab = pl.BlockSpec((2, tk, m), lambda i, o, j: (0, i, 0))
    even = pl.BlockSpec((None, m, td), lambda i, o, j: (o, 0, j))
    odd = pl.BlockSpec((None, m, td), lambda i, o, j: (o, 0, nd + j))
    out = pl.BlockSpec((None, 2, tk, td), lambda i, o, j: (o, 0, i, j))
    sds = jax.ShapeDtypeStruct((order, 2, m, d), F32)
    return pl.pallas_call(
        _dft_filter_kernel,
        grid=(m // tk, order, nd),
        in_specs=[tab, tab, even, odd, even, odd],
        out_specs=[out, out],
        out_shape=[sds, sds],
        compiler_params=_params("parallel", "parallel", "parallel"),
    )(cmat, smat, hs2, hs2, hd2, hd2)


def _dft_fwd_kernel(c_ref, s_ref, ze_ref, zo_ref, hr_ref, hi_ref, ua_ref, ub_ref):
    ze, zo = ze_ref[...], zo_ref[...]
    pc, qc, ps, qs = _half_spectra(c_ref, s_ref, ze, zo, ze, zo)
    yrp = pc * hr_ref[0] + ps * hi_ref[0]
    yip = pc * hi_ref[0] - ps * hr_ref[0]
    yrq = qc * hr_ref[1] + qs * hi_ref[1]
    yiq = qc * hi_ref[1] - qs * hr_ref[1]
    ua_ref[0] = (yrp + yrq).astype(BF16)
    ub_ref[0] = (yip - yiq).astype(BF16)
    ua_ref[1] = (yrp - yrq).astype(BF16)
    ub_ref[1] = (yip + yiq).astype(BF16)


def _dft_fwd(cmat, smat, z2, hr, hi, order, batch, tk, td):
    m = cmat.shape[1]
    d = z2.shape[1] // 2
    nd = d // td
    tab = pl.BlockSpec((2, tk, m), lambda i, s, j: (0, i, 0))
    hspec = pl.BlockSpec((None, 2, tk, td), lambda i, s, j: (order, 0, i, j))
    out = pl.BlockSpec((2, None, tk, td), lambda i, s, j: (0, s, i, j))
    sds = jax.ShapeDtypeStruct((2, batch, m, d), BF16)
    return pl.pallas_call(
        _dft_fwd_kernel,
        grid=(m // tk, batch, nd),
        in_specs=[tab, tab,
                  pl.BlockSpec((m, td), lambda i, s, j: (s, j)),
                  pl.BlockSpec((m, td), lambda i, s, j: (s, nd + j)),
                  hspec, hspec],
        out_specs=[out, out],
        out_shape=[sds, sds],
        compiler_params=_params("parallel", "parallel", "parallel"),
    )(cmat, smat, z2, z2, hr, hi)


def _dft_inv_kernel(ct_ref, st_ref, ua_ref, ub_ref, xg_ref, zp_ref, bias_ref, o_ref, obf_ref, *, inv_n):
    y = (_dot(ct_ref[...], ua_ref[...]) - _dot(st_ref[...], ub_ref[...])) * inv_n
    out = xg_ref[...] * (y + zp_ref[...] * bias_ref[...])
    o_ref[...] = out
    obf_ref[...] = out.astype(BF16)


def _dft_inv(ctm, stm, ua, ub, xg2, xg_col0, zprev2, bias, batch, ts, td):
    m = ctm.shape[1]
    d = zprev2.shape[1] // 2
    nd = d // td
    gw = xg2.shape[1] // 2
    rblk = lambda par, i, s, j: s * (m // ts) + i
    kern = functools.partial(_dft_inv_kernel, inv_n=1.0 / (2 * m))
    return pl.pallas_call(
        kern,
        grid=(2, m // ts, batch, nd),
        in_specs=[pl.BlockSpec((None, ts, m), lambda par, i, s, j: (par, i, 0)),
                  pl.BlockSpec((None, ts, m), lambda par, i, s, j: (par, i, 0)),
                  pl.BlockSpec((None, None, m, td), lambda par, i, s, j: (par, s, 0, j)),
                  pl.BlockSpec((None, None, m, td), lambda par, i, s, j: (par, s, 0, j)),
                  pl.BlockSpec((ts, td), lambda par, i, s, j: (rblk(par, i, s, j), (par * gw + xg_col0) // td + j)),
                  pl.BlockSpec((ts, td), lambda par, i, s, j: (rblk(par, i, s, j), par * nd + j)),
                  pl.BlockSpec((1, td), lambda par, i, s, j: (0, j))],
        out_specs=[pl.BlockSpec((ts, td), lambda par, i, s, j: (rblk(par, i, s, j), par * nd + j)),
                   pl.BlockSpec((ts, td), lambda par, i, s, j: (rblk(par, i, s, j), par * nd + j))],
        out_shape=[jax.ShapeDtypeStruct(zprev2.shape, F32), jax.ShapeDtypeStruct(zprev2.shape, BF16)],
        compiler_params=_params("parallel", "parallel", "parallel", "parallel"),
    )(ctm, stm, ua, ub, xg2, zprev2, bias.reshape(1, d))


def _hyena_layer(x, mods, layer, j, dims, p):
    batch, lat_len, ctx_len, _ = dims
    d = D_MODEL
    u = _mm_norm(x, mods, p['norm_g'][layer, 0], p['hy_in_w'][j].astype(BF16), p['hy_in_b'][j], layer, dims, 512)
    v_parts = _dwconv3(u, p['hy_conv_w'][j], p['hy_conv_b'][j], dims, 0, d, True)
    g_parts = _dwconv3(u, p['hy_conv_w'][j], p['hy_conv_b'][j], dims, d, 2 * d, False)
    z_out = []
    for (n, _), (v32, vbf), (gates,) in zip(_seq_blocks(dims), v_parts, g_parts):
        m = n // 2
        tk = min(512, m)
        td = 512
        cmat, smat, ctm, stm = _dft_tables(n)
        hs, hd = _hy_filters(n, p, j)
        hr, hi = _dft_filter(cmat, smat, hs, hd, tk, td)
        zprev32, zprevbf = v32.reshape(batch * m, 2 * d), vbf.reshape(batch * m, 2 * d)
        gates2 = gates.reshape(batch * m, 4 * d)
        for order in range(HY_ORDER):
            ua, ub = _dft_fwd(cmat, smat, zprevbf, hr, hi, order, batch, tk, td)
            zprev32, zprevbf = _dft_inv(ctm, stm, ua, ub, gates2, order * d, zprev32, p['hy_bias'][j, order],
                                        batch, tk, td)
        z_out.append(zprevbf.reshape(batch * n, d))
    z = jnp.concatenate(z_out, axis=0)
    return _mm_res(z, p['hy_out_w'][j].astype(BF16), p['hy_out_b'][j], x, mods, layer, dims, 512)


def _gdn_conv_kernel(p_ref, w_ref, o_ref):
    x = p_ref[...]
    n = x.shape[0]
    row = lax.broadcasted_iota(jnp.int32, x.shape, 0)
    acc = x * w_ref[2:3, :]
    for s in (1, 2):
        prev = jnp.where(row < s, 0.0, pltpu.roll(x, s, 0))
        nxt = jnp.where(row >= n - s, 0.0, pltpu.roll(x, n - s, 0))
        acc = acc + prev * w_ref[2 - s:3 - s, :] + nxt * w_ref[2 + s:3 + s, :]
    y = _silu(acc)
    head = pl.program_id(1)
    inv = lax.rsqrt(jnp.sum(y * y, axis=-1, keepdims=True) + 1e-6)
    inv = inv * jnp.where(head < GDN_K_HEADS, GDN_HEAD ** -0.5, 1.0)
    o_ref[...] = y * jnp.where(head < 2 * GDN_K_HEADS, inv, 1.0)


def _gdn_conv(proj, w, dims):
    batch = dims[0]
    ntok = proj.shape[0]
    nheads = GDN_CONV_DIM // GDN_HEAD
    w8 = jnp.zeros((8, GDN_CONV_DIM), F32).at[:5].set(w)
    out = None
    for n, off in _seq_blocks(dims):
        args = [proj, w8]
        in_specs = [pl.BlockSpec((n, GDN_HEAD), lambda s, h, off=off: (off + s, h)),
                    pl.BlockSpec((8, GDN_HEAD), lambda s, h: (0, h))]
        aliases = {}
        if out is not None:
            args.append(out)
            in_specs.append(pl.BlockSpec(memory_space=pl.ANY))
            aliases = {2: 0}
        kern = _gdn_conv_kernel if out is None else (lambda p_ref, w_ref, _, o_ref: _gdn_conv_kernel(p_ref, w_ref, o_ref))
        out = pl.pallas_call(
            kern,
            grid=(batch, nheads),
            in_specs=in_specs,
            out_specs=pl.BlockSpec((None, n, GDN_HEAD), lambda s, h, off=off: (h, off + s, 0)),
            out_shape=jax.ShapeDtypeStruct((nheads, ntok, GDN_HEAD), F32),
            input_output_aliases=aliases,
            compiler_params=_params("parallel", "parallel"),
        )(*args)
    return out


def _gdn_gate_kernel(ab_ref, alog_ref, dtb_ref, o_ref):
    c = GDN_CHUNK
    row = lax.broadcasted_iota(jnp.int32, (c, c), 0)
    col = lax.broadcasted_iota(jnp.int32, (c, c), 1)
    lower = jnp.where(row >= col, 1.0, 0.0).astype(F32)
    upper = jnp.where(row <= col, 1.0, 0.0).astype(F32)
    lane = lax.broadcasted_iota(jnp.int32, (c, 128), 1)
    for r in range(ab_ref.shape[0] // c):
        ab = ab_ref[r * c:(r + 1) * c, :]
        xs = ab + dtb_ref[...]
        softplus = jnp.maximum(xs, 0.0) + jnp.log(1.0 + jnp.exp(-jnp.abs(xs)))
        g = -jnp.exp(alog_ref[...]) * softplus
        cum_f = jnp.dot(lower, g, precision=HIGHEST, preferred_element_type=F32)
        cum_b = jnp.dot(upper, g, precision=HIGHEST, preferred_element_type=F32)
        o_ref[r * c:(r + 1) * c, :] = jnp.where(lane < GDN_V_HEADS, cum_f,
                                                jnp.where(lane < 2 * GDN_V_HEADS, cum_b, _sigmoid(ab)))


def _gdn_gates(ab, a_log, dt_bias, tm):
    ntok = ab.shape[0]
    pad = jnp.zeros((2 * GDN_V_HEADS,), F32)
    alog = jnp.concatenate([a_log.reshape(-1), pad]).reshape(1, 128)
    dtb = jnp.concatenate([dt_bias.reshape(-1), pad]).reshape(1, 128)
    return pl.pallas_call(
        _gdn_gate_kernel,
        grid=(ntok // tm,),
        in_specs=[pl.BlockSpec((tm, 128), lambda i: (i, 0)),
                  pl.BlockSpec((1, 128), lambda i: (0, 0)),
                  pl.BlockSpec((1, 128), lambda i: (0, 0))],
        out_specs=pl.BlockSpec((tm, 128), lambda i: (i, 0)),
        out_shape=jax.ShapeDtypeStruct((ntok, 128), F32),
        compiler_params=_params("parallel"),
    )(ab, alog, dtb)


def _unit_triangular_inverses(mats):
    c = mats[0].shape[0]
    row = lax.broadcasted_iota(jnp.int32, (c, c), 0)
    col = lax.broadcasted_iota(jnp.int32, (c, c), 1)
    eye = jnp.where(row == col, 1.0, 0.0).astype(F32)
    ts = [eye - a for a in mats]
    pws = list(mats)
    for _ in range(int(math.log2(c)) - 1):
        pwbs = [pw.astype(BF16) for pw in pws]
        pws = [_dot(pwb, pwb) for pwb in pwbs]
        ts = [t + _dot(t.astype(BF16), pw.astype(BF16)) for t, pw in zip(ts, pws)]
    ms = [eye + a for a in mats]
    m_his = [m.astype(BF16) for m in ms]
    m_los = [(m - m_hi.astype(F32)).astype(BF16) for m, m_hi in zip(ms, m_his)]
    t_his = [t.astype(BF16) for t in ts]
    t_los = [(t - t_hi.astype(F32)).astype(BF16) for t, t_hi in zip(ts, t_his)]
    resids = [eye - (_dot(m_hi, t_hi) + (_dot(m_hi, t_lo) + _dot(m_lo, t_hi)))
              for m_hi, m_lo, t_hi, t_lo in zip(m_his, m_los, t_his, t_los)]
    return [t + _dot(t_hi, r.astype(BF16)) for t, t_hi, r in zip(ts, t_his, resids)]


def _gdn_chunk_kernel(qf_ref, kf_ref, vf_ref, gcf_ref, grf_ref, qb_ref, kb_ref, vb_ref, gcb_ref, grb_ref,
                      of_ref, ob_ref, state_scr, *, nchunks):
    c = GDN_CHUNK

    @pl.when(pl.program_id(2) == 0)
    def _():
        state_scr[...] = jnp.zeros_like(state_scr)

    row = lax.broadcasted_iota(jnp.int32, (c, c), 0)
    col = lax.broadcasted_iota(jnp.int32, (c, c), 1)
    blocks = ((qf_ref, kf_ref, vf_ref, gcf_ref, grf_ref, of_ref), (qb_ref, kb_ref, vb_ref, gcb_ref, grb_ref, ob_ref))
    nkh = qf_ref.shape[0]
    keys, amats, part = [], [], {}
    for direction, (q_ref, k_ref, v_ref, gc_ref, gr_ref, o_ref) in enumerate(blocks):
        if direction == 0:
            incl, strict, last = row >= col, row > col, c - 1
        else:
            incl, strict, last = row <= col, row < col, 0
        for kh in range(nkh):
            for ci in range(nchunks):
                rows = slice(ci * c, (ci + 1) * c)
                q = q_ref[kh, rows, :].astype(BF16)
                k32 = k_ref[kh, rows, :]
                k = k32.astype(BF16)
                k_t = k32.T
                gates_c = gc_ref[kh, ci]
                gates_r = gr_ref[kh, ci]
                kk = _dot_nt(k, k)
                qk = _dot_nt(q, k)
                for e in range(2):
                    ch_g, ch_b = direction * 2 + e, 4 + direction * 2 + e
                    gcc, gcr = gates_c[:, ch_g:ch_g + 1], gates_r[ch_g:ch_g + 1, :]
                    beta_c, beta_r = gates_c[:, ch_b:ch_b + 1], gates_r[ch_b:ch_b + 1, :]
                    decay = jnp.where(incl, jnp.exp(jnp.where(incl, gcc - gcr, 0.0)), 0.0)
                    key = (direction, kh, ci, e)
                    keys.append(key)
                    amats.append(jnp.where(strict, kk * beta_c * decay, 0.0))
                    g_last = gcr[:, last:last + 1]
                    part[key] = dict(
                        q=q, k=k, v=v_ref[2 * kh + e, rows, :].astype(BF16), attn=(qk * decay).astype(BF16),
                        beta_r=beta_r, wscale=beta_r * jnp.exp(gcr), egc=jnp.exp(gcc),
                        kg_t=(k_t * jnp.exp(g_last - gcr)).astype(BF16), e_last=jnp.exp(g_last))
    tmats = _unit_triangular_inverses(amats)
    local = {}
    for key, t in zip(keys, tmats):
        p = part[key]
        u = _dot((t * p['beta_r']).astype(BF16), p['v'])
        w = _dot((t * p['wscale']).astype(BF16), p['k']).astype(BF16)
        local[key] = (p['q'], u, w, p['attn'], p['egc'], p['kg_t'], p['e_last'])

    nslots = state_scr.shape[0]
    states = [state_scr[slot] for slot in range(nslots)]
    streams = [(direction, kh, e) for direction in range(2) for kh in range(nkh) for e in range(2)]
    out_refs = (of_ref, ob_ref)
    for step in range(nchunks):
        cur = [local[d, kh, (step if d == 0 else nchunks - 1 - step), e] for d, kh, e in streams]
        state_bs = [s.astype(BF16) for s in states]
        ws = [_dot(p[2], sb) for p, sb in zip(cur, state_bs)]
        qs = [_dot(p[0], sb) for p, sb in zip(cur, state_bs)]
        v_news = [(p[1] - w).astype(BF16) for p, w in zip(cur, ws)]
        intra = [_dot(p[3], vn) for p, vn in zip(cur, v_news)]
        upd = [_dot(p[5], vn) for p, vn in zip(cur, v_news)]
        for (d, kh, e), p, q_s, o_in in zip(streams, cur, qs, intra):
            ci = step if d == 0 else nchunks - 1 - step
            out_refs[d][2 * kh + e, ci * c:(ci + 1) * c, :] = p[4] * q_s + o_in
        states = [s * p[6] + dlt for s, p, dlt in zip(states, cur, upd)]
    for slot in range(nslots):
        state_scr[slot] = states[slot]


def _gdn_chunks(qkvh, gates, dims):
    batch, lat_len, ctx_len, _ = dims
    c = GDN_CHUNK
    ntok = qkvh.shape[1]
    per_step = min(4, ctx_len // c)
    rows = per_step * c
    nbc, nbl = ctx_len // rows, lat_len // rows
    ctx0 = batch * lat_len // rows
    gk = gates.reshape(ntok // c, c, 4, GDN_K_HEADS, 2).transpose(3, 0, 1, 2, 4).reshape(GDN_K_HEADS, ntok // c, c, 8)
    gk_t = gk.transpose(0, 1, 3, 2)

    def fwd_blk(b, j):
        return jnp.where(j < nbc, ctx0 + b * nbc + j, b * nbl + j - nbc)

    def bwd_blk(b, j):
        return jnp.where(j < nbc, ctx0 + b * nbc + nbc - 1 - j, b * nbl + nbl - 1 - (j - nbc))

    nkh = GDN_KHEADS_PER_STEP
    hblocks = GDN_K_HEADS // nkh

    def specs(blk):
        return [pl.BlockSpec((nkh, rows, GDN_HEAD), lambda b, h, j: (h, blk(b, j), 0)),
                pl.BlockSpec((nkh, rows, GDN_HEAD), lambda b, h, j: (hblocks + h, blk(b, j), 0)),
                pl.BlockSpec((2 * nkh, rows, GDN_HEAD), lambda b, h, j: (hblocks + h, blk(b, j), 0)),
                pl.BlockSpec((nkh, per_step, c, 8), lambda b, h, j: (h, blk(b, j), 0, 0)),
                pl.BlockSpec((nkh, per_step, 8, c), lambda b, h, j: (h, blk(b, j), 0, 0))]

    out_sd = jax.ShapeDtypeStruct((GDN_V_HEADS, ntok, GDN_HEAD), F32)
    return pl.pallas_call(
        functools.partial(_gdn_chunk_kernel, nchunks=per_step),
        grid=(batch, hblocks, nbc + nbl),
        in_specs=specs(fwd_blk) + specs(bwd_blk),
        out_specs=[pl.BlockSpec((2 * nkh, rows, GDN_HEAD), lambda b, h, j: (h, fwd_blk(b, j), 0)),
                   pl.BlockSpec((2 * nkh, rows, GDN_HEAD), lambda b, h, j: (h, bwd_blk(b, j), 0))],
        out_shape=[out_sd, out_sd],
        scratch_shapes=[pltpu.VMEM((4 * nkh, GDN_HEAD, GDN_HEAD), F32)],
        compiler_params=_params("parallel", "parallel", "arbitrary"),
    )(qkvh, qkvh, qkvh, gk, gk_t, qkvh, qkvh, qkvh, gk, gk_t)


def _gdn_out_kernel(of_ref, ob_ref, z_ref, ng_ref, w_ref, x_ref, gate_ref, o_ref, a_scr):
    @pl.when(pl.program_id(1) == 0)
    def _():
        ng = ng_ref[...]
        for h in range(GDN_V_HEADS):
            o = of_ref[h] + ob_ref[h]
            o = o * lax.rsqrt(jnp.mean(o * o, axis=-1, keepdims=True) + EPS)
            z = z_ref[:, h * GDN_HEAD:(h + 1) * GDN_HEAD]
            a_scr[:, h * GDN_HEAD:(h + 1) * GDN_HEAD] = (o * ng * _silu(z)).astype(BF16)

    o_ref[...] = x_ref[...] + gate_ref[...] * _dot(a_scr[...], w_ref[...])


def _gdn_out(o_f, o_b, proj, norm_g, w, x, mods, layer, dims):
    batch, lat_len, _, _ = dims
    tm = min(256, dims[3])
    ntok, d = x.shape
    tn = 512
    gate_blk = 2 * d // tn
    tiles_per_seq = lat_len // tm
    zblk = GDN_CONV_DIM // GDN_V
    return pl.pallas_call(
        _gdn_out_kernel,
        grid=(ntok // tm, d // tn),
        in_specs=[pl.BlockSpec((GDN_V_HEADS, tm, GDN_HEAD), lambda i, j: (0, i, 0)),
                  pl.BlockSpec((GDN_V_HEADS, tm, GDN_HEAD), lambda i, j: (0, i, 0)),
                  pl.BlockSpec((tm, GDN_V), lambda i, j: (i, zblk)),
                  pl.BlockSpec((1, GDN_HEAD), lambda i, j: (0, 0)),
                  pl.BlockSpec((GDN_V, tn), lambda i, j: (0, j)),
                  pl.BlockSpec((tm, tn), lambda i, j: (i, j)),
                  pl.BlockSpec((None, 1, tn), lambda i, j: (
                      layer * 8 + jnp.minimum(i // tiles_per_seq, batch), 0, gate_blk + j))],
        out_specs=pl.BlockSpec((tm, tn), lambda i, j: (i, j)),
        out_shape=jax.ShapeDtypeStruct((ntok, d), F32),
        scratch_shapes=[pltpu.VMEM((tm, GDN_V), BF16)],
        compiler_params=_params("parallel", "arbitrary"),
    )(o_f, o_b, proj, norm_g.reshape(1, GDN_HEAD), w, x, mods)


def _gdn_layer(x, mods, layer, j, dims, p):
    in_w = p['gdn_in_w'][j]
    g = p['norm_g'][layer, 0]
    proj = _mm_norm(x, mods, g, in_w[:, :GDN_MAIN].astype(BF16), jnp.zeros((GDN_MAIN,), F32), layer, dims, 512)
    ab = _mm_norm(x, mods, g, in_w[:, GDN_MAIN:].astype(BF16), jnp.zeros((128,), F32), layer, dims, 128)
    qkvh = _gdn_conv(proj, p['gdn_conv_w'][j], dims)
    gates = _gdn_gates(ab, p['gdn_a_log'][j], p['gdn_dt_bias'][j], dims[3])
    o_f, o_b = _gdn_chunks(qkvh, gates, dims)
    return _gdn_out(o_f, o_b, proj, p['gdn_norm_g'][j], p['gdn_out_w'][j].astype(BF16), x, mods, layer, dims)


def kernel(x, c, ctx, c_ctx, ada_w, ada_b, norm_g, final_g, ffn_w_gate, ffn_w_up, ffn_w_down, s5_a_re, s5_a_im, s5_log_dt, s5_b_re, s5_b_im, s5_c_re, s5_c_im, s5_d, s5_glu_w, s5_glu_b, hy_in_w, hy_in_b, hy_conv_w, hy_conv_b, hy_f_w1, hy_f_b1, hy_f_w2, hy_f_b2, hy_f_w3, hy_f_b3, hy_f_w4, hy_f_freq, hy_bias, hy_out_w, hy_out_b, gdn_in_w, gdn_conv_w, gdn_a_log, gdn_dt_bias, gdn_norm_g, gdn_out_w):
    p = dict(locals())
    batch, lat_len, d = x.shape
    ctx_len = ctx.shape[1]
    depth = ada_w.shape[0]
    tm = min(512, batch * ctx_len)
    dims = (batch, lat_len, ctx_len, tm)
    assert d == D_MODEL and lat_len % tm == 0 and (batch * ctx_len) % tm == 0 and batch + 1 <= 8

    cvec = jnp.zeros((8, d), F32).at[:batch].set(c).at[batch].set(c_ctx)
    mods = _ada_all(cvec, ada_w, ada_b).reshape(depth * 8, 1, 6 * d)
    tok = jnp.concatenate([x.reshape(batch * lat_len, d), ctx.reshape(batch * ctx_len, d)], axis=0)
    for i in range(depth):
        kind, j = i % N_MIXERS, i // N_MIXERS
        if kind == 0:
            tok = _s5_layer(tok, mods, i, j, dims, p)
        elif kind == 1:
            tok = _hyena_layer(tok, mods, i, j, dims, p)
        else:
            tok = _gdn_layer(tok, mods, i, j, dims, p)
        tok = _ffn(tok, mods, norm_g[i, 1], ffn_w_gate[i].astype(BF16), ffn_w_up[i].astype(BF16),
                   ffn_w_down[i].astype(BF16), i, dims)
    out = _final_norm(tok, final_g, batch * lat_len, tm)
    return out.reshape(batch, lat_len, d)
```

```python
import functools
import math

import jax
import jax.numpy as jnp
from jax import lax
from jax.experimental import pallas as pl
from jax.experimental.pallas import tpu as pltpu

F32 = jnp.float32
BF16 = jnp.bfloat16
HIGHEST = lax.Precision.HIGHEST

D_MODEL = 2048
GRID_W = 64
EPS = 1e-6
N_MIXERS = 3

S5_GROUP = 16
S5_STATE = 64
S5_GROUPS = D_MODEL // S5_GROUP
S5_CHUNK = 16
S5_ROW = S5_CHUNK * S5_GROUP
S5_SLAB_GROUPS = 128 // S5_GROUP
S5_SLAB = S5_SLAB_GROUPS * S5_ROW

HY_ORDER = 2
HY_BANDS = 16
HY_EMB = 1 + 2 * HY_BANDS
HY_EMB_PAD = 128
HY_FILTER_HIDDEN = 64
HY_DECAY_TARGET = 1e-2
HY_MAX_DECAY = math.log(HY_DECAY_TARGET) / 0.3
HY_MIN_DECAY = math.log(HY_DECAY_TARGET) / 1.5

GDN_K_HEADS = 16
GDN_V_HEADS = 32
GDN_HEAD = 128
GDN_QK = GDN_K_HEADS * GDN_HEAD
GDN_V = GDN_V_HEADS * GDN_HEAD
GDN_CONV_DIM = 2 * GDN_QK + GDN_V
GDN_MAIN = GDN_CONV_DIM + GDN_V
GDN_CHUNK = 64
GDN_KHEADS_PER_STEP = 2

VMEM_LIMIT_BYTES = 56 * 1024 * 1024


def _params(*sem):
    return pltpu.CompilerParams(dimension_semantics=sem, vmem_limit_bytes=VMEM_LIMIT_BYTES)


def _dot(a, b):
    return jnp.dot(a, b, preferred_element_type=F32)


def _dot_nt(a, b):
    return lax.dot_general(a, b, (((1,), (1,)), ((), ())), preferred_element_type=F32)


def _dot_tn(a, b):
    return lax.dot_general(a, b, (((0,), (0,)), ((), ())), preferred_element_type=F32)


def _sigmoid(x):
    return 1.0 / (1.0 + jnp.exp(-x))


def _silu(x):
    return x * _sigmoid(x)


def _norm_mod(x, g, sc, sh):
    y = x * lax.rsqrt(jnp.mean(x * x, axis=-1, keepdims=True) + EPS) * g
    return y * (1.0 + sc) + sh


def _mod_spec(layer, tm, lat_len, batch, width, col_block):
    tiles_per_seq = lat_len // tm

    def index(i, *_):
        return (layer * 8 + jnp.minimum(i // tiles_per_seq, batch), 0, col_block)

    return pl.BlockSpec((None, 1, width), index)


def _ada_kernel(c_ref, w_ref, b_ref, o_ref):
    c = c_ref[...]
    s = _silu(c).astype(BF16)
    o_ref[...] = _dot(s, w_ref[...].astype(BF16)) + b_ref[...]


def _ada_all(cvec, ada_w, ada_b):
    depth, d, n6 = ada_w.shape
    tn = 1024
    return pl.pallas_call(
        _ada_kernel,
        grid=(depth, n6 // tn),
        in_specs=[pl.BlockSpec((8, d), lambda l, j: (0, 0)),
                  pl.BlockSpec((None, d, tn), lambda l, j: (l, 0, j)),
                  pl.BlockSpec((None, 1, tn), lambda l, j: (l, 0, j))],
        out_specs=pl.BlockSpec((None, 8, tn), lambda l, j: (l, 0, j)),
        out_shape=jax.ShapeDtypeStruct((depth, 8, n6), F32),
        compiler_params=_params("parallel", "parallel"),
    )(cvec, ada_w, ada_b.reshape(depth, 1, n6))


def _prenorm_kernel(x_ref, mod_ref, g_ref, o_ref, obf_ref):
    d = D_MODEL
    h = _norm_mod(x_ref[...], g_ref[...], mod_ref[:, d:2 * d], mod_ref[:, 0:d])
    o_ref[...] = h
    obf_ref[...] = h.astype(BF16)


def _prenorm(x, mods, g, layer, dims):
    batch, lat_len, _, tm = dims
    ntok, d = x.shape
    return pl.pallas_call(
        _prenorm_kernel,
        grid=(ntok // tm,),
        in_specs=[pl.BlockSpec((tm, d), lambda i: (i, 0)),
                  _mod_spec(layer, tm, lat_len, batch, 6 * d, 0),
                  pl.BlockSpec((1, d), lambda i: (0, 0))],
        out_specs=[pl.BlockSpec((tm, d), lambda i: (i, 0)), pl.BlockSpec((tm, d), lambda i: (i, 0))],
        out_shape=[jax.ShapeDtypeStruct((ntok, d), F32), jax.ShapeDtypeStruct((ntok, d), BF16)],
        compiler_params=_params("parallel"),
    )(x, mods, g.reshape(1, d))


def _mm_norm_kernel(x_ref, mod_ref, g_ref, w_ref, b_ref, o_ref, h_scr):
    d = D_MODEL

    @pl.when(pl.program_id(1) == 0)
    def _():
        h_scr[...] = _norm_mod(x_ref[...], g_ref[...], mod_ref[:, d:2 * d], mod_ref[:, 0:d]).astype(BF16)

    o_ref[...] = _dot(h_scr[...], w_ref[...]) + b_ref[...]


def _mm_norm(x, mods, g, w, b, layer, dims, tn):
    batch, lat_len, _, tm = dims
    ntok, d = x.shape
    n = w.shape[1]
    return pl.pallas_call(
        _mm_norm_kernel,
        grid=(ntok // tm, n // tn),
        in_specs=[pl.BlockSpec((tm, d), lambda i, j: (i, 0)),
                  _mod_spec(layer, tm, lat_len, batch, 6 * d, 0),
                  pl.BlockSpec((1, d), lambda i, j: (0, 0)),
                  pl.BlockSpec((d, tn), lambda i, j: (0, j)),
                  pl.BlockSpec((1, tn), lambda i, j: (0, j))],
        out_specs=pl.BlockSpec((tm, tn), lambda i, j: (i, j)),
        out_shape=jax.ShapeDtypeStruct((ntok, n), F32),
        scratch_shapes=[pltpu.VMEM((tm, d), BF16)],
        compiler_params=_params("parallel", "arbitrary"),
    )(x, mods, g.reshape(1, d), w, b.reshape(1, n))


def _mm_res_kernel(a_ref, w_ref, b_ref, x_ref, gate_ref, o_ref):
    y = _dot(a_ref[...].astype(BF16), w_ref[...]) + b_ref[...]
    o_ref[...] = x_ref[...] + gate_ref[...] * y


def _mm_res(a, w, b, x, mods, layer, dims, tn):
    batch, lat_len, _, tm = dims
    ntok, d = x.shape
    k = a.shape[1]
    gate_blk = 2 * d // tn
    tiles_per_seq = lat_len // tm
    return pl.pallas_call(
        _mm_res_kernel,
        grid=(ntok // tm, d // tn),
        in_specs=[pl.BlockSpec((tm, k), lambda i, j: (i, 0)),
                  pl.BlockSpec((k, tn), lambda i, j: (0, j)),
                  pl.BlockSpec((1, tn), lambda i, j: (0, j)),
                  pl.BlockSpec((tm, tn), lambda i, j: (i, j)),
                  pl.BlockSpec((None, 1, tn), lambda i, j: (
                      layer * 8 + jnp.minimum(i // tiles_per_seq, batch), 0, gate_blk + j))],
        out_specs=pl.BlockSpec((tm, tn), lambda i, j: (i, j)),
        out_shape=jax.ShapeDtypeStruct((ntok, d), F32),
        compiler_params=_params("parallel", "parallel"),
    )(a, w, b.reshape(1, d), x, mods)


def _ffn_kernel(x_ref, mod_ref, g_ref, wg_ref, wu_ref, wd_ref, o_ref, h_scr, acc_scr):
    d = D_MODEL
    f = pl.program_id(1)

    @pl.when(f == 0)
    def _():
        h_scr[...] = _norm_mod(x_ref[...], g_ref[...], mod_ref[:, 4 * d:5 * d],
                               mod_ref[:, 3 * d:4 * d]).astype(BF16)
        acc_scr[...] = jnp.zeros_like(acc_scr)

    h = h_scr[...]
    gate = _dot(h, wg_ref[...])
    up = _dot(h, wu_ref[...])
    act = (_silu(gate) * up).astype(BF16)
    acc_scr[...] += _dot(act, wd_ref[...])

    @pl.when(f == pl.num_programs(1) - 1)
    def _():
        o_ref[...] = x_ref[...] + mod_ref[:, 5 * d:6 * d] * acc_scr[...]


def _ffn(x, mods, g, wg, wu, wd, layer, dims):
    batch, lat_len, _, tm = dims
    ntok, d = x.shape
    dff = wg.shape[1]
    tf = 512
    return pl.pallas_call(
        _ffn_kernel,
        grid=(ntok // tm, dff // tf),
        in_specs=[pl.BlockSpec((tm, d), lambda i, f: (i, 0)),
                  _mod_spec(layer, tm, lat_len, batch, 6 * d, 0),
                  pl.BlockSpec((1, d), lambda i, f: (0, 0)),
                  pl.BlockSpec((d, tf), lambda i, f: (0, f)),
                  pl.BlockSpec((d, tf), lambda i, f: (0, f)),
                  pl.BlockSpec((tf, d), lambda i, f: (f, 0))],
        out_specs=pl.BlockSpec((tm, d), lambda i, f: (i, 0)),
        out_shape=jax.ShapeDtypeStruct((ntok, d), F32),
        scratch_shapes=[pltpu.VMEM((tm, d), BF16), pltpu.VMEM((tm, d), F32)],
        compiler_params=_params("parallel", "arbitrary"),
    )(x, mods, g.reshape(1, d), wg, wu, wd)


def _final_norm_kernel(x_ref, g_ref, o_ref):
    x = x_ref[...]
    o_ref[...] = x * lax.rsqrt(jnp.mean(x * x, axis=-1, keepdims=True) + EPS) * g_ref[...]


def _final_norm(x, g, nrows, tm):
    d = x.shape[1]
    return pl.pallas_call(
        _final_norm_kernel,
        grid=(nrows // tm,),
        in_specs=[pl.BlockSpec((tm, d), lambda i: (i, 0)), pl.BlockSpec((1, d), lambda i: (0, 0))],
        out_specs=pl.BlockSpec((tm, d), lambda i: (i, 0)),
        out_shape=jax.ShapeDtypeStruct((nrows, d), F32),
        compiler_params=_params("parallel"),
    )(x, g.reshape(1, d))


def _s5_tables(a_re, a_im, log_dt, b_re, b_im, c_re, c_im):
    t_len = S5_CHUNK
    dt = jnp.exp(log_dt)[..., None]
    lr, li = a_re * dt, a_im * dt
    er = jnp.exp(lr)
    nr, ni = er * jnp.cos(li) - 1.0, er * jnp.sin(li)
    den = a_re * a_re + a_im * a_im
    qr, qi = (nr * a_re + ni * a_im) / den, (ni * a_re - nr * a_im) / den
    bbr = qr[..., None] * b_re - qi[..., None] * b_im
    bbi = qr[..., None] * b_im + qi[..., None] * b_re
    tau0 = t_len - 1
    taus = jnp.arange(-tau0, t_len + 1, dtype=F32)[:, None, None, None]
    mag = jnp.exp(lr[None] * taus)
    pr, pi = mag * jnp.cos(li[None] * taus), mag * jnp.sin(li[None] * taus)
    clr = c_re[None] * pr[:, :, :, None, :] - c_im[None] * pi[:, :, :, None, :]
    cli = c_re[None] * pi[:, :, :, None, :] + c_im[None] * pr[:, :, :, None, :]
    ti = jnp.arange(t_len)
    ngroups = a_re.shape[1]
    w_sr, w_si, w_or, w_oi, w_nr, w_ni = [], [], [], [], [], []
    for direction in range(2):
        st_pow = tau0 + ((t_len - 1 - ti) if direction == 0 else ti)
        sr = pr[st_pow, direction][..., None] * bbr[direction][None] - pi[st_pow, direction][..., None] * bbi[direction][None]
        si = pr[st_pow, direction][..., None] * bbi[direction][None] + pi[st_pow, direction][..., None] * bbr[direction][None]
        w_sr.append(sr.transpose(1, 0, 3, 2).reshape(ngroups, S5_ROW, S5_STATE))
        w_si.append(si.transpose(1, 0, 3, 2).reshape(ngroups, S5_ROW, S5_STATE))
        out_pow = tau0 + ((ti + 1) if direction == 0 else (t_len - ti))
        neg_pow = tau0 + ((ti - (t_len - 1)) if direction == 0 else -ti)
        relay = lambda t: t.transpose(1, 3, 0, 2).reshape(ngroups, S5_STATE, S5_ROW)
        w_or.append(relay(clr[out_pow, direction]))
        w_oi.append(relay(-cli[out_pow, direction]))
        w_nr.append(relay(clr[neg_pow, direction]))
        w_ni.append(relay(-cli[neg_pow, direction]))
    return (jnp.stack(w_sr), jnp.stack(w_si), jnp.stack(w_nr), jnp.stack(w_ni),
            jnp.stack(w_or).astype(BF16), jnp.stack(w_oi).astype(BF16),
            pr[tau0 + t_len], pi[tau0 + t_len])


def _s5_perm():
    src = jnp.arange(S5_SLAB, dtype=jnp.int32)
    t, g, c = src // 128, (src % 128) // S5_GROUP, src % S5_GROUP
    dst = g * S5_ROW + t * S5_GROUP + c
    return (dst[:, None] == jnp.arange(S5_SLAB, dtype=jnp.int32)[None, :]).astype(BF16)


def _s5_kernel(*refs, nparts, split_steps, batch, ncc, ncl, col_chunks):
    nin = S5_CHUNK if split_steps else nparts
    x_refs = refs[:nin]
    perm_ref, wsr_ref, wsi_ref, wnr_ref, wni_ref, wor_ref, woi_ref, lr_ref, li_ref = refs[nin:nin + 9]
    y_refs = refs[nin + 9:nin + 9 + nparts]
    sr_scr, si_scr, hr_scr, hi_scr, y_scr = refs[nin + 9 + nparts:]
    rows = y_scr.shape[0]
    groups = S5_SLAB_GROUPS

    def load(t):
        if split_steps:
            return x_refs[t][...]
        parts = [r[:, t].reshape(-1, 128) for r in x_refs]
        return (parts[0] if nparts == 1 else jnp.concatenate(parts, axis=0)).astype(BF16)

    xcat = jnp.concatenate([load(t) for t in range(S5_CHUNK)], axis=1)
    u_all = _dot(xcat, perm_ref[...]).astype(BF16)
    t_in = lax.broadcasted_iota(jnp.int32, (S5_ROW, S5_ROW), 0) // S5_GROUP
    t_out = lax.broadcasted_iota(jnp.int32, (S5_ROW, S5_ROW), 1) // S5_GROUP

    def hdot(a, b):
        a_hi, b_hi = a.astype(BF16), b.astype(BF16)
        a_lo, b_lo = (a - a_hi.astype(F32)).astype(BF16), (b - b_hi.astype(F32)).astype(BF16)
        return _dot(a_hi, b_hi) + (_dot(a_hi, b_lo) + _dot(a_lo, b_hi))

    def row_of(b, j, direction):
        if direction == 0:
            kc, kl = j, j - ncc
        else:
            kc, kl = ncc - 1 - j, ncl - 1 - (j - ncc)
        if col_chunks:
            kl = (kl % col_chunks) * GRID_W + kl // col_chunks
        return jnp.where(j < ncc, batch * ncl + b * ncc + kc, b * ncl + kl)

    for direction in range(2):
        for g in range(groups):
            u = u_all[:, g * S5_ROW:(g + 1) * S5_ROW]
            sr_scr[g * rows:(g + 1) * rows, :] = _dot(u, wsr_ref[direction, g].astype(BF16))
            si_scr[g * rows:(g + 1) * rows, :] = _dot(u, wsi_ref[direction, g].astype(BF16))
        lam_r = lr_ref[direction]
        lam_i = li_ref[direction]

        def step(j, carry, direction=direction, lam_r=lam_r, lam_i=lam_i):
            out = []
            for b in range(batch):
                h_r, h_i = carry[2 * b], carry[2 * b + 1]
                idx = pl.ds(row_of(b, j, direction), groups, stride=rows)
                hr_scr[idx, :] = h_r
                hi_scr[idx, :] = h_i
                s_r = sr_scr[idx, :]
                s_i = si_scr[idx, :]
                out += [lam_r * h_r - lam_i * h_i + s_r, lam_r * h_i + lam_i * h_r + s_i]
            return tuple(out)

        zero = jnp.zeros((groups, S5_STATE), F32)
        lax.fori_loop(0, ncc + ncl, step, (zero,) * (2 * batch))
        causal = (t_out >= t_in) if direction == 0 else (t_in >= t_out)
        for g in range(groups):
            u = u_all[:, g * S5_ROW:(g + 1) * S5_ROW]
            m_intra = jnp.where(causal, hdot(wsr_ref[direction, g], wnr_ref[direction, g])
                                + hdot(wsi_ref[direction, g], wni_ref[direction, g]), 0.0).astype(BF16)
            y = (_dot(u, m_intra)
                 + _dot(hr_scr[g * rows:(g + 1) * rows, :].astype(BF16), wor_ref[direction, g])
                 + _dot(hi_scr[g * rows:(g + 1) * rows, :].astype(BF16), woi_ref[direction, g]))
            if direction == 0:
                y_scr[:, g * S5_ROW:(g + 1) * S5_ROW] = y
            else:
                y_scr[:, g * S5_ROW:(g + 1) * S5_ROW] += y

    z = _dot_nt(y_scr[...].astype(BF16), perm_ref[...])
    for t in range(S5_CHUNK):
        zt = z[:, t * 128:(t + 1) * 128]
        r0 = 0
        for y_ref in y_refs:
            n = y_ref.shape[0] * (y_ref.shape[2] if len(y_ref.shape) == 4 else 1)
            y_ref[:, t] = zt[r0:r0 + n].reshape(y_ref.shape[:1] + y_ref.shape[2:])
            r0 += n


def _s5_core(h, hbf, tables, dims, col_major):
    batch, lat_len, ctx_len, _ = dims
    wsr, wsi, wnr, wni, wor, woi, lam_r, lam_i = tables
    d = D_MODEL
    ncc, ncl = ctx_len // S5_CHUNK, lat_len // S5_CHUNK
    rows = batch * (ncc + ncl)
    gs = S5_SLAB_GROUPS
    if col_major:
        grid_rows = lat_len // GRID_W
        assert grid_rows % S5_CHUNK == 0
        col_chunks = grid_rows // S5_CHUNK
        nlat = batch * lat_len
        xs = [h[:nlat].reshape(batch * col_chunks, S5_CHUNK, GRID_W, d),
              h[nlat:].reshape(batch * ncc, S5_CHUNK, d)]
        blocks = [pl.BlockSpec((batch * col_chunks, S5_CHUNK, GRID_W, 128), lambda q: (0, 0, 0, q)),
                  pl.BlockSpec((batch * ncc, S5_CHUNK, 128), lambda q: (0, 0, q))]
        ins, in_blocks = xs, blocks
    else:
        col_chunks = 0
        xs = [h.reshape(rows, S5_CHUNK, d)]
        blocks = [pl.BlockSpec((rows, S5_CHUNK, 128), lambda q: (0, 0, q))]
        ins = [hbf.reshape(rows, S5_CHUNK * d)] * S5_CHUNK
        in_blocks = [pl.BlockSpec((rows, 128), lambda q, t=t: (0, t * (d // 128) + q)) for t in range(S5_CHUNK)]
    kern = functools.partial(_s5_kernel, nparts=len(xs), split_steps=not col_major, batch=batch, ncc=ncc, ncl=ncl,
                             col_chunks=col_chunks)
    wspec = lambda a, b: pl.BlockSpec((2, gs, a, b), lambda q: (0, q, 0, 0))
    ys = pl.pallas_call(
        kern,
        grid=(S5_GROUPS // gs,),
        in_specs=in_blocks + [pl.BlockSpec((S5_SLAB, S5_SLAB), lambda q: (0, 0)),
                           wspec(S5_ROW, S5_STATE), wspec(S5_ROW, S5_STATE),
                           wspec(S5_STATE, S5_ROW), wspec(S5_STATE, S5_ROW),
                           wspec(S5_STATE, S5_ROW), wspec(S5_STATE, S5_ROW),
                           pl.BlockSpec((2, gs, S5_STATE), lambda q: (0, q, 0)),
                           pl.BlockSpec((2, gs, S5_STATE), lambda q: (0, q, 0))],
        out_specs=blocks,
        out_shape=[jax.ShapeDtypeStruct(x.shape, F32) for x in xs],
        scratch_shapes=[pltpu.VMEM((gs * rows, S5_STATE), F32) for _ in range(4)]
        + [pltpu.VMEM((rows, S5_SLAB), F32)],
        compiler_params=_params("parallel"),
    )(*ins, _s5_perm(), wsr, wsi, wnr, wni, wor, woi, lam_r, lam_i)
    if col_major:
        return jnp.concatenate([ys[0].reshape(-1, d), ys[1].reshape(-1, d)], axis=0)
    return ys[0].reshape(-1, d)


def _s5_glu_kernel(h_ref, y_ref, dskip_ref, w1_ref, w2_ref, b1_ref, b2_ref, x_ref, gate_ref, o_ref, z_scr):
    @pl.when(pl.program_id(1) == 0)
    def _():
        y = h_ref[...] * dskip_ref[...] + y_ref[...]
        z = 0.5 * y * (1.0 + jnp.tanh(math.sqrt(2.0 / math.pi) * (y + 0.044715 * (y * y * y))))
        z_scr[...] = z.astype(BF16)

    z = z_scr[...]
    lin = _dot(z, w1_ref[...]) + b1_ref[...]
    gat = _dot(z, w2_ref[...]) + b2_ref[...]
    o_ref[...] = x_ref[...] + gate_ref[...] * (lin * _sigmoid(gat))


def _s5_glu(h, y, dskip, w, b, x, mods, layer, dims):
    batch, lat_len, _, tm = dims
    ntok, d = x.shape
    tn = 512
    nblk = d // tn
    tiles_per_seq = lat_len // tm
    b = b.reshape(1, 2 * d)
    return pl.pallas_call(
        _s5_glu_kernel,
        grid=(ntok // tm, nblk),
        in_specs=[pl.BlockSpec((tm, d), lambda i, j: (i, 0)),
                  pl.BlockSpec((tm, d), lambda i, j: (i, 0)),
                  pl.BlockSpec((1, d), lambda i, j: (0, 0)),
                  pl.BlockSpec((d, tn), lambda i, j: (0, j)),
                  pl.BlockSpec((d, tn), lambda i, j: (0, nblk + j)),
                  pl.BlockSpec((1, tn), lambda i, j: (0, j)),
                  pl.BlockSpec((1, tn), lambda i, j: (0, nblk + j)),
                  pl.BlockSpec((tm, tn), lambda i, j: (i, j)),
                  pl.BlockSpec((None, 1, tn), lambda i, j: (
                      layer * 8 + jnp.minimum(i // tiles_per_seq, batch), 0, 2 * nblk + j))],
        out_specs=pl.BlockSpec((tm, tn), lambda i, j: (i, j)),
        out_shape=jax.ShapeDtypeStruct((ntok, d), F32),
        scratch_shapes=[pltpu.VMEM((tm, d), BF16)],
        compiler_params=_params("parallel", "arbitrary"),
    )(h, y, dskip.reshape(1, d), w, w, b, b, x, mods)


def _s5_layer(x, mods, layer, j, dims, p):
    batch, lat_len, ctx_len, _ = dims
    col_major = (j % 2) == 1
    h, hbf = _prenorm(x, mods, p['norm_g'][layer, 0], layer, dims)
    tables = _s5_tables(p['s5_a_re'][j], p['s5_a_im'][j], p['s5_log_dt'][j], p['s5_b_re'][j],
                        p['s5_b_im'][j], p['s5_c_re'][j], p['s5_c_im'][j])
    y = _s5_core(h, hbf, tables, dims, col_major)
    return _s5_glu(h, y, p['s5_d'][j], p['s5_glu_w'][j].astype(BF16), p['s5_glu_b'][j], x, mods, layer, dims)


def _seq_blocks(dims):
    batch, lat_len, ctx_len, _ = dims
    return [(lat_len, 0), (ctx_len, batch * lat_len // ctx_len)]


def _dwconv3_kernel(u_ref, w_ref, b_ref, o_ref, *maybe_bf16_ref):
    x = u_ref[...]
    n = x.shape[0]
    row = lax.broadcasted_iota(jnp.int32, x.shape, 0)
    prev = jnp.where(row == 0, 0.0, pltpu.roll(x, 1, 0))
    nxt = jnp.where(row == n - 1, 0.0, pltpu.roll(x, n - 1, 0))
    y = prev * w_ref[0:1, :] + x * w_ref[1:2, :] + nxt * w_ref[2:3, :] + b_ref[...]
    o_ref[...] = y
    for r in maybe_bf16_ref:
        r[...] = y.astype(BF16)


def _dwconv3(u, w, b, dims, col0, ncols, with_bf16):
    batch = dims[0]
    ntok = u.shape[0]
    tc = 256
    cb0 = col0 // tc
    w8 = jnp.zeros((8, w.shape[1]), F32).at[:3].set(w)
    outs = []
    for n, off in _seq_blocks(dims):
        out_shape = [jax.ShapeDtypeStruct((batch * n, ncols), F32)]
        out_specs = [pl.BlockSpec((n, tc), lambda s, j: (s, j))]
        if with_bf16:
            out_shape.append(jax.ShapeDtypeStruct((batch * n, ncols), BF16))
            out_specs.append(pl.BlockSpec((n, tc), lambda s, j: (s, j)))
        outs.append(pl.pallas_call(
            _dwconv3_kernel,
            grid=(batch, ncols // tc),
            in_specs=[pl.BlockSpec((n, tc), lambda s, j, off=off: (off + s, cb0 + j)),
                      pl.BlockSpec((8, tc), lambda s, j: (0, cb0 + j)),
                      pl.BlockSpec((1, tc), lambda s, j: (0, cb0 + j))],
            out_specs=out_specs,
            out_shape=out_shape,
            compiler_params=_params("parallel", "parallel"),
        )(u, w8, b.reshape(1, -1)))
    return outs


def _phase_table_kernel(ar_ref, ai_ref, br_ref, bi_ref, c_ref, s_ref):
    br, bi = br_ref[...], bi_ref[...]
    for r in range(ar_ref.shape[0]):
        ar, ai = ar_ref[r:r + 1, :], ai_ref[r:r + 1, :]
        c_ref[r * 16:(r + 1) * 16, :] = (ar * br - ai * bi).astype(BF16)
        s_ref[r * 16:(r + 1) * 16, :] = (ai * br + ar * bi).astype(BF16)


def _phase_tables(coarse, fine, period):
    def unit(m):
        ang = (m % period).astype(F32) * (2.0 * math.pi / period)
        return jnp.cos(ang), jnp.sin(ang)

    ar, ai = unit(coarse)
    br, bi = unit(fine)
    r1, ncol = coarse.shape
    rb = min(32, r1)
    out = jax.ShapeDtypeStruct((r1 * 16, ncol), BF16)
    return pl.pallas_call(
        _phase_table_kernel,
        grid=(r1 // rb,),
        in_specs=[pl.BlockSpec((rb, ncol), lambda i: (i, 0)), pl.BlockSpec((rb, ncol), lambda i: (i, 0)),
                  pl.BlockSpec((16, ncol), lambda i: (0, 0)), pl.BlockSpec((16, ncol), lambda i: (0, 0))],
        out_specs=[pl.BlockSpec((rb * 16, ncol), lambda i: (i, 0)), pl.BlockSpec((rb * 16, ncol), lambda i: (i, 0))],
        out_shape=[out, out],
        compiler_params=_params("parallel"),
    )(ar, ai, br, bi)


def _dft_tables(n):
    m = n // 2
    idx = jnp.arange(m, dtype=jnp.int32)[None, :]
    r1 = jnp.arange(m // 16, dtype=jnp.int32)[:, None]
    r0 = jnp.arange(16, dtype=jnp.int32)[:, None]
    fwd, inv = [], []
    for parity in range(2):
        tau = 2 * idx + parity
        fwd.append(_phase_tables(32 * r1 * tau, (2 * r0 + 1) * tau, 4 * n))
        odd = 2 * idx + 1
        inv.append(_phase_tables(32 * r1 * odd, (2 * r0 + parity) * odd, 4 * n))
    stack = lambda tabs, which: jnp.stack([t[which] for t in tabs])
    return stack(fwd, 0), stack(fwd, 1), stack(inv, 0), stack(inv, 1)


def _hy_filter_kernel(feat_ref, w1_ref, b1_ref, w2_ref, b2_ref, w3_ref, b3_ref, fq_ref, tu_ref, dl_ref,
                      w4_ref, hs_ref, hd_ref, hdn_scr, split_scr):
    @pl.when(pl.program_id(0) == 0)
    def _():
        fq = fq_ref[...]
        h = jnp.sin(fq * (jnp.dot(feat_ref[...], w1_ref[...], precision=HIGHEST, preferred_element_type=F32) + b1_ref[...]))
        h = jnp.sin(fq * (jnp.dot(h, w2_ref[...], precision=HIGHEST, preferred_element_type=F32) + b2_ref[...]))
        h = jnp.sin(fq * (jnp.dot(h, w3_ref[...], precision=HIGHEST, preferred_element_type=F32) + b3_ref[...]))
        hdn_scr[...] = h

    hdn = hdn_scr[...]
    decay = jnp.exp(-tu_ref[...] * dl_ref[...])
    row = lax.broadcasted_iota(jnp.int32, decay.shape, 0)
    m = decay.shape[0] // 2
    for order in range(HY_ORDER):
        fwd = jnp.dot(hdn, w4_ref[2 * order], precision=HIGHEST, preferred_element_type=F32) * decay
        bwd = jnp.dot(hdn, w4_ref[2 * order + 1], precision=HIGHEST, preferred_element_type=F32) * decay
        bwd = jnp.where(row == 0, 0.0, bwd)
        norm = jnp.sum(jnp.abs(fwd), axis=0, keepdims=True) + jnp.sum(jnp.abs(bwd), axis=0, keepdims=True)
        for out_ref, vals in ((hs_ref, (fwd + bwd) / norm), (hd_ref, (fwd - bwd) / norm)):
            split_scr[...] = vals
            for parity in range(2):
                out_ref[order, parity] = split_scr[pl.ds(parity, m, stride=2), :].astype(BF16)


def _hy_filters(n, p, j):
    d = D_MODEL
    t = jnp.arange(n, dtype=F32)
    t_unit = t / max(n - 1, 1)
    bands = jnp.linspace(1e-4, HY_BANDS - 1, HY_BANDS, dtype=F32)
    ang = (2.0 * math.pi / n) * t[:, None] * bands[None, :]
    feats = jnp.concatenate([t_unit[:, None], jnp.cos(ang), -jnp.sin(ang)], axis=-1)
    feats = jnp.pad(feats, ((0, 0), (0, HY_EMB_PAD - HY_EMB)))
    w1 = jnp.pad(p['hy_f_w1'][j], ((0, HY_EMB_PAD - HY_EMB), (0, 0)))
    deltas = jnp.abs(jnp.linspace(HY_MIN_DECAY, HY_MAX_DECAY, d, dtype=F32)).reshape(1, d)
    w4 = p['hy_f_w4'][j].reshape(HY_FILTER_HIDDEN, 2 * HY_ORDER, d).transpose(1, 0, 2)
    hid = HY_FILTER_HIDDEN
    tc = 128
    full = lambda shape: pl.BlockSpec(shape, lambda c: tuple(0 for _ in shape))
    row = lambda v: v.reshape(1, hid)
    return pl.pallas_call(
        _hy_filter_kernel,
        grid=(d // tc,),
        in_specs=[full((n, HY_EMB_PAD)), full((HY_EMB_PAD, hid)), full((1, hid)), full((hid, hid)), full((1, hid)),
                  full((hid, hid)), full((1, hid)), full((1, hid)), full((n, 1)),
                  pl.BlockSpec((1, tc), lambda c: (0, c)),
                  pl.BlockSpec((2 * HY_ORDER, hid, tc), lambda c: (0, 0, c))],
        out_specs=[pl.BlockSpec((HY_ORDER, 2, n // 2, tc), lambda c: (0, 0, 0, c)),
                   pl.BlockSpec((HY_ORDER, 2, n // 2, tc), lambda c: (0, 0, 0, c))],
        out_shape=[jax.ShapeDtypeStruct((HY_ORDER, 2, n // 2, d), BF16),
                   jax.ShapeDtypeStruct((HY_ORDER, 2, n // 2, d), BF16)],
        scratch_shapes=[pltpu.VMEM((n, hid), F32), pltpu.VMEM((n, tc), F32)],
        compiler_params=_params("arbitrary"),
    )(feats, w1, row(p['hy_f_b1'][j]), p['hy_f_w2'][j], row(p['hy_f_b2'][j]), p['hy_f_w3'][j],
      row(p['hy_f_b3'][j]), row(p['hy_f_freq'][j]), t_unit.reshape(n, 1), deltas, w4)


def _half_spectra(c_ref, s_ref, xe_c, xo_c, xe_s, xo_s):
    a, b = _dot(c_ref[0], xe_c), _dot(c_ref[1], xo_c)
    cs, ds = _dot(s_ref[0], xe_s), _dot(s_ref[1], xo_s)
    return a + b, a - b, cs + ds, ds - cs


def _dft_filter_kernel(c_ref, s_ref, hse_ref, hso_ref, hde_ref, hdo_ref, hr_ref, hi_ref):
    pc, qc, ps, qs = _half_spectra(c_ref, s_ref, hse_ref[...], hso_ref[...], hde_ref[...], hdo_ref[...])
    hr_ref[0], hr_ref[1] = pc, qc
    hi_ref[0], hi_ref[1] = -ps, -qs


def _dft_filter(cmat, smat, hs, hd, tk, td):
    order, _, m, d = hs.shape
    nd = d // td
    tab = pl.BlockSpec((2, tk, m), lambda i, o, j: (0, i, 0))
    even = pl.BlockSpec((None, None, m, td), lambda i, o, j: (o, 0, 0, j))
    odd = pl.BlockSpec((None, None, m, td), lambda i, o, j: (o, 1, 0, j))
    out = pl.BlockSpec((None, 2, tk, td), lambda i, o, j: (o, 0, i, j))
    sds = jax.ShapeDtypeStruct((order, 2, m, d), F32)
    return pl.pallas_call(
        _dft_filter_kernel,
        grid=(m // tk, order, nd),
        in_specs=[tab, tab, even, odd, even, odd],
        out_specs=[out, out],
        out_shape=[sds, sds],
        compiler_params=_params("parallel", "parallel", "parallel"),
    )(cmat, smat, hs, hs, hd, hd)


def _lane_blocks(nrows, width, row_block, col_block):
    ncol = width // 128
    return [pl.BlockSpec((nrows, 128), lambda *g, c=c: (row_block(*g), col_block(*g) * ncol + c))
            for c in range(ncol)]


def _parity_rows(refs, parity, count):
    cols = [r[pl.ds(parity, count, stride=2), :] for r in refs]
    return cols[0] if len(cols) == 1 else jnp.concatenate(cols, axis=1)


def _dft_fwd_kernel(c_ref, s_ref, *refs):
    z_refs = refs[:-4]
    hr_ref, hi_ref, ua_ref, ub_ref = refs[-4:]
    m = z_refs[0].shape[0] // 2
    ze = _parity_rows(z_refs, 0, m).astype(BF16)
    zo = _parity_rows(z_refs, 1, m).astype(BF16)
    pc, qc, ps, qs = _half_spectra(c_ref, s_ref, ze, zo, ze, zo)
    yrp = pc * hr_ref[0] + ps * hi_ref[0]
    yip = pc * hi_ref[0] - ps * hr_ref[0]
    yrq = qc * hr_ref[1] + qs * hi_ref[1]
    yiq = qc * hi_ref[1] - qs * hr_ref[1]
    ua_ref[0] = (yrp + yrq).astype(BF16)
    ub_ref[0] = (yip - yiq).astype(BF16)
    ua_ref[1] = (yrp - yrq).astype(BF16)
    ub_ref[1] = (yip + yiq).astype(BF16)


def _dft_fwd(cmat, smat, z, zcol, d, hr, hi, order, batch, tk, td):
    m = cmat.shape[1]
    tab = pl.BlockSpec((2, tk, m), lambda i, s, j: (0, i, 0), pipeline_mode=pl.Buffered(1))
    hspec = pl.BlockSpec((None, 2, tk, td), lambda i, s, j: (order, 0, i, j))
    out = pl.BlockSpec((2, None, tk, td), lambda i, s, j: (0, s, i, j))
    sds = jax.ShapeDtypeStruct((2, batch, m, d), BF16)
    return pl.pallas_call(
        _dft_fwd_kernel,
        grid=(m // tk, batch, d // td),
        in_specs=[tab, tab] + _lane_blocks(2 * m, td, lambda i, s, j: s, lambda i, s, j: zcol // td + j)
        + [hspec, hspec],
        out_specs=[out, out],
        out_shape=[sds, sds],
        compiler_params=_params("parallel", "parallel", "parallel"),
    )(cmat, smat, *([z] * (td // 128)), hr, hi)


def _dft_inv_kernel(ct_ref, st_ref, ua_ref, ub_ref, *refs, inv_n):
    ncol = (len(refs) - 2) // 3
    xg_refs, zp_refs, bias_ref, o_ref = refs[:ncol], refs[ncol:2 * ncol], refs[2 * ncol], refs[2 * ncol + 1]
    mix_scrs = refs[2 * ncol + 2:]
    ts = ct_ref.shape[1]
    bias = bias_ref[...]
    for parity in range(2):
        y = (_dot(ct_ref[parity], ua_ref[parity]) - _dot(st_ref[parity], ub_ref[parity])) * inv_n
        out = _parity_rows(xg_refs, parity, ts) * (y + _parity_rows(zp_refs, parity, ts) * bias)
        for c, scr in enumerate(mix_scrs):
            scr[pl.ds(parity, ts, stride=2), :] = out[:, c * 128:(c + 1) * 128]
    for c, scr in enumerate(mix_scrs):
        o_ref[:, c * 128:(c + 1) * 128] = scr[...]


def _dft_inv(ctm, stm, ua, ub, xg, xg_col0, zprev, zcol, bias, batch, ts, td):
    m = ctm.shape[1]
    d = ua.shape[3]
    ncol = td // 128
    tab = pl.BlockSpec((2, ts, m), lambda i, s, j: (0, i, 0))
    spec = pl.BlockSpec((2, None, m, td), lambda i, s, j: (0, s, 0, j))
    rblk = lambda i, s, j: s * (m // ts) + i
    rows = lambda col0: _lane_blocks(2 * ts, td, rblk, lambda i, s, j: col0 // td + j)
    kern = functools.partial(_dft_inv_kernel, inv_n=1.0 / (2 * m))
    return pl.pallas_call(
        kern,
        grid=(m // ts, batch, d // td),
        in_specs=[tab, tab, spec, spec] + rows(xg_col0) + rows(zcol) + [pl.BlockSpec((1, td), lambda i, s, j: (0, j))],
        out_specs=pl.BlockSpec((2 * ts, td), lambda i, s, j: (rblk(i, s, j), j)),
        out_shape=jax.ShapeDtypeStruct((zprev.shape[0], d), F32),
        scratch_shapes=[pltpu.VMEM((2 * ts, 128), F32) for _ in range(ncol)],
        compiler_params=_params("parallel", "parallel", "parallel"),
    )(ctm, stm, ua, ub, *([xg] * ncol), *([zprev] * ncol), bias.reshape(1, d))


def _hyena_layer(x, mods, layer, j, dims, p):
    batch, lat_len, ctx_len, _ = dims
    d = D_MODEL
    u = _mm_norm(x, mods, p['norm_g'][layer, 0], p['hy_in_w'][j].astype(BF16), p['hy_in_b'][j], layer, dims, 1024)
    parts = _dwconv3(u, p['hy_conv_w'][j], p['hy_conv_b'][j], dims, 0, 3 * d, False)
    z_out = []
    for (n, _), (conv,) in zip(_seq_blocks(dims), parts):
        m = n // 2
        cmat, smat, ctm, stm = _dft_tables(n)
        hs, hd = _hy_filters(n, p, j)
        hr, hi = _dft_filter(cmat, smat, hs, hd, min(512, m), 512)
        zprev, zcol = conv, 0
        for order in range(HY_ORDER):
            ua, ub = _dft_fwd(cmat, smat, zprev, zcol, d, hr, hi, order, batch, min(1024, m), 256)
            zprev = _dft_inv(ctm, stm, ua, ub, conv, (1 + order) * d, zprev, zcol, p['hy_bias'][j, order],
                             batch, min(512, m), 256)
            zcol = 0
        z_out.append(zprev)
    z = jnp.concatenate(z_out, axis=0)
    return _mm_res(z, p['hy_out_w'][j].astype(BF16), p['hy_out_b'][j], x, mods, layer, dims, 512)


def _gdn_conv_kernel(p_ref, w_ref, o_ref):
    x = p_ref[...]
    n = x.shape[0]
    row = lax.broadcasted_iota(jnp.int32, x.shape, 0)
    acc = x * w_ref[2:3, :]
    for s in (1, 2):
        prev = jnp.where(row < s, 0.0, pltpu.roll(x, s, 0))
        nxt = jnp.where(row >= n - s, 0.0, pltpu.roll(x, n - s, 0))
        acc = acc + prev * w_ref[2 - s:3 - s, :] + nxt * w_ref[2 + s:3 + s, :]
    y = _silu(acc)
    head = pl.program_id(1)
    inv = lax.rsqrt(jnp.sum(y * y, axis=-1, keepdims=True) + 1e-6)
    inv = inv * jnp.where(head < GDN_K_HEADS, GDN_HEAD ** -0.5, 1.0)
    o_ref[...] = y * jnp.where(head < 2 * GDN_K_HEADS, inv, 1.0)


def _gdn_conv(proj, w, dims):
    batch = dims[0]
    ntok = proj.shape[0]
    nheads = GDN_CONV_DIM // GDN_HEAD
    w8 = jnp.zeros((8, GDN_CONV_DIM), F32).at[:5].set(w)
    out = None
    for n, off in _seq_blocks(dims):
        args = [proj, w8]
        in_specs = [pl.BlockSpec((n, GDN_HEAD), lambda s, h, off=off: (off + s, h)),
                    pl.BlockSpec((8, GDN_HEAD), lambda s, h: (0, h))]
        aliases = {}
        if out is not None:
            args.append(out)
            in_specs.append(pl.BlockSpec(memory_space=pl.ANY))
            aliases = {2: 0}
        kern = _gdn_conv_kernel if out is None else (lambda p_ref, w_ref, _, o_ref: _gdn_conv_kernel(p_ref, w_ref, o_ref))
        out = pl.pallas_call(
            kern,
            grid=(batch, nheads),
            in_specs=in_specs,
            out_specs=pl.BlockSpec((None, n, GDN_HEAD), lambda s, h, off=off: (h, off + s, 0)),
            out_shape=jax.ShapeDtypeStruct((nheads, ntok, GDN_HEAD), F32),
            input_output_aliases=aliases,
            compiler_params=_params("parallel", "parallel"),
        )(*args)
    return out


def _gdn_gate_kernel(ab_ref, alog_ref, dtb_ref, o_ref):
    c = GDN_CHUNK
    row = lax.broadcasted_iota(jnp.int32, (c, c), 0)
    col = lax.broadcasted_iota(jnp.int32, (c, c), 1)
    lower = jnp.where(row >= col, 1.0, 0.0).astype(F32)
    upper = jnp.where(row <= col, 1.0, 0.0).astype(F32)
    lane = lax.broadcasted_iota(jnp.int32, (c, 128), 1)
    for r in range(ab_ref.shape[0] // c):
        ab = ab_ref[r * c:(r + 1) * c, :]
        xs = ab + dtb_ref[...]
        softplus = jnp.maximum(xs, 0.0) + jnp.log(1.0 + jnp.exp(-jnp.abs(xs)))
        g = -jnp.exp(alog_ref[...]) * softplus
        cum_f = jnp.dot(lower, g, precision=HIGHEST, preferred_element_type=F32)
        cum_b = jnp.dot(upper, g, precision=HIGHEST, preferred_element_type=F32)
        o_ref[r * c:(r + 1) * c, :] = jnp.where(lane < GDN_V_HEADS, cum_f,
                                                jnp.where(lane < 2 * GDN_V_HEADS, cum_b, _sigmoid(ab)))


def _gdn_gates(ab, a_log, dt_bias, tm):
    ntok = ab.shape[0]
    pad = jnp.zeros((2 * GDN_V_HEADS,), F32)
    alog = jnp.concatenate([a_log.reshape(-1), pad]).reshape(1, 128)
    dtb = jnp.concatenate([dt_bias.reshape(-1), pad]).reshape(1, 128)
    return pl.pallas_call(
        _gdn_gate_kernel,
        grid=(ntok // tm,),
        in_specs=[pl.BlockSpec((tm, 128), lambda i: (i, 0)),
                  pl.BlockSpec((1, 128), lambda i: (0, 0)),
                  pl.BlockSpec((1, 128), lambda i: (0, 0))],
        out_specs=pl.BlockSpec((tm, 128), lambda i: (i, 0)),
        out_shape=jax.ShapeDtypeStruct((ntok, 128), F32),
        compiler_params=_params("parallel"),
    )(ab, alog, dtb)


def _unit_triangular_inverses(mats):
    c = mats[0].shape[0]
    row = lax.broadcasted_iota(jnp.int32, (c, c), 0)
    col = lax.broadcasted_iota(jnp.int32, (c, c), 1)
    eye = jnp.where(row == col, 1.0, 0.0).astype(F32)
    ts = [eye - a for a in mats]
    pws = list(mats)
    for _ in range(int(math.log2(c)) - 1):
        pwbs = [pw.astype(BF16) for pw in pws]
        pws = [_dot(pwb, pwb) for pwb in pwbs]
        ts = [t + _dot(t.astype(BF16), pw.astype(BF16)) for t, pw in zip(ts, pws)]
    ms = [eye + a for a in mats]
    m_his = [m.astype(BF16) for m in ms]
    m_los = [(m - m_hi.astype(F32)).astype(BF16) for m, m_hi in zip(ms, m_his)]
    t_his = [t.astype(BF16) for t in ts]
    t_los = [(t - t_hi.astype(F32)).astype(BF16) for t, t_hi in zip(ts, t_his)]
    resids = [eye - (_dot(m_hi, t_hi) + (_dot(m_hi, t_lo) + _dot(m_lo, t_hi)))
              for m_hi, m_lo, t_hi, t_lo in zip(m_his, m_los, t_his, t_los)]
    return [t + _dot(t_hi, r.astype(BF16)) for t, t_hi, r in zip(ts, t_his, resids)]


def _gdn_chunk_kernel(qf_ref, kf_ref, vf_ref, gcf_ref, grf_ref, qb_ref, kb_ref, vb_ref, gcb_ref, grb_ref,
                      of_ref, ob_ref, state_scr, *, nchunks):
    c = GDN_CHUNK

    @pl.when(pl.program_id(2) == 0)
    def _():
        state_scr[...] = jnp.zeros_like(state_scr)

    row = lax.broadcasted_iota(jnp.int32, (c, c), 0)
    col = lax.broadcasted_iota(jnp.int32, (c, c), 1)
    blocks = ((qf_ref, kf_ref, vf_ref, gcf_ref, grf_ref, of_ref), (qb_ref, kb_ref, vb_ref, gcb_ref, grb_ref, ob_ref))
    nkh = qf_ref.shape[0]
    keys, amats, part = [], [], {}
    for direction, (q_ref, k_ref, v_ref, gc_ref, gr_ref, o_ref) in enumerate(blocks):
        if direction == 0:
            incl, strict, last = row >= col, row > col, c - 1
        else:
            incl, strict, last = row <= col, row < col, 0
        for kh in range(nkh):
            for ci in range(nchunks):
                rows = slice(ci * c, (ci + 1) * c)
                q = q_ref[kh, rows, :].astype(BF16)
                k32 = k_ref[kh, rows, :]
                k = k32.astype(BF16)
                k_t = k32.T
                gates_c = gc_ref[kh, ci]
                gates_r = gr_ref[kh, ci]
                kk = _dot_nt(k, k)
                qk = _dot_nt(q, k)
                for e in range(2):
                    ch_g, ch_b = direction * 2 + e, 4 + direction * 2 + e
                    gcc, gcr = gates_c[:, ch_g:ch_g + 1], gates_r[ch_g:ch_g + 1, :]
                    beta_c, beta_r = gates_c[:, ch_b:ch_b + 1], gates_r[ch_b:ch_b + 1, :]
                    decay = jnp.where(incl, jnp.exp(jnp.where(incl, gcc - gcr, 0.0)), 0.0)
                    key = (direction, kh, ci, e)
                    keys.append(key)
                    amats.append(jnp.where(strict, kk * beta_c * decay, 0.0))
                    g_last = gcr[:, last:last + 1]
                    part[key] = dict(
                        q=q, k=k, v=v_ref[2 * kh + e, rows, :].astype(BF16), attn=(qk * decay).astype(BF16),
                        beta_r=beta_r, wscale=beta_r * jnp.exp(gcr), egc=jnp.exp(gcc),
                        kg_t=(k_t * jnp.exp(g_last - gcr)).astype(BF16), e_last=jnp.exp(g_last))
    tmats = _unit_triangular_inverses(amats)
    local = {}
    for key, t in zip(keys, tmats):
        p = part[key]
        u = _dot((t * p['beta_r']).astype(BF16), p['v'])
        w = _dot((t * p['wscale']).astype(BF16), p['k']).astype(BF16)
        local[key] = (p['q'], u, w, p['attn'], p['egc'], p['kg_t'], p['e_last'])

    nslots = state_scr.shape[0]
    states = [state_scr[slot] for slot in range(nslots)]
    streams = [(direction, kh, e) for direction in range(2) for kh in range(nkh) for e in range(2)]
    out_refs = (of_ref, ob_ref)
    for step in range(nchunks):
        cur = [local[d, kh, (step if d == 0 else nchunks - 1 - step), e] for d, kh, e in streams]
        state_bs = [s.astype(BF16) for s in states]
        ws = [_dot(p[2], sb) for p, sb in zip(cur, state_bs)]
        qs = [_dot(p[0], sb) for p, sb in zip(cur, state_bs)]
        v_news = [(p[1] - w).astype(BF16) for p, w in zip(cur, ws)]
        intra = [_dot(p[3], vn) for p, vn in zip(cur, v_news)]
        upd = [_dot(p[5], vn) for p, vn in zip(cur, v_news)]
        for (d, kh, e), p, q_s, o_in in zip(streams, cur, qs, intra):
            ci = step if d == 0 else nchunks - 1 - step
            out_refs[d][2 * kh + e, ci * c:(ci + 1) * c, :] = p[4] * q_s + o_in
        states = [s * p[6] + dlt for s, p, dlt in zip(states, cur, upd)]
    for slot in range(nslots):
        state_scr[slot] = states[slot]


def _gdn_chunks(qkvh, gates, dims):
    batch, lat_len, ctx_len, _ = dims
    c = GDN_CHUNK
    ntok = qkvh.shape[1]
    per_step = min(4, ctx_len // c)
    rows = per_step * c
    nbc, nbl = ctx_len // rows, lat_len // rows
    ctx0 = batch * lat_len // rows
    gk = gates.reshape(ntok // c, c, 4, GDN_K_HEADS, 2).transpose(3, 0, 1, 2, 4).reshape(GDN_K_HEADS, ntok // c, c, 8)
    gk_t = gk.transpose(0, 1, 3, 2)

    def fwd_blk(b, j):
        return jnp.where(j < nbc, ctx0 + b * nbc + j, b * nbl + j - nbc)

    def bwd_blk(b, j):
        return jnp.where(j < nbc, ctx0 + b * nbc + nbc - 1 - j, b * nbl + nbl - 1 - (j - nbc))

    nkh = GDN_KHEADS_PER_STEP
    hblocks = GDN_K_HEADS // nkh

    def specs(blk):
        return [pl.BlockSpec((nkh, rows, GDN_HEAD), lambda b, h, j: (h, blk(b, j), 0)),
                pl.BlockSpec((nkh, rows, GDN_HEAD), lambda b, h, j: (hblocks + h, blk(b, j), 0)),
                pl.BlockSpec((2 * nkh, rows, GDN_HEAD), lambda b, h, j: (hblocks + h, blk(b, j), 0)),
                pl.BlockSpec((nkh, per_step, c, 8), lambda b, h, j: (h, blk(b, j), 0, 0)),
                pl.BlockSpec((nkh, per_step, 8, c), lambda b, h, j: (h, blk(b, j), 0, 0))]

    out_sd = jax.ShapeDtypeStruct((GDN_V_HEADS, ntok, GDN_HEAD), F32)
    return pl.pallas_call(
        functools.partial(_gdn_chunk_kernel, nchunks=per_step),
        grid=(batch, hblocks, nbc + nbl),
        in_specs=specs(fwd_blk) + specs(bwd_blk),
        out_specs=[pl.BlockSpec((2 * nkh, rows, GDN_HEAD), lambda b, h, j: (h, fwd_blk(b, j), 0)),
                   pl.BlockSpec((2 * nkh, rows, GDN_HEAD), lambda b, h, j: (h, bwd_blk(b, j), 0))],
        out_shape=[out_sd, out_sd],
        scratch_shapes=[pltpu.VMEM((4 * nkh, GDN_HEAD, GDN_HEAD), F32)],
        compiler_params=_params("parallel", "parallel", "arbitrary"),
    )(qkvh, qkvh, qkvh, gk, gk_t, qkvh, qkvh, qkvh, gk, gk_t)


def _gdn_out_kernel(of_ref, ob_ref, z_ref, ng_ref, w_ref, x_ref, gate_ref, o_ref, a_scr):
    @pl.when(pl.program_id(1) == 0)
    def _():
        ng = ng_ref[...]
        for h in range(GDN_V_HEADS):
            o = of_ref[h] + ob_ref[h]
            o = o * lax.rsqrt(jnp.mean(o * o, axis=-1, keepdims=True) + EPS)
            z = z_ref[:, h * GDN_HEAD:(h + 1) * GDN_HEAD]
            a_scr[:, h * GDN_HEAD:(h + 1) * GDN_HEAD] = (o * ng * _silu(z)).astype(BF16)

    o_ref[...] = x_ref[...] + gate_ref[...] * _dot(a_scr[...], w_ref[...])


def _gdn_out(o_f, o_b, proj, norm_g, w, x, mods, layer, dims):
    batch, lat_len, _, _ = dims
    tm = min(256, dims[3])
    ntok, d = x.shape
    tn = 512
    gate_blk = 2 * d // tn
    tiles_per_seq = lat_len // tm
    zblk = GDN_CONV_DIM // GDN_V
    return pl.pallas_call(
        _gdn_out_kernel,
        grid=(ntok // tm, d // tn),
        in_specs=[pl.BlockSpec((GDN_V_HEADS, tm, GDN_HEAD), lambda i, j: (0, i, 0)),
                  pl.BlockSpec((GDN_V_HEADS, tm, GDN_HEAD), lambda i, j: (0, i, 0)),
                  pl.BlockSpec((tm, GDN_V), lambda i, j: (i, zblk)),
                  pl.BlockSpec((1, GDN_HEAD), lambda i, j: (0, 0)),
                  pl.BlockSpec((GDN_V, tn), lambda i, j: (0, j)),
                  pl.BlockSpec((tm, tn), lambda i, j: (i, j)),
                  pl.BlockSpec((None, 1, tn), lambda i, j: (
                      layer * 8 + jnp.minimum(i // tiles_per_seq, batch), 0, gate_blk + j))],
        out_specs=pl.BlockSpec((tm, tn), lambda i, j: (i, j)),
        out_shape=jax.ShapeDtypeStruct((ntok, d), F32),
        scratch_shapes=[pltpu.VMEM((tm, GDN_V), BF16)],
        compiler_params=_params("parallel", "arbitrary"),
    )(o_f, o_b, proj, norm_g.reshape(1, GDN_HEAD), w, x, mods)


def _gdn_layer(x, mods, layer, j, dims, p):
    in_w = p['gdn_in_w'][j]
    g = p['norm_g'][layer, 0]
    proj = _mm_norm(x, mods, g, in_w[:, :GDN_MAIN].astype(BF16), jnp.zeros((GDN_MAIN,), F32), layer, dims, 1024)
    ab = _mm_norm(x, mods, g, in_w[:, GDN_MAIN:].astype(BF16), jnp.zeros((128,), F32), layer, dims, 128)
    qkvh = _gdn_conv(proj, p['gdn_conv_w'][j], dims)
    gates = _gdn_gates(ab, p['gdn_a_log'][j], p['gdn_dt_bias'][j], dims[3])
    o_f, o_b = _gdn_chunks(qkvh, gates, dims)
    return _gdn_out(o_f, o_b, proj, p['gdn_norm_g'][j], p['gdn_out_w'][j].astype(BF16), x, mods, layer, dims)


def kernel(x, c, ctx, c_ctx, ada_w, ada_b, norm_g, final_g, ffn_w_gate, ffn_w_up, ffn_w_down, s5_a_re, s5_a_im, s5_log_dt, s5_b_re, s5_b_im, s5_c_re, s5_c_im, s5_d, s5_glu_w, s5_glu_b, hy_in_w, hy_in_b, hy_conv_w, hy_conv_b, hy_f_w1, hy_f_b1, hy_f_w2, hy_f_b2, hy_f_w3, hy_f_b3, hy_f_w4, hy_f_freq, hy_bias, hy_out_w, hy_out_b, gdn_in_w, gdn_conv_w, gdn_a_log, gdn_dt_bias, gdn_norm_g, gdn_out_w):
    p = dict(locals())
    batch, lat_len, d = x.shape
    ctx_len = ctx.shape[1]
    depth = ada_w.shape[0]
    tm = min(512, batch * ctx_len)
    dims = (batch, lat_len, ctx_len, tm)
    assert d == D_MODEL and lat_len % tm == 0 and (batch * ctx_len) % tm == 0 and batch + 1 <= 8

    cvec = jnp.zeros((8, d), F32).at[:batch].set(c).at[batch].set(c_ctx)
    mods = _ada_all(cvec, ada_w, ada_b).reshape(depth * 8, 1, 6 * d)
    tok = jnp.concatenate([x.reshape(batch * lat_len, d), ctx.reshape(batch * ctx_len, d)], axis=0)
    for i in range(depth):
        kind, j = i % N_MIXERS, i // N_MIXERS
        if kind == 0:
            tok = _s5_layer(tok, mods, i, j, dims, p)
        elif kind == 1:
            tok = _hyena_layer(tok, mods, i, j, dims, p)
        else:
            tok = _gdn_layer(tok, mods, i, j, dims, p)
        tok = _ffn(tok, mods, norm_g[i, 1], ffn_w_gate[i].astype(BF16), ffn_w_up[i].astype(BF16),
                   ffn_w_down[i].astype(BF16), i, dims)
    out = _final_norm(tok, final_g, batch * lat_len, tm)
    return out.reshape(batch, lat_len, d)
```

```python
import functools
import math

import jax
import jax.numpy as jnp
from jax import lax
from jax.experimental import pallas as pl
from jax.experimental.pallas import tpu as pltpu

F32 = jnp.float32
BF16 = jnp.bfloat16
HIGHEST = lax.Precision.HIGHEST

D_MODEL = 2048
GRID_W = 64
EPS = 1e-6
N_MIXERS = 3

S5_GROUP = 16
S5_STATE = 64
S5_GROUPS = D_MODEL // S5_GROUP
S5_CHUNK = 16
S5_ROW = S5_CHUNK * S5_GROUP
S5_SLAB_GROUPS = 128 // S5_GROUP
S5_SLAB = S5_SLAB_GROUPS * S5_ROW

HY_ORDER = 2
HY_BANDS = 16
HY_EMB = 1 + 2 * HY_BANDS
HY_EMB_PAD = 128
HY_FILTER_HIDDEN = 64
HY_DECAY_TARGET = 1e-2
HY_MAX_DECAY = math.log(HY_DECAY_TARGET) / 0.3
HY_MIN_DECAY = math.log(HY_DECAY_TARGET) / 1.5

GDN_K_HEADS = 16
GDN_V_HEADS = 32
GDN_HEAD = 128
GDN_QK = GDN_K_HEADS * GDN_HEAD
GDN_V = GDN_V_HEADS * GDN_HEAD
GDN_CONV_DIM = 2 * GDN_QK + GDN_V
GDN_MAIN = GDN_CONV_DIM + GDN_V
GDN_CHUNK = 64
GDN_KHEADS_PER_STEP = 2

VMEM_LIMIT_BYTES = 56 * 1024 * 1024


def _params(*sem):
    return pltpu.CompilerParams(dimension_semantics=sem, vmem_limit_bytes=VMEM_LIMIT_BYTES)


def _dot(a, b):
    return jnp.dot(a, b, preferred_element_type=F32)


def _dot_nt(a, b):
    return lax.dot_general(a, b, (((1,), (1,)), ((), ())), preferred_element_type=F32)


def _dot_tn(a, b):
    return lax.dot_general(a, b, (((0,), (0,)), ((), ())), preferred_element_type=F32)


def _sigmoid(x):
    return 1.0 / (1.0 + jnp.exp(-x))


def _silu(x):
    return x * _sigmoid(x)


def _norm_mod(x, g, sc, sh):
    y = x * lax.rsqrt(jnp.mean(x * x, axis=-1, keepdims=True) + EPS) * g
    return y * (1.0 + sc) + sh


def _mod_spec(layer, tm, lat_len, batch, width, col_block):
    tiles_per_seq = lat_len // tm

    def index(i, *_):
        return (layer * 8 + jnp.minimum(i // tiles_per_seq, batch), 0, col_block)

    return pl.BlockSpec((None, 1, width), index)


def _ada_kernel(c_ref, w_ref, b_ref, o_ref):
    c = c_ref[...]
    s = _silu(c).astype(BF16)
    o_ref[...] = _dot(s, w_ref[...].astype(BF16)) + b_ref[...]


def _ada_all(cvec, ada_w, ada_b):
    depth, d, n6 = ada_w.shape
    tn = 1024
    return pl.pallas_call(
        _ada_kernel,
        grid=(depth, n6 // tn),
        in_specs=[pl.BlockSpec((8, d), lambda l, j: (0, 0)),
                  pl.BlockSpec((None, d, tn), lambda l, j: (l, 0, j)),
                  pl.BlockSpec((None, 1, tn), lambda l, j: (l, 0, j))],
        out_specs=pl.BlockSpec((None, 8, tn), lambda l, j: (l, 0, j)),
        out_shape=jax.ShapeDtypeStruct((depth, 8, n6), F32),
        compiler_params=_params("parallel", "parallel"),
    )(cvec, ada_w, ada_b.reshape(depth, 1, n6))


def _prenorm_kernel(x_ref, mod_ref, g_ref, o_ref, obf_ref):
    d = D_MODEL
    h = _norm_mod(x_ref[...], g_ref[...], mod_ref[:, d:2 * d], mod_ref[:, 0:d])
    o_ref[...] = h
    obf_ref[...] = h.astype(BF16)


def _prenorm(x, mods, g, layer, dims):
    batch, lat_len, _, tm = dims
    ntok, d = x.shape
    return pl.pallas_call(
        _prenorm_kernel,
        grid=(ntok // tm,),
        in_specs=[pl.BlockSpec((tm, d), lambda i: (i, 0)),
                  _mod_spec(layer, tm, lat_len, batch, 6 * d, 0),
                  pl.BlockSpec((1, d), lambda i: (0, 0))],
        out_specs=[pl.BlockSpec((tm, d), lambda i: (i, 0)), pl.BlockSpec((tm, d), lambda i: (i, 0))],
        out_shape=[jax.ShapeDtypeStruct((ntok, d), F32), jax.ShapeDtypeStruct((ntok, d), BF16)],
        compiler_params=_params("parallel"),
    )(x, mods, g.reshape(1, d))


def _mm_norm_kernel(x_ref, mod_ref, g_ref, w_ref, b_ref, *rest):
    o_ref, h_scr = rest[-2:]
    d = D_MODEL

    @pl.when(pl.program_id(1) == 0)
    def _():
        h_scr[...] = _norm_mod(x_ref[...], g_ref[...], mod_ref[:, d:2 * d], mod_ref[:, 0:d]).astype(BF16)

    o_ref[...] = _dot(h_scr[...], w_ref[...].astype(BF16)) + b_ref[...]


def _mm_norm(x, mods, g, w_all, widx, col0, n, b, layer, dims, tn):
    batch, lat_len, ctx_len, _ = dims
    ntok, d = x.shape
    out = None
    for tm, row0, nrows in ((min(1024, lat_len), 0, batch * lat_len),
                            (batch * ctx_len, batch * lat_len, batch * ctx_len)):
        blk0 = row0 // tm
        tiles_per_seq = lat_len // tm
        mod_row = (lambda i: i // tiles_per_seq) if row0 == 0 else (lambda i: batch)
        args = [x, mods, g.reshape(1, d), w_all, b.reshape(1, n)]
        in_specs = [pl.BlockSpec((tm, d), lambda i, j: (blk0 + i, 0), pipeline_mode=pl.Buffered(1)),
                    pl.BlockSpec((None, 1, 6 * d), lambda i, j: (layer * 8 + mod_row(i), 0, 0)),
                    pl.BlockSpec((1, d), lambda i, j: (0, 0)),
                    pl.BlockSpec((None, d, tn), lambda i, j: (widx, 0, col0 // tn + j)),
                    pl.BlockSpec((1, tn), lambda i, j: (0, j))]
        aliases = {}
        if out is not None:
            args.append(out)
            in_specs.append(pl.BlockSpec(memory_space=pl.ANY))
            aliases = {5: 0}
        out = pl.pallas_call(
            _mm_norm_kernel,
            grid=(nrows // tm, n // tn),
            in_specs=in_specs,
            out_specs=pl.BlockSpec((tm, tn), lambda i, j: (blk0 + i, j)),
            out_shape=jax.ShapeDtypeStruct((ntok, n), F32),
            scratch_shapes=[pltpu.VMEM((tm, d), BF16)],
            input_output_aliases=aliases,
            compiler_params=_params("parallel", "arbitrary"),
        )(*args)
    return out


def _mm_res_kernel(a_ref, w_ref, b_ref, x_ref, gate_ref, o_ref):
    y = _dot(a_ref[...].astype(BF16), w_ref[...]) + b_ref[...]
    o_ref[...] = x_ref[...] + gate_ref[...] * y


def _mm_res(a, w, b, x, mods, layer, dims, tn):
    batch, lat_len, _, tm = dims
    ntok, d = x.shape
    k = a.shape[1]
    gate_blk = 2 * d // tn
    tiles_per_seq = lat_len // tm
    return pl.pallas_call(
        _mm_res_kernel,
        grid=(ntok // tm, d // tn),
        in_specs=[pl.BlockSpec((tm, k), lambda i, j: (i, 0)),
                  pl.BlockSpec((k, tn), lambda i, j: (0, j)),
                  pl.BlockSpec((1, tn), lambda i, j: (0, j)),
                  pl.BlockSpec((tm, tn), lambda i, j: (i, j)),
                  pl.BlockSpec((None, 1, tn), lambda i, j: (
                      layer * 8 + jnp.minimum(i // tiles_per_seq, batch), 0, gate_blk + j))],
        out_specs=pl.BlockSpec((tm, tn), lambda i, j: (i, j)),
        out_shape=jax.ShapeDtypeStruct((ntok, d), F32),
        compiler_params=_params("parallel", "parallel"),
    )(a, w, b.reshape(1, d), x, mods)


def _ffn_kernel(x_ref, mod_ref, g_ref, wg_ref, wu_ref, wd_ref, *rest):
    o_ref, h_scr = rest[-2:]
    d = D_MODEL
    f = pl.program_id(1)

    @pl.when(f == 0)
    def _():
        h_scr[...] = _norm_mod(x_ref[...], g_ref[...], mod_ref[:, 4 * d:5 * d],
                               mod_ref[:, 3 * d:4 * d]).astype(BF16)

    h = h_scr[...]
    gate = _dot(h, wg_ref[...].astype(BF16))
    up = _dot(h, wu_ref[...].astype(BF16))
    act = (_silu(gate) * up).astype(BF16)
    down = _dot(act, wd_ref[...].astype(BF16))

    @pl.when(f == 0)
    def _():
        o_ref[...] = down

    @pl.when(f > 0)
    def _():
        o_ref[...] += down

    @pl.when(f == pl.num_programs(1) - 1)
    def _():
        o_ref[...] = x_ref[...] + mod_ref[:, 5 * d:6 * d] * o_ref[...]


def _ffn(x, mods, g, wg, wu, wd, layer, dims, with_ctx):
    batch, lat_len, ctx_len, _ = dims
    ntok, d = x.shape
    dff = wg.shape[2]
    tf = 256
    out = None
    calls = [(min(1024, lat_len), 0, batch * lat_len)]
    if with_ctx:
        calls.append((batch * ctx_len, batch * lat_len, batch * ctx_len))
    for tm, row0, nrows in calls:
        blk0 = row0 // tm
        tiles_per_seq = lat_len // tm
        mod_row = (lambda i: i // tiles_per_seq) if row0 == 0 else (lambda i: batch)
        args = [x, mods, g.reshape(1, d), wg, wu, wd]
        in_specs = [pl.BlockSpec((tm, d), lambda i, f: (blk0 + i, 0), pipeline_mode=pl.Buffered(1)),
                    pl.BlockSpec((None, 1, 6 * d), lambda i, f: (layer * 8 + mod_row(i), 0, 0)),
                    pl.BlockSpec((1, d), lambda i, f: (0, 0)),
                    pl.BlockSpec((None, d, tf), lambda i, f: (layer, 0, f)),
                    pl.BlockSpec((None, d, tf), lambda i, f: (layer, 0, f)),
                    pl.BlockSpec((None, tf, d), lambda i, f: (layer, f, 0))]
        aliases = {}
        if out is not None:
            args.append(out)
            in_specs.append(pl.BlockSpec(memory_space=pl.ANY))
            aliases = {6: 0}
        out = pl.pallas_call(
            _ffn_kernel,
            grid=(nrows // tm, dff // tf),
            in_specs=in_specs,
            out_specs=pl.BlockSpec((tm, d), lambda i, f: (blk0 + i, 0)),
            out_shape=jax.ShapeDtypeStruct((ntok, d), F32),
            scratch_shapes=[pltpu.VMEM((tm, d), BF16)],
            input_output_aliases=aliases,
            compiler_params=_params("parallel", "arbitrary"),
        )(*args)
    return out


def _final_norm_kernel(x_ref, g_ref, o_ref):
    x = x_ref[...]
    o_ref[...] = x * lax.rsqrt(jnp.mean(x * x, axis=-1, keepdims=True) + EPS) * g_ref[...]


def _final_norm(x, g, nrows, tm):
    d = x.shape[1]
    return pl.pallas_call(
        _final_norm_kernel,
        grid=(nrows // tm,),
        in_specs=[pl.BlockSpec((tm, d), lambda i: (i, 0)), pl.BlockSpec((1, d), lambda i: (0, 0))],
        out_specs=pl.BlockSpec((tm, d), lambda i: (i, 0)),
        out_shape=jax.ShapeDtypeStruct((nrows, d), F32),
        compiler_params=_params("parallel"),
    )(x, g.reshape(1, d))


def _s5_tables(a_re, a_im, log_dt, b_re, b_im, c_re, c_im):
    t_len = S5_CHUNK
    dt = jnp.exp(log_dt)[..., None]
    lr, li = a_re * dt, a_im * dt
    er = jnp.exp(lr)
    nr, ni = er * jnp.cos(li) - 1.0, er * jnp.sin(li)
    den = a_re * a_re + a_im * a_im
    qr, qi = (nr * a_re + ni * a_im) / den, (ni * a_re - nr * a_im) / den
    bbr = qr[..., None] * b_re - qi[..., None] * b_im
    bbi = qr[..., None] * b_im + qi[..., None] * b_re
    tau0 = t_len - 1
    taus = jnp.arange(-tau0, t_len + 1, dtype=F32)[:, None, None, None]
    mag = jnp.exp(lr[None] * taus)
    pr, pi = mag * jnp.cos(li[None] * taus), mag * jnp.sin(li[None] * taus)
    clr = c_re[None] * pr[:, :, :, None, :] - c_im[None] * pi[:, :, :, None, :]
    cli = c_re[None] * pi[:, :, :, None, :] + c_im[None] * pr[:, :, :, None, :]
    ti = jnp.arange(t_len)
    ngroups = a_re.shape[1]
    w_sr, w_si, w_or, w_oi, w_nr, w_ni = [], [], [], [], [], []
    for direction in range(2):
        st_pow = tau0 + ((t_len - 1 - ti) if direction == 0 else ti)
        sr = pr[st_pow, direction][..., None] * bbr[direction][None] - pi[st_pow, direction][..., None] * bbi[direction][None]
        si = pr[st_pow, direction][..., None] * bbi[direction][None] + pi[st_pow, direction][..., None] * bbr[direction][None]
        w_sr.append(sr.transpose(1, 0, 3, 2).reshape(ngroups, S5_ROW, S5_STATE))
        w_si.append(si.transpose(1, 0, 3, 2).reshape(ngroups, S5_ROW, S5_STATE))
        out_pow = tau0 + ((ti + 1) if direction == 0 else (t_len - ti))
        neg_pow = tau0 + ((ti - (t_len - 1)) if direction == 0 else -ti)
        relay = lambda t: t.transpose(1, 3, 0, 2).reshape(ngroups, S5_STATE, S5_ROW)
        w_or.append(relay(clr[out_pow, direction]))
        w_oi.append(relay(-cli[out_pow, direction]))
        w_nr.append(relay(clr[neg_pow, direction]))
        w_ni.append(relay(-cli[neg_pow, direction]))
    return (jnp.stack(w_sr), jnp.stack(w_si), jnp.stack(w_nr), jnp.stack(w_ni),
            jnp.stack(w_or).astype(BF16), jnp.stack(w_oi).astype(BF16),
            pr[tau0 + t_len], pi[tau0 + t_len])


def _s5_perm():
    src = jnp.arange(S5_SLAB, dtype=jnp.int32)
    t, g, c = src // 128, (src % 128) // S5_GROUP, src % S5_GROUP
    dst = g * S5_ROW + t * S5_GROUP + c
    return (dst[:, None] == jnp.arange(S5_SLAB, dtype=jnp.int32)[None, :]).astype(BF16)


def _s5_kernel(*refs, nparts, split_steps, batch, ncc, ncl, col_chunks):
    nin = S5_CHUNK if split_steps else nparts
    x_refs = refs[:nin]
    perm_ref, wsr_ref, wsi_ref, wnr_ref, wni_ref, wor_ref, woi_ref, lr_ref, li_ref = refs[nin:nin + 9]
    y_refs = refs[nin + 9:nin + 9 + nparts]
    sr_scr, si_scr, hr_scr, hi_scr, y_scr = refs[nin + 9 + nparts:]
    rows = y_scr.shape[0]
    groups = S5_SLAB_GROUPS

    def load(t):
        if split_steps:
            return x_refs[t][...]
        parts = [r[:, t].reshape(-1, 128) for r in x_refs]
        return (parts[0] if nparts == 1 else jnp.concatenate(parts, axis=0)).astype(BF16)

    xcat = jnp.concatenate([load(t) for t in range(S5_CHUNK)], axis=1)
    u_all = _dot(xcat, perm_ref[...]).astype(BF16)
    t_in = lax.broadcasted_iota(jnp.int32, (S5_ROW, S5_ROW), 0) // S5_GROUP
    t_out = lax.broadcasted_iota(jnp.int32, (S5_ROW, S5_ROW), 1) // S5_GROUP

    def hdot(a, b):
        a_hi, b_hi = a.astype(BF16), b.astype(BF16)
        a_lo, b_lo = (a - a_hi.astype(F32)).astype(BF16), (b - b_hi.astype(F32)).astype(BF16)
        return _dot(a_hi, b_hi) + (_dot(a_hi, b_lo) + _dot(a_lo, b_hi))

    def row_of(b, j, direction):
        if direction == 0:
            kc, kl = j, j - ncc
        else:
            kc, kl = ncc - 1 - j, ncl - 1 - (j - ncc)
        if col_chunks:
            kl = (kl % col_chunks) * GRID_W + kl // col_chunks
        return jnp.where(j < ncc, batch * ncl + b * ncc + kc, b * ncl + kl)

    for direction in range(2):
        for g in range(groups):
            u = u_all[:, g * S5_ROW:(g + 1) * S5_ROW]
            sr_scr[g * rows:(g + 1) * rows, :] = _dot(u, wsr_ref[direction, g].astype(BF16))
            si_scr[g * rows:(g + 1) * rows, :] = _dot(u, wsi_ref[direction, g].astype(BF16))
        lam_r = lr_ref[direction]
        lam_i = li_ref[direction]

        def step(j, carry, direction=direction, lam_r=lam_r, lam_i=lam_i):
            out = []
            for b in range(batch):
                h_r, h_i = carry[2 * b], carry[2 * b + 1]
                idx = pl.ds(row_of(b, j, direction), groups, stride=rows)
                hr_scr[idx, :] = h_r
                hi_scr[idx, :] = h_i
                s_r = sr_scr[idx, :]
                s_i = si_scr[idx, :]
                out += [lam_r * h_r - lam_i * h_i + s_r, lam_r * h_i + lam_i * h_r + s_i]
            return tuple(out)

        zero = jnp.zeros((groups, S5_STATE), F32)
        lax.fori_loop(0, ncc + ncl, step, (zero,) * (2 * batch))
        causal = (t_out >= t_in) if direction == 0 else (t_in >= t_out)
        for g in range(groups):
            u = u_all[:, g * S5_ROW:(g + 1) * S5_ROW]
            m_intra = jnp.where(causal, hdot(wsr_ref[direction, g], wnr_ref[direction, g])
                                + hdot(wsi_ref[direction, g], wni_ref[direction, g]), 0.0).astype(BF16)
            y = (_dot(u, m_intra)
                 + _dot(hr_scr[g * rows:(g + 1) * rows, :].astype(BF16), wor_ref[direction, g])
                 + _dot(hi_scr[g * rows:(g + 1) * rows, :].astype(BF16), woi_ref[direction, g]))
            if direction == 0:
                y_scr[:, g * S5_ROW:(g + 1) * S5_ROW] = y
            else:
                y_scr[:, g * S5_ROW:(g + 1) * S5_ROW] += y

    z = _dot_nt(y_scr[...].astype(BF16), perm_ref[...])
    for t in range(S5_CHUNK):
        zt = z[:, t * 128:(t + 1) * 128]
        r0 = 0
        for y_ref in y_refs:
            n = y_ref.shape[0] * (y_ref.shape[2] if len(y_ref.shape) == 4 else 1)
            y_ref[:, t] = zt[r0:r0 + n].reshape(y_ref.shape[:1] + y_ref.shape[2:])
            r0 += n


def _s5_core(h, hbf, tables, dims, col_major):
    batch, lat_len, ctx_len, _ = dims
    wsr, wsi, wnr, wni, wor, woi, lam_r, lam_i = tables
    d = D_MODEL
    ncc, ncl = ctx_len // S5_CHUNK, lat_len // S5_CHUNK
    rows = batch * (ncc + ncl)
    gs = S5_SLAB_GROUPS
    if col_major:
        grid_rows = lat_len // GRID_W
        assert grid_rows % S5_CHUNK == 0
        col_chunks = grid_rows // S5_CHUNK
        nlat = batch * lat_len
        xs = [h[:nlat].reshape(batch * col_chunks, S5_CHUNK, GRID_W, d),
              h[nlat:].reshape(batch * ncc, S5_CHUNK, d)]
        blocks = [pl.BlockSpec((batch * col_chunks, S5_CHUNK, GRID_W, 128), lambda q: (0, 0, 0, q)),
                  pl.BlockSpec((batch * ncc, S5_CHUNK, 128), lambda q: (0, 0, q))]
        ins, in_blocks = xs, blocks
    else:
        col_chunks = 0
        xs = [h.reshape(rows, S5_CHUNK, d)]
        blocks = [pl.BlockSpec((rows, S5_CHUNK, 128), lambda q: (0, 0, q))]
        ins = [hbf.reshape(rows, S5_CHUNK * d)] * S5_CHUNK
        in_blocks = [pl.BlockSpec((rows, 128), lambda q, t=t: (0, t * (d // 128) + q)) for t in range(S5_CHUNK)]
    kern = functools.partial(_s5_kernel, nparts=len(xs), split_steps=not col_major, batch=batch, ncc=ncc, ncl=ncl,
                             col_chunks=col_chunks)
    wspec = lambda a, b: pl.BlockSpec((2, gs, a, b), lambda q: (0, q, 0, 0))
    ys = pl.pallas_call(
        kern,
        grid=(S5_GROUPS // gs,),
        in_specs=in_blocks + [pl.BlockSpec((S5_SLAB, S5_SLAB), lambda q: (0, 0)),
                           wspec(S5_ROW, S5_STATE), wspec(S5_ROW, S5_STATE),
                           wspec(S5_STATE, S5_ROW), wspec(S5_STATE, S5_ROW),
                           wspec(S5_STATE, S5_ROW), wspec(S5_STATE, S5_ROW),
                           pl.BlockSpec((2, gs, S5_STATE), lambda q: (0, q, 0)),
                           pl.BlockSpec((2, gs, S5_STATE), lambda q: (0, q, 0))],
        out_specs=blocks,
        out_shape=[jax.ShapeDtypeStruct(x.shape, F32) for x in xs],
        scratch_shapes=[pltpu.VMEM((gs * rows, S5_STATE), F32) for _ in range(4)]
        + [pltpu.VMEM((rows, S5_SLAB), F32)],
        compiler_params=_params("parallel"),
    )(*ins, _s5_perm(), wsr, wsi, wnr, wni, wor, woi, lam_r, lam_i)
    if col_major:
        return jnp.concatenate([ys[0].reshape(-1, d), ys[1].reshape(-1, d)], axis=0)
    return ys[0].reshape(-1, d)


def _s5_glu_kernel(h_ref, y_ref, dskip_ref, w1_ref, w2_ref, b1_ref, b2_ref, x_ref, gate_ref, o_ref, z_scr):
    @pl.when(pl.program_id(1) == 0)
    def _():
        y = h_ref[...] * dskip_ref[...] + y_ref[...]
        z = 0.5 * y * (1.0 + jnp.tanh(math.sqrt(2.0 / math.pi) * (y + 0.044715 * (y * y * y))))
        z_scr[...] = z.astype(BF16)

    z = z_scr[...]
    lin = _dot(z, w1_ref[...]) + b1_ref[...]
    gat = _dot(z, w2_ref[...]) + b2_ref[...]
    o_ref[...] = x_ref[...] + gate_ref[...] * (lin * _sigmoid(gat))


def _s5_glu(h, y, dskip, w, b, x, mods, layer, dims):
    batch, lat_len, _, tm = dims
    ntok, d = x.shape
    tn = 512
    nblk = d // tn
    tiles_per_seq = lat_len // tm
    b = b.reshape(1, 2 * d)
    return pl.pallas_call(
        _s5_glu_kernel,
        grid=(ntok // tm, nblk),
        in_specs=[pl.BlockSpec((tm, d), lambda i, j: (i, 0)),
                  pl.BlockSpec((tm, d), lambda i, j: (i, 0)),
                  pl.BlockSpec((1, d), lambda i, j: (0, 0)),
                  pl.BlockSpec((d, tn), lambda i, j: (0, j)),
                  pl.BlockSpec((d, tn), lambda i, j: (0, nblk + j)),
                  pl.BlockSpec((1, tn), lambda i, j: (0, j)),
                  pl.BlockSpec((1, tn), lambda i, j: (0, nblk + j)),
                  pl.BlockSpec((tm, tn), lambda i, j: (i, j)),
                  pl.BlockSpec((None, 1, tn), lambda i, j: (
                      layer * 8 + jnp.minimum(i // tiles_per_seq, batch), 0, 2 * nblk + j))],
        out_specs=pl.BlockSpec((tm, tn), lambda i, j: (i, j)),
        out_shape=jax.ShapeDtypeStruct((ntok, d), F32),
        scratch_shapes=[pltpu.VMEM((tm, d), BF16)],
        compiler_params=_params("parallel", "arbitrary"),
    )(h, y, dskip.reshape(1, d), w, w, b, b, x, mods)


def _s5_layer(x, mods, layer, j, dims, p):
    batch, lat_len, ctx_len, _ = dims
    col_major = (j % 2) == 1
    h, hbf = _prenorm(x, mods, p['norm_g'][layer, 0], layer, dims)
    tables = _s5_tables(p['s5_a_re'][j], p['s5_a_im'][j], p['s5_log_dt'][j], p['s5_b_re'][j],
                        p['s5_b_im'][j], p['s5_c_re'][j], p['s5_c_im'][j])
    y = _s5_core(h, hbf, tables, dims, col_major)
    return _s5_glu(h, y, p['s5_d'][j], p['s5_glu_w'][j].astype(BF16), p['s5_glu_b'][j], x, mods, layer, dims)


def _seq_blocks(dims):
    batch, lat_len, ctx_len, _ = dims
    return [(lat_len, 0), (ctx_len, batch * lat_len // ctx_len)]


def _dwconv3_kernel(u_ref, w_ref, b_ref, o_ref, *maybe_bf16_ref):
    x = u_ref[...]
    n = x.shape[0]
    row = lax.broadcasted_iota(jnp.int32, x.shape, 0)
    prev = jnp.where(row == 0, 0.0, pltpu.roll(x, 1, 0))
    nxt = jnp.where(row == n - 1, 0.0, pltpu.roll(x, n - 1, 0))
    y = prev * w_ref[0:1, :] + x * w_ref[1:2, :] + nxt * w_ref[2:3, :] + b_ref[...]
    o_ref[...] = y
    for r in maybe_bf16_ref:
        r[...] = y.astype(BF16)


def _dwconv3(u, w, b, dims, col0, ncols, with_bf16):
    batch = dims[0]
    ntok = u.shape[0]
    tc = 256
    cb0 = col0 // tc
    w8 = jnp.zeros((8, w.shape[1]), F32).at[:3].set(w)
    outs = []
    for n, off in _seq_blocks(dims):
        out_shape = [jax.ShapeDtypeStruct((batch * n, ncols), F32)]
        out_specs = [pl.BlockSpec((n, tc), lambda s, j: (s, j))]
        if with_bf16:
            out_shape.append(jax.ShapeDtypeStruct((batch * n, ncols), BF16))
            out_specs.append(pl.BlockSpec((n, tc), lambda s, j: (s, j)))
        outs.append(pl.pallas_call(
            _dwconv3_kernel,
            grid=(batch, ncols // tc),
            in_specs=[pl.BlockSpec((n, tc), lambda s, j, off=off: (off + s, cb0 + j)),
                      pl.BlockSpec((8, tc), lambda s, j: (0, cb0 + j)),
                      pl.BlockSpec((1, tc), lambda s, j: (0, cb0 + j))],
            out_specs=out_specs,
            out_shape=out_shape,
            compiler_params=_params("parallel", "parallel"),
        )(u, w8, b.reshape(1, -1)))
    return outs


def _phase_table_kernel(ar_ref, ai_ref, br_ref, bi_ref, c_ref, s_ref):
    br, bi = br_ref[...], bi_ref[...]
    for r in range(ar_ref.shape[0]):
        ar, ai = ar_ref[r:r + 1, :], ai_ref[r:r + 1, :]
        c_ref[r * 16:(r + 1) * 16, :] = (ar * br - ai * bi).astype(BF16)
        s_ref[r * 16:(r + 1) * 16, :] = (ai * br + ar * bi).astype(BF16)


def _phase_tables(coarse, fine, period):
    def unit(m):
        ang = (m % period).astype(F32) * (2.0 * math.pi / period)
        return jnp.cos(ang), jnp.sin(ang)

    ar, ai = unit(coarse)
    br, bi = unit(fine)
    r1, ncol = coarse.shape
    rb = min(32, r1)
    out = jax.ShapeDtypeStruct((r1 * 16, ncol), BF16)
    return pl.pallas_call(
        _phase_table_kernel,
        grid=(r1 // rb,),
        in_specs=[pl.BlockSpec((rb, ncol), lambda i: (i, 0)), pl.BlockSpec((rb, ncol), lambda i: (i, 0)),
                  pl.BlockSpec((16, ncol), lambda i: (0, 0)), pl.BlockSpec((16, ncol), lambda i: (0, 0))],
        out_specs=[pl.BlockSpec((rb * 16, ncol), lambda i: (i, 0)), pl.BlockSpec((rb * 16, ncol), lambda i: (i, 0))],
        out_shape=[out, out],
        compiler_params=_params("parallel"),
    )(ar, ai, br, bi)


def _dft_tables(n):
    m = n // 2
    idx = jnp.arange(m, dtype=jnp.int32)[None, :]
    r1 = jnp.arange(m // 16, dtype=jnp.int32)[:, None]
    r0 = jnp.arange(16, dtype=jnp.int32)[:, None]
    fwd, inv = [], []
    for parity in range(2):
        tau = 2 * idx + parity
        fwd.append(_phase_tables(32 * r1 * tau, (2 * r0 + 1) * tau, 4 * n))
        odd = 2 * idx + 1
        inv.append(_phase_tables(32 * r1 * odd, (2 * r0 + parity) * odd, 4 * n))
    stack = lambda tabs, which: jnp.stack([t[which] for t in tabs])
    return stack(fwd, 0), stack(fwd, 1), stack(inv, 0), stack(inv, 1)


def _hy_filter_kernel(feat_ref, w1_ref, b1_ref, w2_ref, b2_ref, w3_ref, b3_ref, fq_ref, tu_ref, dl_ref,
                      w4_ref, hs_ref, hd_ref, hdn_scr, split_scr):
    @pl.when(pl.program_id(0) == 0)
    def _():
        fq = fq_ref[...]
        h = jnp.sin(fq * (jnp.dot(feat_ref[...], w1_ref[...], precision=HIGHEST, preferred_element_type=F32) + b1_ref[...]))
        h = jnp.sin(fq * (jnp.dot(h, w2_ref[...], precision=HIGHEST, preferred_element_type=F32) + b2_ref[...]))
        h = jnp.sin(fq * (jnp.dot(h, w3_ref[...], precision=HIGHEST, preferred_element_type=F32) + b3_ref[...]))
        hdn_scr[...] = h

    hdn = hdn_scr[...]
    decay = jnp.exp(-tu_ref[...] * dl_ref[...])
    row = lax.broadcasted_iota(jnp.int32, decay.shape, 0)
    m = decay.shape[0] // 2
    for order in range(HY_ORDER):
        fwd = jnp.dot(hdn, w4_ref[2 * order], precision=HIGHEST, preferred_element_type=F32) * decay
        bwd = jnp.dot(hdn, w4_ref[2 * order + 1], precision=HIGHEST, preferred_element_type=F32) * decay
        bwd = jnp.where(row == 0, 0.0, bwd)
        norm = jnp.sum(jnp.abs(fwd), axis=0, keepdims=True) + jnp.sum(jnp.abs(bwd), axis=0, keepdims=True)
        for out_ref, vals in ((hs_ref, (fwd + bwd) / norm), (hd_ref, (fwd - bwd) / norm)):
            split_scr[...] = vals
            for parity in range(2):
                out_ref[order, parity] = split_scr[pl.ds(parity, m, stride=2), :].astype(BF16)


def _hy_filters(n, p, j):
    d = D_MODEL
    t = jnp.arange(n, dtype=F32)
    t_unit = t / max(n - 1, 1)
    bands = jnp.linspace(1e-4, HY_BANDS - 1, HY_BANDS, dtype=F32)
    ang = (2.0 * math.pi / n) * t[:, None] * bands[None, :]
    feats = jnp.concatenate([t_unit[:, None], jnp.cos(ang), -jnp.sin(ang)], axis=-1)
    feats = jnp.pad(feats, ((0, 0), (0, HY_EMB_PAD - HY_EMB)))
    w1 = jnp.pad(p['hy_f_w1'][j], ((0, HY_EMB_PAD - HY_EMB), (0, 0)))
    deltas = jnp.abs(jnp.linspace(HY_MIN_DECAY, HY_MAX_DECAY, d, dtype=F32)).reshape(1, d)
    w4 = p['hy_f_w4'][j].reshape(HY_FILTER_HIDDEN, 2 * HY_ORDER, d).transpose(1, 0, 2)
    hid = HY_FILTER_HIDDEN
    tc = 128
    full = lambda shape: pl.BlockSpec(shape, lambda c: tuple(0 for _ in shape))
    row = lambda v: v.reshape(1, hid)
    return pl.pallas_call(
        _hy_filter_kernel,
        grid=(d // tc,),
        in_specs=[full((n, HY_EMB_PAD)), full((HY_EMB_PAD, hid)), full((1, hid)), full((hid, hid)), full((1, hid)),
                  full((hid, hid)), full((1, hid)), full((1, hid)), full((n, 1)),
                  pl.BlockSpec((1, tc), lambda c: (0, c)),
                  pl.BlockSpec((2 * HY_ORDER, hid, tc), lambda c: (0, 0, c))],
        out_specs=[pl.BlockSpec((HY_ORDER, 2, n // 2, tc), lambda c: (0, 0, 0, c)),
                   pl.BlockSpec((HY_ORDER, 2, n // 2, tc), lambda c: (0, 0, 0, c))],
        out_shape=[jax.ShapeDtypeStruct((HY_ORDER, 2, n // 2, d), BF16),
                   jax.ShapeDtypeStruct((HY_ORDER, 2, n // 2, d), BF16)],
        scratch_shapes=[pltpu.VMEM((n, hid), F32), pltpu.VMEM((n, tc), F32)],
        compiler_params=_params("arbitrary"),
    )(feats, w1, row(p['hy_f_b1'][j]), p['hy_f_w2'][j], row(p['hy_f_b2'][j]), p['hy_f_w3'][j],
      row(p['hy_f_b3'][j]), row(p['hy_f_freq'][j]), t_unit.reshape(n, 1), deltas, w4)


def _half_spectra(c_ref, s_ref, xe_c, xo_c, xe_s, xo_s):
    a, b = _dot(c_ref[0], xe_c), _dot(c_ref[1], xo_c)
    cs, ds = _dot(s_ref[0], xe_s), _dot(s_ref[1], xo_s)
    return a + b, a - b, cs + ds, ds - cs


def _dft_filter_kernel(c_ref, s_ref, hse_ref, hso_ref, hde_ref, hdo_ref, hr_ref, hi_ref):
    pc, qc, ps, qs = _half_spectra(c_ref, s_ref, hse_ref[...], hso_ref[...], hde_ref[...], hdo_ref[...])
    hr_ref[0], hr_ref[1] = pc, qc
    hi_ref[0], hi_ref[1] = -ps, -qs


def _dft_filter(cmat, smat, hs, hd, tk, td):
    order, _, m, d = hs.shape
    nd = d // td
    tab = pl.BlockSpec((2, tk, m), lambda i, o, j: (0, i, 0))
    even = pl.BlockSpec((None, None, m, td), lambda i, o, j: (o, 0, 0, j))
    odd = pl.BlockSpec((None, None, m, td), lambda i, o, j: (o, 1, 0, j))
    out = pl.BlockSpec((None, 2, tk, td), lambda i, o, j: (o, 0, i, j))
    sds = jax.ShapeDtypeStruct((order, 2, m, d), F32)
    return pl.pallas_call(
        _dft_filter_kernel,
        grid=(m // tk, order, nd),
        in_specs=[tab, tab, even, odd, even, odd],
        out_specs=[out, out],
        out_shape=[sds, sds],
        compiler_params=_params("parallel", "parallel", "parallel"),
    )(cmat, smat, hs, hs, hd, hd)


def _lane_blocks(nrows, width, row_block, col_block):
    ncol = width // 128
    return [pl.BlockSpec((nrows, 128), lambda *g, c=c: (row_block(*g), col_block(*g) * ncol + c))
            for c in range(ncol)]


def _parity_rows(refs, parity, count):
    cols = [r[pl.ds(parity, count, stride=2), :] for r in refs]
    return cols[0] if len(cols) == 1 else jnp.concatenate(cols, axis=1)


def _dft_fwd_kernel(c_ref, s_ref, *refs):
    z_refs = refs[:-4]
    hr_ref, hi_ref, ua_ref, ub_ref = refs[-4:]
    m = z_refs[0].shape[0] // 2
    ze = _parity_rows(z_refs, 0, m).astype(BF16)
    zo = _parity_rows(z_refs, 1, m).astype(BF16)
    pc, qc, ps, qs = _half_spectra(c_ref, s_ref, ze, zo, ze, zo)
    yrp = pc * hr_ref[0] + ps * hi_ref[0]
    yip = pc * hi_ref[0] - ps * hr_ref[0]
    yrq = qc * hr_ref[1] + qs * hi_ref[1]
    yiq = qc * hi_ref[1] - qs * hr_ref[1]
    ua_ref[0] = (yrp + yrq).astype(BF16)
    ub_ref[0] = (yip - yiq).astype(BF16)
    ua_ref[1] = (yrp - yrq).astype(BF16)
    ub_ref[1] = (yip + yiq).astype(BF16)


def _dft_fwd(cmat, smat, z, zcol, d, hr, hi, order, batch, tk, td):
    m = cmat.shape[1]
    tab = pl.BlockSpec((2, tk, m), lambda i, s, j: (0, i, 0), pipeline_mode=pl.Buffered(1))
    hspec = pl.BlockSpec((None, 2, tk, td), lambda i, s, j: (order, 0, i, j))
    out = pl.BlockSpec((2, None, tk, td), lambda i, s, j: (0, s, i, j))
    sds = jax.ShapeDtypeStruct((2, batch, m, d), BF16)
    return pl.pallas_call(
        _dft_fwd_kernel,
        grid=(m // tk, batch, d // td),
        in_specs=[tab, tab] + _lane_blocks(2 * m, td, lambda i, s, j: s, lambda i, s, j: zcol // td + j)
        + [hspec, hspec],
        out_specs=[out, out],
        out_shape=[sds, sds],
        compiler_params=_params("parallel", "parallel", "parallel"),
    )(cmat, smat, *([z] * (td // 128)), hr, hi)


def _dft_inv_kernel(ct_ref, st_ref, ua_ref, ub_ref, *refs, inv_n):
    ncol = (len(refs) - 2) // 3
    xg_refs, zp_refs, bias_ref, o_ref = refs[:ncol], refs[ncol:2 * ncol], refs[2 * ncol], refs[2 * ncol + 1]
    mix_scrs = refs[2 * ncol + 2:]
    ts = ct_ref.shape[1]
    bias = bias_ref[...]
    for parity in range(2):
        y = (_dot(ct_ref[parity], ua_ref[parity]) - _dot(st_ref[parity], ub_ref[parity])) * inv_n
        out = _parity_rows(xg_refs, parity, ts) * (y + _parity_rows(zp_refs, parity, ts) * bias)
        for c, scr in enumerate(mix_scrs):
            scr[pl.ds(parity, ts, stride=2), :] = out[:, c * 128:(c + 1) * 128]
    for c, scr in enumerate(mix_scrs):
        o_ref[:, c * 128:(c + 1) * 128] = scr[...]


def _dft_inv(ctm, stm, ua, ub, xg, xg_col0, zprev, zcol, bias, batch, ts, td):
    m = ctm.shape[1]
    d = ua.shape[3]
    ncol = td // 128
    tab = pl.BlockSpec((2, ts, m), lambda i, s, j: (0, i, 0))
    spec = pl.BlockSpec((2, None, m, td), lambda i, s, j: (0, s, 0, j))
    rblk = lambda i, s, j: s * (m // ts) + i
    rows = lambda col0: _lane_blocks(2 * ts, td, rblk, lambda i, s, j: col0 // td + j)
    kern = functools.partial(_dft_inv_kernel, inv_n=1.0 / (2 * m))
    return pl.pallas_call(
        kern,
        grid=(m // ts, batch, d // td),
        in_specs=[tab, tab, spec, spec] + rows(xg_col0) + rows(zcol) + [pl.BlockSpec((1, td), lambda i, s, j: (0, j))],
        out_specs=pl.BlockSpec((2 * ts, td), lambda i, s, j: (rblk(i, s, j), j)),
        out_shape=jax.ShapeDtypeStruct((zprev.shape[0], d), F32),
        scratch_shapes=[pltpu.VMEM((2 * ts, 128), F32) for _ in range(ncol)],
        compiler_params=_params("parallel", "parallel", "parallel"),
    )(ctm, stm, ua, ub, *([xg] * ncol), *([zprev] * ncol), bias.reshape(1, d))


def _hyena_layer(x, mods, layer, j, dims, p):
    batch, lat_len, ctx_len, _ = dims
    d = D_MODEL
    u = _mm_norm(x, mods, p['norm_g'][layer, 0], p['hy_in_w'], j, 0, 3 * d, p['hy_in_b'][j], layer, dims, 512)
    parts = _dwconv3(u, p['hy_conv_w'][j], p['hy_conv_b'][j], dims, 0, 3 * d, False)
    z_out = []
    for (n, _), (conv,) in zip(_seq_blocks(dims), parts):
        m = n // 2
        cmat, smat, ctm, stm = _dft_tables(n)
        hs, hd = _hy_filters(n, p, j)
        hr, hi = _dft_filter(cmat, smat, hs, hd, min(512, m), 512)
        zprev, zcol = conv, 0
        for order in range(HY_ORDER):
            ua, ub = _dft_fwd(cmat, smat, zprev, zcol, d, hr, hi, order, batch, min(1024, m), 256)
            zprev = _dft_inv(ctm, stm, ua, ub, conv, (1 + order) * d, zprev, zcol, p['hy_bias'][j, order],
                             batch, min(512, m), 256)
            zcol = 0
        z_out.append(zprev)
    z = jnp.concatenate(z_out, axis=0)
    return _mm_res(z, p['hy_out_w'][j].astype(BF16), p['hy_out_b'][j], x, mods, layer, dims, 512)


def _gdn_conv_kernel(p_ref, w_ref, o_ref):
    x = p_ref[...]
    n = x.shape[0]
    row = lax.broadcasted_iota(jnp.int32, x.shape, 0)
    acc = x * w_ref[2:3, :]
    for s in (1, 2):
        prev = jnp.where(row < s, 0.0, pltpu.roll(x, s, 0))
        nxt = jnp.where(row >= n - s, 0.0, pltpu.roll(x, n - s, 0))
        acc = acc + prev * w_ref[2 - s:3 - s, :] + nxt * w_ref[2 + s:3 + s, :]
    y = _silu(acc)
    head = pl.program_id(1)
    inv = lax.rsqrt(jnp.sum(y * y, axis=-1, keepdims=True) + 1e-6)
    inv = inv * jnp.where(head < GDN_K_HEADS, GDN_HEAD ** -0.5, 1.0)
    o_ref[...] = y * jnp.where(head < 2 * GDN_K_HEADS, inv, 1.0)


def _gdn_conv(proj, w, dims):
    batch = dims[0]
    ntok = proj.shape[0]
    nheads = GDN_CONV_DIM // GDN_HEAD
    w8 = jnp.zeros((8, GDN_CONV_DIM), F32).at[:5].set(w)
    out = None
    for n, off in _seq_blocks(dims):
        args = [proj, w8]
        in_specs = [pl.BlockSpec((n, GDN_HEAD), lambda s, h, off=off: (off + s, h)),
                    pl.BlockSpec((8, GDN_HEAD), lambda s, h: (0, h))]
        aliases = {}
        if out is not None:
            args.append(out)
            in_specs.append(pl.BlockSpec(memory_space=pl.ANY))
            aliases = {2: 0}
        kern = _gdn_conv_kernel if out is None else (lambda p_ref, w_ref, _, o_ref: _gdn_conv_kernel(p_ref, w_ref, o_ref))
        out = pl.pallas_call(
            kern,
            grid=(batch, nheads),
            in_specs=in_specs,
            out_specs=pl.BlockSpec((None, n, GDN_HEAD), lambda s, h, off=off: (h, off + s, 0)),
            out_shape=jax.ShapeDtypeStruct((nheads, ntok, GDN_HEAD), F32),
            input_output_aliases=aliases,
            compiler_params=_params("parallel", "parallel"),
        )(*args)
    return out


def _gdn_gate_kernel(ab_ref, alog_ref, dtb_ref, o_ref):
    c = GDN_CHUNK
    row = lax.broadcasted_iota(jnp.int32, (c, c), 0)
    col = lax.broadcasted_iota(jnp.int32, (c, c), 1)
    lower = jnp.where(row >= col, 1.0, 0.0).astype(F32)
    upper = jnp.where(row <= col, 1.0, 0.0).astype(F32)
    lane = lax.broadcasted_iota(jnp.int32, (c, 128), 1)
    for r in range(ab_ref.shape[0] // c):
        ab = ab_ref[r * c:(r + 1) * c, :]
        xs = ab + dtb_ref[...]
        softplus = jnp.maximum(xs, 0.0) + jnp.log(1.0 + jnp.exp(-jnp.abs(xs)))
        g = -jnp.exp(alog_ref[...]) * softplus
        cum_f = jnp.dot(lower, g, precision=HIGHEST, preferred_element_type=F32)
        cum_b = jnp.dot(upper, g, precision=HIGHEST, preferred_element_type=F32)
        o_ref[r * c:(r + 1) * c, :] = jnp.where(lane < GDN_V_HEADS, cum_f,
                                                jnp.where(lane < 2 * GDN_V_HEADS, cum_b, _sigmoid(ab)))


def _gdn_gates(ab, a_log, dt_bias, tm):
    ntok = ab.shape[0]
    pad = jnp.zeros((2 * GDN_V_HEADS,), F32)
    alog = jnp.concatenate([a_log.reshape(-1), pad]).reshape(1, 128)
    dtb = jnp.concatenate([dt_bias.reshape(-1), pad]).reshape(1, 128)
    return pl.pallas_call(
        _gdn_gate_kernel,
        grid=(ntok // tm,),
        in_specs=[pl.BlockSpec((tm, 128), lambda i: (i, 0)),
                  pl.BlockSpec((1, 128), lambda i: (0, 0)),
                  pl.BlockSpec((1, 128), lambda i: (0, 0))],
        out_specs=pl.BlockSpec((tm, 128), lambda i: (i, 0)),
        out_shape=jax.ShapeDtypeStruct((ntok, 128), F32),
        compiler_params=_params("parallel"),
    )(ab, alog, dtb)


def _unit_triangular_inverses(mats):
    c = mats[0].shape[0]
    row = lax.broadcasted_iota(jnp.int32, (c, c), 0)
    col = lax.broadcasted_iota(jnp.int32, (c, c), 1)
    eye = jnp.where(row == col, 1.0, 0.0).astype(F32)
    ts = [eye - a for a in mats]
    pws = list(mats)
    for _ in range(int(math.log2(c)) - 1):
        pwbs = [pw.astype(BF16) for pw in pws]
        pws = [_dot(pwb, pwb) for pwb in pwbs]
        ts = [t + _dot(t.astype(BF16), pw.astype(BF16)) for t, pw in zip(ts, pws)]
    ms = [eye + a for a in mats]
    m_his = [m.astype(BF16) for m in ms]
    m_los = [(m - m_hi.astype(F32)).astype(BF16) for m, m_hi in zip(ms, m_his)]
    t_his = [t.astype(BF16) for t in ts]
    t_los = [(t - t_hi.astype(F32)).astype(BF16) for t, t_hi in zip(ts, t_his)]
    resids = [eye - (_dot(m_hi, t_hi) + (_dot(m_hi, t_lo) + _dot(m_lo, t_hi)))
              for m_hi, m_lo, t_hi, t_lo in zip(m_his, m_los, t_his, t_los)]
    return [t + _dot(t_hi, r.astype(BF16)) for t, t_hi, r in zip(ts, t_his, resids)]


def _gdn_chunk_kernel(qf_ref, kf_ref, vf_ref, gcf_ref, grf_ref, qb_ref, kb_ref, vb_ref, gcb_ref, grb_ref,
                      of_ref, ob_ref, state_scr, *, nchunks):
    c = GDN_CHUNK

    @pl.when(pl.program_id(2) == 0)
    def _():
        state_scr[...] = jnp.zeros_like(state_scr)

    row = lax.broadcasted_iota(jnp.int32, (c, c), 0)
    col = lax.broadcasted_iota(jnp.int32, (c, c), 1)
    blocks = ((qf_ref, kf_ref, vf_ref, gcf_ref, grf_ref, of_ref), (qb_ref, kb_ref, vb_ref, gcb_ref, grb_ref, ob_ref))
    nkh = qf_ref.shape[0]
    keys, amats, part = [], [], {}
    for direction, (q_ref, k_ref, v_ref, gc_ref, gr_ref, o_ref) in enumerate(blocks):
        if direction == 0:
            incl, strict, last = row >= col, row > col, c - 1
        else:
            incl, strict, last = row <= col, row < col, 0
        for kh in range(nkh):
            for ci in range(nchunks):
                rows = slice(ci * c, (ci + 1) * c)
                q = q_ref[kh, rows, :].astype(BF16)
                k32 = k_ref[kh, rows, :]
                k = k32.astype(BF16)
                k_t = k32.T
                gates_c = gc_ref[kh, ci]
                gates_r = gr_ref[kh, ci]
                kk = _dot_nt(k, k)
                qk = _dot_nt(q, k)
                for e in range(2):
                    ch_g, ch_b = direction * 2 + e, 4 + direction * 2 + e
                    gcc, gcr = gates_c[:, ch_g:ch_g + 1], gates_r[ch_g:ch_g + 1, :]
                    beta_c, beta_r = gates_c[:, ch_b:ch_b + 1], gates_r[ch_b:ch_b + 1, :]
                    decay = jnp.where(incl, jnp.exp(jnp.where(incl, gcc - gcr, 0.0)), 0.0)
                    key = (direction, kh, ci, e)
                    keys.append(key)
                    amats.append(jnp.where(strict, kk * beta_c * decay, 0.0))
                    g_last = gcr[:, last:last + 1]
                    part[key] = dict(
                        q=q, k=k, v=v_ref[2 * kh + e, rows, :].astype(BF16), attn=(qk * decay).astype(BF16),
                        beta_r=beta_r, wscale=beta_r * jnp.exp(gcr), egc=jnp.exp(gcc),
                        kg_t=(k_t * jnp.exp(g_last - gcr)).astype(BF16), e_last=jnp.exp(g_last))
    tmats = _unit_triangular_inverses(amats)
    local = {}
    for key, t in zip(keys, tmats):
        p = part[key]
        u = _dot((t * p['beta_r']).astype(BF16), p['v'])
        w = _dot((t * p['wscale']).astype(BF16), p['k']).astype(BF16)
        local[key] = (p['q'], u, w, p['attn'], p['egc'], p['kg_t'], p['e_last'])

    nslots = state_scr.shape[0]
    states = [state_scr[slot] for slot in range(nslots)]
    streams = [(direction, kh, e) for direction in range(2) for kh in range(nkh) for e in range(2)]
    out_refs = (of_ref, ob_ref)
    for step in range(nchunks):
        cur = [local[d, kh, (step if d == 0 else nchunks - 1 - step), e] for d, kh, e in streams]
        state_bs = [s.astype(BF16) for s in states]
        ws = [_dot(p[2], sb) for p, sb in zip(cur, state_bs)]
        qs = [_dot(p[0], sb) for p, sb in zip(cur, state_bs)]
        v_news = [(p[1] - w).astype(BF16) for p, w in zip(cur, ws)]
        intra = [_dot(p[3], vn) for p, vn in zip(cur, v_news)]
        upd = [_dot(p[5], vn) for p, vn in zip(cur, v_news)]
        for (d, kh, e), p, q_s, o_in in zip(streams, cur, qs, intra):
            ci = step if d == 0 else nchunks - 1 - step
            out_refs[d][2 * kh + e, ci * c:(ci + 1) * c, :] = p[4] * q_s + o_in
        states = [s * p[6] + dlt for s, p, dlt in zip(states, cur, upd)]
    for slot in range(nslots):
        state_scr[slot] = states[slot]


def _gdn_chunks(qkvh, gates, dims):
    batch, lat_len, ctx_len, _ = dims
    c = GDN_CHUNK
    ntok = qkvh.shape[1]
    per_step = min(4, ctx_len // c)
    rows = per_step * c
    nbc, nbl = ctx_len // rows, lat_len // rows
    ctx0 = batch * lat_len // rows
    gk = gates.reshape(ntok // c, c, 4, GDN_K_HEADS, 2).transpose(3, 0, 1, 2, 4).reshape(GDN_K_HEADS, ntok // c, c, 8)
    gk_t = gk.transpose(0, 1, 3, 2)

    def fwd_blk(b, j):
        return jnp.where(j < nbc, ctx0 + b * nbc + j, b * nbl + j - nbc)

    def bwd_blk(b, j):
        return jnp.where(j < nbc, ctx0 + b * nbc + nbc - 1 - j, b * nbl + nbl - 1 - (j - nbc))

    nkh = GDN_KHEADS_PER_STEP
    hblocks = GDN_K_HEADS // nkh

    def specs(blk):
        return [pl.BlockSpec((nkh, rows, GDN_HEAD), lambda b, h, j: (h, blk(b, j), 0)),
                pl.BlockSpec((nkh, rows, GDN_HEAD), lambda b, h, j: (hblocks + h, blk(b, j), 0)),
                pl.BlockSpec((2 * nkh, rows, GDN_HEAD), lambda b, h, j: (hblocks + h, blk(b, j), 0)),
                pl.BlockSpec((nkh, per_step, c, 8), lambda b, h, j: (h, blk(b, j), 0, 0)),
                pl.BlockSpec((nkh, per_step, 8, c), lambda b, h, j: (h, blk(b, j), 0, 0))]

    out_sd = jax.ShapeDtypeStruct((GDN_V_HEADS, ntok, GDN_HEAD), F32)
    return pl.pallas_call(
        functools.partial(_gdn_chunk_kernel, nchunks=per_step),
        grid=(batch, hblocks, nbc + nbl),
        in_specs=specs(fwd_blk) + specs(bwd_blk),
        out_specs=[pl.BlockSpec((2 * nkh, rows, GDN_HEAD), lambda b, h, j: (h, fwd_blk(b, j), 0)),
                   pl.BlockSpec((2 * nkh, rows, GDN_HEAD), lambda b, h, j: (h, bwd_blk(b, j), 0))],
        out_shape=[out_sd, out_sd],
        scratch_shapes=[pltpu.VMEM((4 * nkh, GDN_HEAD, GDN_HEAD), F32)],
        compiler_params=_params("parallel", "parallel", "arbitrary"),
    )(qkvh, qkvh, qkvh, gk, gk_t, qkvh, qkvh, qkvh, gk, gk_t)


def _gdn_gated_norm_kernel(of_ref, ob_ref, z_ref, ng_ref, a_ref):
    ng = ng_ref[...]
    for h in range(of_ref.shape[0]):
        o = of_ref[h] + ob_ref[h]
        o = o * lax.rsqrt(jnp.mean(o * o, axis=-1, keepdims=True) + EPS)
        z = z_ref[:, h * GDN_HEAD:(h + 1) * GDN_HEAD]
        a_ref[:, h * GDN_HEAD:(h + 1) * GDN_HEAD] = (o * ng * _silu(z)).astype(BF16)


def _gdn_gated_norm(o_f, o_b, proj, norm_g, dims):
    tm = dims[3]
    ntok = proj.shape[0]
    hb = 8
    zblk0 = GDN_CONV_DIM // (hb * GDN_HEAD)
    return pl.pallas_call(
        _gdn_gated_norm_kernel,
        grid=(ntok // tm, GDN_V_HEADS // hb),
        in_specs=[pl.BlockSpec((hb, tm, GDN_HEAD), lambda i, h: (h, i, 0)),
                  pl.BlockSpec((hb, tm, GDN_HEAD), lambda i, h: (h, i, 0)),
                  pl.BlockSpec((tm, hb * GDN_HEAD), lambda i, h: (i, zblk0 + h)),
                  pl.BlockSpec((1, GDN_HEAD), lambda i, h: (0, 0))],
        out_specs=pl.BlockSpec((tm, hb * GDN_HEAD), lambda i, h: (i, h)),
        out_shape=jax.ShapeDtypeStruct((ntok, GDN_V), BF16),
        compiler_params=_params("parallel", "parallel"),
    )(o_f, o_b, proj, norm_g.reshape(1, GDN_HEAD))


def _gdn_layer(x, mods, layer, j, dims, p):
    in_w = p['gdn_in_w']
    g = p['norm_g'][layer, 0]
    proj = _mm_norm(x, mods, g, in_w, j, 0, GDN_MAIN, jnp.zeros((GDN_MAIN,), F32), layer, dims, 512)
    ab = _mm_norm(x, mods, g, in_w, j, GDN_MAIN, 128, jnp.zeros((128,), F32), layer, dims, 128)
    qkvh = _gdn_conv(proj, p['gdn_conv_w'][j], dims)
    gates = _gdn_gates(ab, p['gdn_a_log'][j], p['gdn_dt_bias'][j], dims[3])
    o_f, o_b = _gdn_chunks(qkvh, gates, dims)
    a = _gdn_gated_norm(o_f, o_b, proj, p['gdn_norm_g'][j], dims)
    return _mm_res(a, p['gdn_out_w'][j].astype(BF16), jnp.zeros((D_MODEL,), F32), x, mods, layer, dims, 512)


def kernel(x, c, ctx, c_ctx, ada_w, ada_b, norm_g, final_g, ffn_w_gate, ffn_w_up, ffn_w_down, s5_a_re, s5_a_im, s5_log_dt, s5_b_re, s5_b_im, s5_c_re, s5_c_im, s5_d, s5_glu_w, s5_glu_b, hy_in_w, hy_in_b, hy_conv_w, hy_conv_b, hy_f_w1, hy_f_b1, hy_f_w2, hy_f_b2, hy_f_w3, hy_f_b3, hy_f_w4, hy_f_freq, hy_bias, hy_out_w, hy_out_b, gdn_in_w, gdn_conv_w, gdn_a_log, gdn_dt_bias, gdn_norm_g, gdn_out_w):
    p = dict(locals())
    batch, lat_len, d = x.shape
    ctx_len = ctx.shape[1]
    depth = ada_w.shape[0]
    tm = min(512, batch * ctx_len)
    dims = (batch, lat_len, ctx_len, tm)
    assert d == D_MODEL and lat_len % tm == 0 and (batch * ctx_len) % tm == 0 and batch + 1 <= 8

    cvec = jnp.zeros((8, d), F32).at[:batch].set(c).at[batch].set(c_ctx)
    mods = _ada_all(cvec, ada_w, ada_b).reshape(depth * 8, 1, 6 * d)
    tok = jnp.concatenate([x.reshape(batch * lat_len, d), ctx.reshape(batch * ctx_len, d)], axis=0)
    for i in range(depth):
        kind, j = i % N_MIXERS, i // N_MIXERS
        if kind == 0:
            tok = _s5_layer(tok, mods, i, j, dims, p)
        elif kind == 1:
            tok = _hyena_layer(tok, mods, i, j, dims, p)
        else:
            tok = _gdn_layer(tok, mods, i, j, dims, p)
        tok = _ffn(tok, mods, norm_g[i, 1], ffn_w_gate, ffn_w_up, ffn_w_down, i, dims, with_ctx=i < depth - 1)
    out = _final_norm(tok, final_g, batch * lat_len, tm)
    return out.reshape(batch, lat_len, d)
```

```python
import functools
import math

import jax
import jax.numpy as jnp
from jax import lax
from jax.experimental import pallas as pl
from jax.experimental.pallas import tpu as pltpu

F32 = jnp.float32
BF16 = jnp.bfloat16
HIGHEST = lax.Precision.HIGHEST

D_MODEL = 2048
GRID_W = 64
EPS = 1e-6
N_MIXERS = 3

S5_GROUP = 16
S5_STATE = 64
S5_GROUPS = D_MODEL // S5_GROUP
S5_CHUNK = 16
S5_ROW = S5_CHUNK * S5_GROUP
S5_SLAB_GROUPS = 128 // S5_GROUP
S5_SLAB = S5_SLAB_GROUPS * S5_ROW

HY_ORDER = 2
HY_BANDS = 16
HY_EMB = 1 + 2 * HY_BANDS
HY_EMB_PAD = 128
HY_FILTER_HIDDEN = 64
HY_DECAY_TARGET = 1e-2
HY_MAX_DECAY = math.log(HY_DECAY_TARGET) / 0.3
HY_MIN_DECAY = math.log(HY_DECAY_TARGET) / 1.5

GDN_K_HEADS = 16
GDN_V_HEADS = 32
GDN_HEAD = 128
GDN_QK = GDN_K_HEADS * GDN_HEAD
GDN_V = GDN_V_HEADS * GDN_HEAD
GDN_CONV_DIM = 2 * GDN_QK + GDN_V
GDN_MAIN = GDN_CONV_DIM + GDN_V
GDN_CHUNK = 64
GDN_KHEADS_PER_STEP = 2

VMEM_LIMIT_BYTES = 56 * 1024 * 1024


def _params(*sem):
    return pltpu.CompilerParams(dimension_semantics=sem, vmem_limit_bytes=VMEM_LIMIT_BYTES)


def _dot(a, b):
    return jnp.dot(a, b, preferred_element_type=F32)


def _dot_nt(a, b):
    return lax.dot_general(a, b, (((1,), (1,)), ((), ())), preferred_element_type=F32)


def _dot_tn(a, b):
    return lax.dot_general(a, b, (((0,), (0,)), ((), ())), preferred_element_type=F32)


def _sigmoid(x):
    return 1.0 / (1.0 + jnp.exp(-x))


def _silu(x):
    return x * _sigmoid(x)


def _norm_mod(x, g, sc, sh):
    y = x * lax.rsqrt(jnp.mean(x * x, axis=-1, keepdims=True) + EPS) * g
    return y * (1.0 + sc) + sh


def _mod_spec(layer, tm, lat_len, batch, width, col_block):
    tiles_per_seq = lat_len // tm

    def index(i, *_):
        return (layer * 8 + jnp.minimum(i // tiles_per_seq, batch), 0, col_block)

    return pl.BlockSpec((None, 1, width), index)


def _ada_kernel(c_ref, w_ref, b_ref, o_ref):
    c = c_ref[...]
    s = _silu(c).astype(BF16)
    o_ref[...] = _dot(s, w_ref[...].astype(BF16)) + b_ref[...]


def _ada_all(cvec, ada_w, ada_b):
    depth, d, n6 = ada_w.shape
    tn = 1024
    return pl.pallas_call(
        _ada_kernel,
        grid=(depth, n6 // tn),
        in_specs=[pl.BlockSpec((8, d), lambda l, j: (0, 0)),
                  pl.BlockSpec((None, d, tn), lambda l, j: (l, 0, j)),
                  pl.BlockSpec((None, 1, tn), lambda l, j: (l, 0, j))],
        out_specs=pl.BlockSpec((None, 8, tn), lambda l, j: (l, 0, j)),
        out_shape=jax.ShapeDtypeStruct((depth, 8, n6), F32),
        compiler_params=_params("parallel", "parallel"),
    )(cvec, ada_w, ada_b.reshape(depth, 1, n6))


def _prenorm_kernel(x_ref, mod_ref, g_ref, o_ref, obf_ref):
    d = D_MODEL
    h = _norm_mod(x_ref[...], g_ref[...], mod_ref[:, d:2 * d], mod_ref[:, 0:d])
    o_ref[...] = h
    obf_ref[...] = h.astype(BF16)


def _prenorm(x, mods, g, layer, dims):
    batch, lat_len, _, tm = dims
    ntok, d = x.shape
    return pl.pallas_call(
        _prenorm_kernel,
        grid=(ntok // tm,),
        in_specs=[pl.BlockSpec((tm, d), lambda i: (i, 0)),
                  _mod_spec(layer, tm, lat_len, batch, 6 * d, 0),
                  pl.BlockSpec((1, d), lambda i: (0, 0))],
        out_specs=[pl.BlockSpec((tm, d), lambda i: (i, 0)), pl.BlockSpec((tm, d), lambda i: (i, 0))],
        out_shape=[jax.ShapeDtypeStruct((ntok, d), F32), jax.ShapeDtypeStruct((ntok, d), BF16)],
        compiler_params=_params("parallel"),
    )(x, mods, g.reshape(1, d))


def _mm_norm_kernel(x_ref, mod_ref, g_ref, w_ref, b_ref, *rest):
    o_ref, h_scr = rest[-2:]
    d = D_MODEL

    @pl.when(pl.program_id(1) == 0)
    def _():
        h_scr[...] = _norm_mod(x_ref[...], g_ref[...], mod_ref[:, d:2 * d], mod_ref[:, 0:d]).astype(BF16)

    o_ref[...] = _dot(h_scr[...], w_ref[...].astype(BF16)) + b_ref[...]


def _mm_norm(x, mods, g, w_all, widx, col0, n, b, layer, dims, tn):
    batch, lat_len, ctx_len, _ = dims
    ntok, d = x.shape
    out = None
    for tm, row0, nrows in ((min(1024, lat_len), 0, batch * lat_len),
                            (batch * ctx_len, batch * lat_len, batch * ctx_len)):
        blk0 = row0 // tm
        tiles_per_seq = lat_len // tm
        mod_row = (lambda i: i // tiles_per_seq) if row0 == 0 else (lambda i: batch)
        args = [x, mods, g.reshape(1, d), w_all, b.reshape(1, n)]
        in_specs = [pl.BlockSpec((tm, d), lambda i, j: (blk0 + i, 0), pipeline_mode=pl.Buffered(1)),
                    pl.BlockSpec((None, 1, 6 * d), lambda i, j: (layer * 8 + mod_row(i), 0, 0)),
                    pl.BlockSpec((1, d), lambda i, j: (0, 0)),
                    pl.BlockSpec((None, d, tn), lambda i, j: (widx, 0, col0 // tn + j)),
                    pl.BlockSpec((1, tn), lambda i, j: (0, j))]
        aliases = {}
        if out is not None:
            args.append(out)
            in_specs.append(pl.BlockSpec(memory_space=pl.ANY))
            aliases = {5: 0}
        out = pl.pallas_call(
            _mm_norm_kernel,
            grid=(nrows // tm, n // tn),
            in_specs=in_specs,
            out_specs=pl.BlockSpec((tm, tn), lambda i, j: (blk0 + i, j)),
            out_shape=jax.ShapeDtypeStruct((ntok, n), F32),
            scratch_shapes=[pltpu.VMEM((tm, d), BF16)],
            input_output_aliases=aliases,
            compiler_params=_params("parallel", "arbitrary"),
        )(*args)
    return out


def _mm_res_kernel(a_ref, w_ref, b_ref, x_ref, gate_ref, o_ref):
    y = _dot(a_ref[...].astype(BF16), w_ref[...]) + b_ref[...]
    o_ref[...] = x_ref[...] + gate_ref[...] * y


def _mm_res(a, w, b, x, mods, layer, dims, tn):
    batch, lat_len, _, tm = dims
    ntok, d = x.shape
    k = a.shape[1]
    gate_blk = 2 * d // tn
    tiles_per_seq = lat_len // tm
    return pl.pallas_call(
        _mm_res_kernel,
        grid=(ntok // tm, d // tn),
        in_specs=[pl.BlockSpec((tm, k), lambda i, j: (i, 0)),
                  pl.BlockSpec((k, tn), lambda i, j: (0, j)),
                  pl.BlockSpec((1, tn), lambda i, j: (0, j)),
                  pl.BlockSpec((tm, tn), lambda i, j: (i, j)),
                  pl.BlockSpec((None, 1, tn), lambda i, j: (
                      layer * 8 + jnp.minimum(i // tiles_per_seq, batch), 0, gate_blk + j))],
        out_specs=pl.BlockSpec((tm, tn), lambda i, j: (i, j)),
        out_shape=jax.ShapeDtypeStruct((ntok, d), F32),
        compiler_params=_params("parallel", "parallel"),
    )(a, w, b.reshape(1, d), x, mods)


def _ffn_kernel(x_ref, mod_ref, g_ref, wg_ref, wu_ref, wd_ref, *rest):
    o_ref, h_scr = rest[-2:]
    d = D_MODEL
    f = pl.program_id(1)

    @pl.when(f == 0)
    def _():
        h_scr[...] = _norm_mod(x_ref[...], g_ref[...], mod_ref[:, 4 * d:5 * d],
                               mod_ref[:, 3 * d:4 * d]).astype(BF16)

    h = h_scr[...]
    gate = _dot(h, wg_ref[...])
    up = _dot(h, wu_ref[...])
    act = (_silu(gate) * up).astype(BF16)
    down = _dot(act, wd_ref[...])

    @pl.when(f == 0)
    def _():
        o_ref[...] = down

    @pl.when(f > 0)
    def _():
        o_ref[...] += down

    @pl.when(f == pl.num_programs(1) - 1)
    def _():
        o_ref[...] = x_ref[...] + mod_ref[:, 5 * d:6 * d] * o_ref[...]


def _ffn(x, mods, g, wg, wu, wd, layer, dims, with_ctx):
    batch, lat_len, ctx_len, _ = dims
    ntok, d = x.shape
    dff = wg.shape[2]
    tf = 512
    out = None
    calls = [(min(1024, lat_len), 0, batch * lat_len)]
    if with_ctx:
        calls.append((batch * ctx_len, batch * lat_len, batch * ctx_len))
    for tm, row0, nrows in calls:
        blk0 = row0 // tm
        tiles_per_seq = lat_len // tm
        mod_row = (lambda i: i // tiles_per_seq) if row0 == 0 else (lambda i: batch)
        args = [x, mods, g.reshape(1, d), wg, wu, wd]
        in_specs = [pl.BlockSpec((tm, d), lambda i, f: (blk0 + i, 0), pipeline_mode=pl.Buffered(1)),
                    pl.BlockSpec((None, 1, 6 * d), lambda i, f: (layer * 8 + mod_row(i), 0, 0)),
                    pl.BlockSpec((1, d), lambda i, f: (0, 0)),
                    pl.BlockSpec((None, d, tf), lambda i, f: (layer, 0, f)),
                    pl.BlockSpec((None, d, tf), lambda i, f: (layer, 0, f)),
                    pl.BlockSpec((None, tf, d), lambda i, f: (layer, f, 0))]
        aliases = {}
        if out is not None:
            args.append(out)
            in_specs.append(pl.BlockSpec(memory_space=pl.ANY))
            aliases = {6: 0}
        out = pl.pallas_call(
            _ffn_kernel,
            grid=(nrows // tm, dff // tf),
            in_specs=in_specs,
            out_specs=pl.BlockSpec((tm, d), lambda i, f: (blk0 + i, 0)),
            out_shape=jax.ShapeDtypeStruct((ntok, d), F32),
            scratch_shapes=[pltpu.VMEM((tm, d), BF16)],
            input_output_aliases=aliases,
            compiler_params=_params("parallel", "arbitrary"),
        )(*args)
    return out


def _final_norm_kernel(x_ref, g_ref, o_ref):
    x = x_ref[...]
    o_ref[...] = x * lax.rsqrt(jnp.mean(x * x, axis=-1, keepdims=True) + EPS) * g_ref[...]


def _final_norm(x, g, nrows, tm):
    d = x.shape[1]
    return pl.pallas_call(
        _final_norm_kernel,
        grid=(nrows // tm,),
        in_specs=[pl.BlockSpec((tm, d), lambda i: (i, 0)), pl.BlockSpec((1, d), lambda i: (0, 0))],
        out_specs=pl.BlockSpec((tm, d), lambda i: (i, 0)),
        out_shape=jax.ShapeDtypeStruct((nrows, d), F32),
        compiler_params=_params("parallel"),
    )(x, g.reshape(1, d))


def _s5_tables(a_re, a_im, log_dt, b_re, b_im, c_re, c_im):
    t_len = S5_CHUNK
    dt = jnp.exp(log_dt)[..., None]
    lr, li = a_re * dt, a_im * dt
    er = jnp.exp(lr)
    nr, ni = er * jnp.cos(li) - 1.0, er * jnp.sin(li)
    den = a_re * a_re + a_im * a_im
    qr, qi = (nr * a_re + ni * a_im) / den, (ni * a_re - nr * a_im) / den
    bbr = qr[..., None] * b_re - qi[..., None] * b_im
    bbi = qr[..., None] * b_im + qi[..., None] * b_re
    ngroups = a_re.shape[1]
    ti = jnp.arange(t_len, dtype=F32)

    def powers(exp_fwd, exp_bwd):
        e = jnp.stack([exp_fwd, exp_bwd], axis=1)[:, :, None, None]
        mag = jnp.exp(lr[None] * e)
        return mag * jnp.cos(li[None] * e), mag * jnp.sin(li[None] * e)

    def times_c(pr, pi):
        cr = c_re[None] * pr[:, :, :, None, :] - c_im[None] * pi[:, :, :, None, :]
        ci = c_re[None] * pi[:, :, :, None, :] + c_im[None] * pr[:, :, :, None, :]
        relay = lambda t: t.transpose(1, 2, 4, 0, 3).reshape(2, ngroups, S5_STATE, S5_ROW)
        return relay(cr), relay(-ci)

    pr, pi = powers(t_len - 1 - ti, ti)
    sr = pr[..., None] * bbr[None] - pi[..., None] * bbi[None]
    si = pr[..., None] * bbi[None] + pi[..., None] * bbr[None]
    to_rows = lambda t: t.transpose(1, 2, 0, 4, 3).reshape(2, ngroups, S5_ROW, S5_STATE)
    w_sr, w_si = to_rows(sr), to_rows(si)
    w_or, w_oi = times_c(*powers(ti + 1, t_len - ti))
    w_nr, w_ni = times_c(*powers(ti - (t_len - 1), -ti))
    full = jnp.full((t_len,), float(t_len), F32)
    lam_r, lam_i = powers(full, full)
    return (w_sr, w_si, w_nr, w_ni, w_or.astype(BF16), w_oi.astype(BF16), lam_r[0], lam_i[0])


def _s5_perm():
    src = jnp.arange(S5_SLAB, dtype=jnp.int32)
    t, g, c = src // 128, (src % 128) // S5_GROUP, src % S5_GROUP
    dst = g * S5_ROW + t * S5_GROUP + c
    return (dst[:, None] == jnp.arange(S5_SLAB, dtype=jnp.int32)[None, :]).astype(BF16)


def _s5_kernel(*refs, nparts, split_steps, batch, ncc, ncl, col_chunks):
    nin = S5_CHUNK if split_steps else nparts
    x_refs = refs[:nin]
    perm_ref, wsr_ref, wsi_ref, wnr_ref, wni_ref, wor_ref, woi_ref, lr_ref, li_ref = refs[nin:nin + 9]
    y_refs = refs[nin + 9:nin + 9 + nparts]
    sr_scr, si_scr, hr_scr, hi_scr, y_scr = refs[nin + 9 + nparts:]
    rows = y_scr.shape[0]
    groups = S5_SLAB_GROUPS

    def load(t):
        if split_steps:
            return x_refs[t][...]
        parts = [r[:, t].reshape(-1, 128) for r in x_refs]
        return (parts[0] if nparts == 1 else jnp.concatenate(parts, axis=0)).astype(BF16)

    xcat = jnp.concatenate([load(t) for t in range(S5_CHUNK)], axis=1)
    u_all = _dot(xcat, perm_ref[...]).astype(BF16)
    t_in = lax.broadcasted_iota(jnp.int32, (S5_ROW, S5_ROW), 0) // S5_GROUP
    t_out = lax.broadcasted_iota(jnp.int32, (S5_ROW, S5_ROW), 1) // S5_GROUP

    def hdot(a, b):
        a_hi, b_hi = a.astype(BF16), b.astype(BF16)
        a_lo, b_lo = (a - a_hi.astype(F32)).astype(BF16), (b - b_hi.astype(F32)).astype(BF16)
        return _dot(a_hi, b_hi) + (_dot(a_hi, b_lo) + _dot(a_lo, b_hi))

    def row_of(b, j, direction):
        if direction == 0:
            kc, kl = j, j - ncc
        else:
            kc, kl = ncc - 1 - j, ncl - 1 - (j - ncc)
        if col_chunks:
            kl = (kl % col_chunks) * GRID_W + kl // col_chunks
        return jnp.where(j < ncc, batch * ncl + b * ncc + kc, b * ncl + kl)

    for direction in range(2):
        for g in range(groups):
            u = u_all[:, g * S5_ROW:(g + 1) * S5_ROW]
            sr_scr[g * rows:(g + 1) * rows, :] = _dot(u, wsr_ref[direction, g].astype(BF16))
            si_scr[g * rows:(g + 1) * rows, :] = _dot(u, wsi_ref[direction, g].astype(BF16))
        lam_r = lr_ref[direction]
        lam_i = li_ref[direction]

        def step(j, carry, direction=direction, lam_r=lam_r, lam_i=lam_i):
            out = []
            for b in range(batch):
                h_r, h_i = carry[2 * b], carry[2 * b + 1]
                idx = pl.ds(row_of(b, j, direction), groups, stride=rows)
                hr_scr[idx, :] = h_r
                hi_scr[idx, :] = h_i
                s_r = sr_scr[idx, :]
                s_i = si_scr[idx, :]
                out += [lam_r * h_r - lam_i * h_i + s_r, lam_r * h_i + lam_i * h_r + s_i]
            return tuple(out)

        zero = jnp.zeros((groups, S5_STATE), F32)
        lax.fori_loop(0, ncc + ncl, step, (zero,) * (2 * batch))
        causal = (t_out >= t_in) if direction == 0 else (t_in >= t_out)
        for g in range(groups):
            u = u_all[:, g * S5_ROW:(g + 1) * S5_ROW]
            m_intra = jnp.where(causal, hdot(wsr_ref[direction, g], wnr_ref[direction, g])
                                + hdot(wsi_ref[direction, g], wni_ref[direction, g]), 0.0).astype(BF16)
            y = (_dot(u, m_intra)
                 + _dot(hr_scr[g * rows:(g + 1) * rows, :].astype(BF16), wor_ref[direction, g])
                 + _dot(hi_scr[g * rows:(g + 1) * rows, :].astype(BF16), woi_ref[direction, g]))
            if direction == 0:
                y_scr[:, g * S5_ROW:(g + 1) * S5_ROW] = y
            else:
                y_scr[:, g * S5_ROW:(g + 1) * S5_ROW] += y

    z = _dot_nt(y_scr[...].astype(BF16), perm_ref[...])
    for t in range(S5_CHUNK):
        zt = z[:, t * 128:(t + 1) * 128]
        r0 = 0
        for y_ref in y_refs:
            n = y_ref.shape[0] * (y_ref.shape[2] if len(y_ref.shape) == 4 else 1)
            y_ref[:, t] = zt[r0:r0 + n].reshape(y_ref.shape[:1] + y_ref.shape[2:])
            r0 += n


def _s5_core(h, hbf, tables, dims, col_major):
    batch, lat_len, ctx_len, _ = dims
    wsr, wsi, wnr, wni, wor, woi, lam_r, lam_i = tables
    d = D_MODEL
    ncc, ncl = ctx_len // S5_CHUNK, lat_len // S5_CHUNK
    rows = batch * (ncc + ncl)
    gs = S5_SLAB_GROUPS
    if col_major:
        grid_rows = lat_len // GRID_W
        assert grid_rows % S5_CHUNK == 0
        col_chunks = grid_rows // S5_CHUNK
        nlat = batch * lat_len
        xs = [h[:nlat].reshape(batch * col_chunks, S5_CHUNK, GRID_W, d),
              h[nlat:].reshape(batch * ncc, S5_CHUNK, d)]
        blocks = [pl.BlockSpec((batch * col_chunks, S5_CHUNK, GRID_W, 128), lambda q: (0, 0, 0, q)),
                  pl.BlockSpec((batch * ncc, S5_CHUNK, 128), lambda q: (0, 0, q))]
        ins, in_blocks = xs, blocks
    else:
        col_chunks = 0
        xs = [h.reshape(rows, S5_CHUNK, d)]
        blocks = [pl.BlockSpec((rows, S5_CHUNK, 128), lambda q: (0, 0, q))]
        ins = [hbf.reshape(rows, S5_CHUNK * d)] * S5_CHUNK
        in_blocks = [pl.BlockSpec((rows, 128), lambda q, t=t: (0, t * (d // 128) + q)) for t in range(S5_CHUNK)]
    kern = functools.partial(_s5_kernel, nparts=len(xs), split_steps=not col_major, batch=batch, ncc=ncc, ncl=ncl,
                             col_chunks=col_chunks)
    wspec = lambda a, b: pl.BlockSpec((2, gs, a, b), lambda q: (0, q, 0, 0))
    ys = pl.pallas_call(
        kern,
        grid=(S5_GROUPS // gs,),
        in_specs=in_blocks + [pl.BlockSpec((S5_SLAB, S5_SLAB), lambda q: (0, 0)),
                           wspec(S5_ROW, S5_STATE), wspec(S5_ROW, S5_STATE),
                           wspec(S5_STATE, S5_ROW), wspec(S5_STATE, S5_ROW),
                           wspec(S5_STATE, S5_ROW), wspec(S5_STATE, S5_ROW),
                           pl.BlockSpec((2, gs, S5_STATE), lambda q: (0, q, 0)),
                           pl.BlockSpec((2, gs, S5_STATE), lambda q: (0, q, 0))],
        out_specs=blocks,
        out_shape=[jax.ShapeDtypeStruct(x.shape, F32) for x in xs],
        scratch_shapes=[pltpu.VMEM((gs * rows, S5_STATE), F32) for _ in range(4)]
        + [pltpu.VMEM((rows, S5_SLAB), F32)],
        compiler_params=_params("parallel"),
    )(*ins, _s5_perm(), wsr, wsi, wnr, wni, wor, woi, lam_r, lam_i)
    if col_major:
        return jnp.concatenate([ys[0].reshape(-1, d), ys[1].reshape(-1, d)], axis=0)
    return ys[0].reshape(-1, d)


def _s5_glu_kernel(h_ref, y_ref, dskip_ref, w1_ref, w2_ref, b1_ref, b2_ref, x_ref, gate_ref, o_ref, z_scr):
    @pl.when(pl.program_id(1) == 0)
    def _():
        y = h_ref[...] * dskip_ref[...] + y_ref[...]
        z = 0.5 * y * (1.0 + jnp.tanh(math.sqrt(2.0 / math.pi) * (y + 0.044715 * (y * y * y))))
        z_scr[...] = z.astype(BF16)

    z = z_scr[...]
    lin = _dot(z, w1_ref[...]) + b1_ref[...]
    gat = _dot(z, w2_ref[...]) + b2_ref[...]
    o_ref[...] = x_ref[...] + gate_ref[...] * (lin * _sigmoid(gat))


def _s5_glu(h, y, dskip, w, b, x, mods, layer, dims):
    batch, lat_len, _, tm = dims
    ntok, d = x.shape
    tn = 512
    nblk = d // tn
    tiles_per_seq = lat_len // tm
    b = b.reshape(1, 2 * d)
    return pl.pallas_call(
        _s5_glu_kernel,
        grid=(ntok // tm, nblk),
        in_specs=[pl.BlockSpec((tm, d), lambda i, j: (i, 0)),
                  pl.BlockSpec((tm, d), lambda i, j: (i, 0)),
                  pl.BlockSpec((1, d), lambda i, j: (0, 0)),
                  pl.BlockSpec((d, tn), lambda i, j: (0, j)),
                  pl.BlockSpec((d, tn), lambda i, j: (0, nblk + j)),
                  pl.BlockSpec((1, tn), lambda i, j: (0, j)),
                  pl.BlockSpec((1, tn), lambda i, j: (0, nblk + j)),
                  pl.BlockSpec((tm, tn), lambda i, j: (i, j)),
                  pl.BlockSpec((None, 1, tn), lambda i, j: (
                      layer * 8 + jnp.minimum(i // tiles_per_seq, batch), 0, 2 * nblk + j))],
        out_specs=pl.BlockSpec((tm, tn), lambda i, j: (i, j)),
        out_shape=jax.ShapeDtypeStruct((ntok, d), F32),
        scratch_shapes=[pltpu.VMEM((tm, d), BF16)],
        compiler_params=_params("parallel", "arbitrary"),
    )(h, y, dskip.reshape(1, d), w, w, b, b, x, mods)


def _s5_layer(x, mods, layer, j, dims, p):
    batch, lat_len, ctx_len, _ = dims
    col_major = (j % 2) == 1
    h, hbf = _prenorm(x, mods, p['norm_g'][layer, 0], layer, dims)
    tables = _s5_tables(p['s5_a_re'][j], p['s5_a_im'][j], p['s5_log_dt'][j], p['s5_b_re'][j],
                        p['s5_b_im'][j], p['s5_c_re'][j], p['s5_c_im'][j])
    y = _s5_core(h, hbf, tables, dims, col_major)
    return _s5_glu(h, y, p['s5_d'][j], p['s5_glu_w'][j].astype(BF16), p['s5_glu_b'][j], x, mods, layer, dims)


def _seq_blocks(dims):
    batch, lat_len, ctx_len, _ = dims
    return [(lat_len, 0), (ctx_len, batch * lat_len // ctx_len)]


def _dwconv3_kernel(u_ref, w_ref, b_ref, o_ref, *maybe_bf16_ref):
    x = u_ref[...]
    n = x.shape[0]
    row = lax.broadcasted_iota(jnp.int32, x.shape, 0)
    prev = jnp.where(row == 0, 0.0, pltpu.roll(x, 1, 0))
    nxt = jnp.where(row == n - 1, 0.0, pltpu.roll(x, n - 1, 0))
    y = prev * w_ref[0:1, :] + x * w_ref[1:2, :] + nxt * w_ref[2:3, :] + b_ref[...]
    o_ref[...] = y
    for r in maybe_bf16_ref:
        r[...] = y.astype(BF16)


def _dwconv3(u, w, b, dims, col0, ncols, with_bf16):
    batch = dims[0]
    ntok = u.shape[0]
    tc = 256
    cb0 = col0 // tc
    w8 = jnp.zeros((8, w.shape[1]), F32).at[:3].set(w)
    outs = []
    for n, off in _seq_blocks(dims):
        out_shape = [jax.ShapeDtypeStruct((batch * n, ncols), F32)]
        out_specs = [pl.BlockSpec((n, tc), lambda s, j: (s, j))]
        if with_bf16:
            out_shape.append(jax.ShapeDtypeStruct((batch * n, ncols), BF16))
            out_specs.append(pl.BlockSpec((n, tc), lambda s, j: (s, j)))
        outs.append(pl.pallas_call(
            _dwconv3_kernel,
            grid=(batch, ncols // tc),
            in_specs=[pl.BlockSpec((n, tc), lambda s, j, off=off: (off + s, cb0 + j)),
                      pl.BlockSpec((8, tc), lambda s, j: (0, cb0 + j)),
                      pl.BlockSpec((1, tc), lambda s, j: (0, cb0 + j))],
            out_specs=out_specs,
            out_shape=out_shape,
            compiler_params=_params("parallel", "parallel"),
        )(u, w8, b.reshape(1, -1)))
    return outs


def _phase_table_kernel(ar_ref, ai_ref, br_ref, bi_ref, c_ref, s_ref):
    br, bi = br_ref[...], bi_ref[...]
    for r in range(ar_ref.shape[0]):
        ar, ai = ar_ref[r:r + 1, :], ai_ref[r:r + 1, :]
        c_ref[r * 16:(r + 1) * 16, :] = (ar * br - ai * bi).astype(BF16)
        s_ref[r * 16:(r + 1) * 16, :] = (ai * br + ar * bi).astype(BF16)


def _phase_tables(coarse, fine, period):
    def unit(m):
        ang = (m % period).astype(F32) * (2.0 * math.pi / period)
        return jnp.cos(ang), jnp.sin(ang)

    ar, ai = unit(coarse)
    br, bi = unit(fine)
    ntab, r1, ncol = coarse.shape
    rb = min(32, r1)
    out = jax.ShapeDtypeStruct((ntab, r1 * 16, ncol), BF16)
    cspec = pl.BlockSpec((None, rb, ncol), lambda t, i: (t, i, 0))
    fspec = pl.BlockSpec((None, 16, ncol), lambda t, i: (t, 0, 0))
    ospec = pl.BlockSpec((None, rb * 16, ncol), lambda t, i: (t, i, 0))
    return pl.pallas_call(
        _phase_table_kernel,
        grid=(ntab, r1 // rb),
        in_specs=[cspec, cspec, fspec, fspec],
        out_specs=[ospec, ospec],
        out_shape=[out, out],
        compiler_params=_params("parallel", "parallel"),
    )(ar, ai, br, bi)


def _dft_tables(n):
    m = n // 2
    idx = jnp.arange(m, dtype=jnp.int32)[None, :]
    r1 = jnp.arange(m // 16, dtype=jnp.int32)[:, None]
    r0 = jnp.arange(16, dtype=jnp.int32)[:, None]
    par = jnp.arange(2, dtype=jnp.int32)[:, None, None]
    tau = 2 * idx[None] + par
    cmat, smat = _phase_tables(32 * r1[None] * tau, (2 * r0[None] + 1) * tau, 4 * n)
    odd = 2 * idx[None] + 1
    ctm, stm = _phase_tables(jnp.broadcast_to(32 * r1[None] * odd, (2,) + (m // 16, m)), (2 * r0[None] + par) * odd, 4 * n)
    return cmat, smat, ctm, stm


def _hy_filter_kernel(feat_ref, w1_ref, b1_ref, w2_ref, b2_ref, w3_ref, b3_ref, fq_ref, tu_ref, dl_ref,
                      w4_ref, hs_ref, hd_ref, hdn_scr, split_scr):
    @pl.when(pl.program_id(0) == 0)
    def _():
        fq = fq_ref[...]
        h = jnp.sin(fq * (jnp.dot(feat_ref[...], w1_ref[...], precision=HIGHEST, preferred_element_type=F32) + b1_ref[...]))
        h = jnp.sin(fq * (jnp.dot(h, w2_ref[...], precision=HIGHEST, preferred_element_type=F32) + b2_ref[...]))
        h = jnp.sin(fq * (jnp.dot(h, w3_ref[...], precision=HIGHEST, preferred_element_type=F32) + b3_ref[...]))
        hdn_scr[...] = h

    hdn = hdn_scr[...]
    decay = jnp.exp(-tu_ref[...] * dl_ref[...])
    row = lax.broadcasted_iota(jnp.int32, decay.shape, 0)
    m = decay.shape[0] // 2
    for order in range(HY_ORDER):
        fwd = jnp.dot(hdn, w4_ref[2 * order], precision=HIGHEST, preferred_element_type=F32) * decay
        bwd = jnp.dot(hdn, w4_ref[2 * order + 1], precision=HIGHEST, preferred_element_type=F32) * decay
        bwd = jnp.where(row == 0, 0.0, bwd)
        norm = jnp.sum(jnp.abs(fwd), axis=0, keepdims=True) + jnp.sum(jnp.abs(bwd), axis=0, keepdims=True)
        for out_ref, vals in ((hs_ref, (fwd + bwd) / norm), (hd_ref, (fwd - bwd) / norm)):
            split_scr[...] = vals
            for parity in range(2):
                out_ref[order, parity] = split_scr[pl.ds(parity, m, stride=2), :].astype(BF16)


def _hy_filters(n, p, j):
    d = D_MODEL
    t = jnp.arange(n, dtype=F32)
    t_unit = t / max(n - 1, 1)
    bands = jnp.linspace(1e-4, HY_BANDS - 1, HY_BANDS, dtype=F32)
    ang = (2.0 * math.pi / n) * t[:, None] * bands[None, :]
    feats = jnp.concatenate([t_unit[:, None], jnp.cos(ang), -jnp.sin(ang)], axis=-1)
    feats = jnp.pad(feats, ((0, 0), (0, HY_EMB_PAD - HY_EMB)))
    w1 = jnp.pad(p['hy_f_w1'][j], ((0, HY_EMB_PAD - HY_EMB), (0, 0)))
    deltas = jnp.abs(jnp.linspace(HY_MIN_DECAY, HY_MAX_DECAY, d, dtype=F32)).reshape(1, d)
    w4 = p['hy_f_w4'][j].reshape(HY_FILTER_HIDDEN, 2 * HY_ORDER, d).transpose(1, 0, 2)
    hid = HY_FILTER_HIDDEN
    tc = 128
    full = lambda shape: pl.BlockSpec(shape, lambda c: tuple(0 for _ in shape))
    row = lambda v: v.reshape(1, hid)
    return pl.pallas_call(
        _hy_filter_kernel,
        grid=(d // tc,),
        in_specs=[full((n, HY_EMB_PAD)), full((HY_EMB_PAD, hid)), full((1, hid)), full((hid, hid)), full((1, hid)),
                  full((hid, hid)), full((1, hid)), full((1, hid)), full((n, 1)),
                  pl.BlockSpec((1, tc), lambda c: (0, c)),
                  pl.BlockSpec((2 * HY_ORDER, hid, tc), lambda c: (0, 0, c))],
        out_specs=[pl.BlockSpec((HY_ORDER, 2, n // 2, tc), lambda c: (0, 0, 0, c)),
                   pl.BlockSpec((HY_ORDER, 2, n // 2, tc), lambda c: (0, 0, 0, c))],
        out_shape=[jax.ShapeDtypeStruct((HY_ORDER, 2, n // 2, d), BF16),
                   jax.ShapeDtypeStruct((HY_ORDER, 2, n // 2, d), BF16)],
        scratch_shapes=[pltpu.VMEM((n, hid), F32), pltpu.VMEM((n, tc), F32)],
        compiler_params=_params("arbitrary"),
    )(feats, w1, row(p['hy_f_b1'][j]), p['hy_f_w2'][j], row(p['hy_f_b2'][j]), p['hy_f_w3'][j],
      row(p['hy_f_b3'][j]), row(p['hy_f_freq'][j]), t_unit.reshape(n, 1), deltas, w4)


def _half_spectra(c_ref, s_ref, xe_c, xo_c, xe_s, xo_s):
    a, b = _dot(c_ref[0], xe_c), _dot(c_ref[1], xo_c)
    cs, ds = _dot(s_ref[0], xe_s), _dot(s_ref[1], xo_s)
    return a + b, a - b, cs + ds, ds - cs


def _dft_filter_kernel(c_ref, s_ref, hse_ref, hso_ref, hde_ref, hdo_ref, hr_ref, hi_ref):
    pc, qc, ps, qs = _half_spectra(c_ref, s_ref, hse_ref[...], hso_ref[...], hde_ref[...], hdo_ref[...])
    hr_ref[0], hr_ref[1] = pc, qc
    hi_ref[0], hi_ref[1] = -ps, -qs


def _dft_filter(cmat, smat, hs, hd, tk, td):
    order, _, m, d = hs.shape
    nd = d // td
    tab = pl.BlockSpec((2, tk, m), lambda i, o, j: (0, i, 0))
    even = pl.BlockSpec((None, None, m, td), lambda i, o, j: (o, 0, 0, j))
    odd = pl.BlockSpec((None, None, m, td), lambda i, o, j: (o, 1, 0, j))
    out = pl.BlockSpec((None, 2, tk, td), lambda i, o, j: (o, 0, i, j))
    sds = jax.ShapeDtypeStruct((order, 2, m, d), F32)
    return pl.pallas_call(
        _dft_filter_kernel,
        grid=(m // tk, order, nd),
        in_specs=[tab, tab, even, odd, even, odd],
        out_specs=[out, out],
        out_shape=[sds, sds],
        compiler_params=_params("parallel", "parallel", "parallel"),
    )(cmat, smat, hs, hs, hd, hd)


def _lane_blocks(nrows, width, row_block, col_block):
    ncol = width // 128
    return [pl.BlockSpec((nrows, 128), lambda *g, c=c: (row_block(*g), col_block(*g) * ncol + c))
            for c in range(ncol)]


def _parity_rows(refs, parity, count):
    cols = [r[pl.ds(parity, count, stride=2), :] for r in refs]
    return cols[0] if len(cols) == 1 else jnp.concatenate(cols, axis=1)


def _dft_fwd_kernel(c_ref, s_ref, *refs):
    z_refs = refs[:-4]
    hr_ref, hi_ref, ua_ref, ub_ref = refs[-4:]
    m = z_refs[0].shape[0] // 2
    ze = _parity_rows(z_refs, 0, m).astype(BF16)
    zo = _parity_rows(z_refs, 1, m).astype(BF16)
    pc, qc, ps, qs = _half_spectra(c_ref, s_ref, ze, zo, ze, zo)
    yrp = pc * hr_ref[0] + ps * hi_ref[0]
    yip = pc * hi_ref[0] - ps * hr_ref[0]
    yrq = qc * hr_ref[1] + qs * hi_ref[1]
    yiq = qc * hi_ref[1] - qs * hr_ref[1]
    ua_ref[0] = (yrp + yrq).astype(BF16)
    ub_ref[0] = (yip - yiq).astype(BF16)
    ua_ref[1] = (yrp - yrq).astype(BF16)
    ub_ref[1] = (yip + yiq).astype(BF16)


def _dft_fwd(cmat, smat, z, zcol, d, hr, hi, order, batch, tk, td):
    m = cmat.shape[1]
    tab = pl.BlockSpec((2, tk, m), lambda i, s, j: (0, i, 0), pipeline_mode=pl.Buffered(1))
    hspec = pl.BlockSpec((None, 2, tk, td), lambda i, s, j: (order, 0, i, j))
    out = pl.BlockSpec((2, None, tk, td), lambda i, s, j: (0, s, i, j))
    sds = jax.ShapeDtypeStruct((2, batch, m, d), BF16)
    return pl.pallas_call(
        _dft_fwd_kernel,
        grid=(m // tk, batch, d // td),
        in_specs=[tab, tab] + _lane_blocks(2 * m, td, lambda i, s, j: s, lambda i, s, j: zcol // td + j)
        + [hspec, hspec],
        out_specs=[out, out],
        out_shape=[sds, sds],
        compiler_params=_params("parallel", "parallel", "parallel"),
    )(cmat, smat, *([z] * (td // 128)), hr, hi)


def _dft_inv_kernel(ct_ref, st_ref, ua_ref, ub_ref, *refs, inv_n):
    ncol = (len(refs) - 2) // 3
    xg_refs, zp_refs, bias_ref, o_ref = refs[:ncol], refs[ncol:2 * ncol], refs[2 * ncol], refs[2 * ncol + 1]
    mix_scrs = refs[2 * ncol + 2:]
    ts = ct_ref.shape[1]
    bias = bias_ref[...]
    for parity in range(2):
        y = (_dot(ct_ref[parity], ua_ref[parity]) - _dot(st_ref[parity], ub_ref[parity])) * inv_n
        out = _parity_rows(xg_refs, parity, ts) * (y + _parity_rows(zp_refs, parity, ts) * bias)
        for c, scr in enumerate(mix_scrs):
            scr[pl.ds(parity, ts, stride=2), :] = out[:, c * 128:(c + 1) * 128]
    for c, scr in enumerate(mix_scrs):
        o_ref[:, c * 128:(c + 1) * 128] = scr[...]


def _dft_inv(ctm, stm, ua, ub, xg, xg_col0, zprev, zcol, bias, batch, ts, td):
    m = ctm.shape[1]
    d = ua.shape[3]
    ncol = td // 128
    tab = pl.BlockSpec((2, ts, m), lambda i, s, j: (0, i, 0))
    spec = pl.BlockSpec((2, None, m, td), lambda i, s, j: (0, s, 0, j))
    rblk = lambda i, s, j: s * (m // ts) + i
    rows = lambda col0: _lane_blocks(2 * ts, td, rblk, lambda i, s, j: col0 // td + j)
    kern = functools.partial(_dft_inv_kernel, inv_n=1.0 / (2 * m))
    return pl.pallas_call(
        kern,
        grid=(m // ts, batch, d // td),
        in_specs=[tab, tab, spec, spec] + rows(xg_col0) + rows(zcol) + [pl.BlockSpec((1, td), lambda i, s, j: (0, j))],
        out_specs=pl.BlockSpec((2 * ts, td), lambda i, s, j: (rblk(i, s, j), j)),
        out_shape=jax.ShapeDtypeStruct((zprev.shape[0], d), F32),
        scratch_shapes=[pltpu.VMEM((2 * ts, 128), F32) for _ in range(ncol)],
        compiler_params=_params("parallel", "parallel", "parallel"),
    )(ctm, stm, ua, ub, *([xg] * ncol), *([zprev] * ncol), bias.reshape(1, d))


def _hyena_layer(x, mods, layer, j, dims, p):
    batch, lat_len, ctx_len, _ = dims
    d = D_MODEL
    u = _mm_norm(x, mods, p['norm_g'][layer, 0], p['hy_in_w'], j, 0, 3 * d, p['hy_in_b'][j], layer, dims, 512)
    parts = _dwconv3(u, p['hy_conv_w'][j], p['hy_conv_b'][j], dims, 0, 3 * d, False)
    z_out = []
    for (n, _), (conv,) in zip(_seq_blocks(dims), parts):
        m = n // 2
        cmat, smat, ctm, stm = _dft_tables(n)
        hs, hd = _hy_filters(n, p, j)
        hr, hi = _dft_filter(cmat, smat, hs, hd, min(512, m), 512)
        zprev, zcol = conv, 0
        for order in range(HY_ORDER):
            ua, ub = _dft_fwd(cmat, smat, zprev, zcol, d, hr, hi, order, batch, min(1024, m), 256)
            zprev = _dft_inv(ctm, stm, ua, ub, conv, (1 + order) * d, zprev, zcol, p['hy_bias'][j, order],
                             batch, min(512, m), 256)
            zcol = 0
        z_out.append(zprev)
    z = jnp.concatenate(z_out, axis=0)
    return _mm_res(z, p['hy_out_w'][j].astype(BF16), p['hy_out_b'][j], x, mods, layer, dims, 512)


def _gdn_conv_kernel(p_ref, w_ref, o_ref):
    x = p_ref[...]
    n = x.shape[0]
    row = lax.broadcasted_iota(jnp.int32, x.shape, 0)
    acc = x * w_ref[2:3, :]
    for s in (1, 2):
        prev = jnp.where(row < s, 0.0, pltpu.roll(x, s, 0))
        nxt = jnp.where(row >= n - s, 0.0, pltpu.roll(x, n - s, 0))
        acc = acc + prev * w_ref[2 - s:3 - s, :] + nxt * w_ref[2 + s:3 + s, :]
    y = _silu(acc)
    head = pl.program_id(1)
    inv = lax.rsqrt(jnp.sum(y * y, axis=-1, keepdims=True) + 1e-6)
    inv = inv * jnp.where(head < GDN_K_HEADS, GDN_HEAD ** -0.5, 1.0)
    o_ref[...] = y * jnp.where(head < 2 * GDN_K_HEADS, inv, 1.0)


def _gdn_conv(proj, w, dims):
    batch = dims[0]
    ntok = proj.shape[0]
    nheads = GDN_CONV_DIM // GDN_HEAD
    w8 = jnp.zeros((8, GDN_CONV_DIM), F32).at[:5].set(w)
    out = None
    for n, off in _seq_blocks(dims):
        args = [proj, w8]
        in_specs = [pl.BlockSpec((n, GDN_HEAD), lambda s, h, off=off: (off + s, h)),
                    pl.BlockSpec((8, GDN_HEAD), lambda s, h: (0, h))]
        aliases = {}
        if out is not None:
            args.append(out)
            in_specs.append(pl.BlockSpec(memory_space=pl.ANY))
            aliases = {2: 0}
        kern = _gdn_conv_kernel if out is None else (lambda p_ref, w_ref, _, o_ref: _gdn_conv_kernel(p_ref, w_ref, o_ref))
        out = pl.pallas_call(
            kern,
            grid=(batch, nheads),
            in_specs=in_specs,
            out_specs=pl.BlockSpec((None, n, GDN_HEAD), lambda s, h, off=off: (h, off + s, 0)),
            out_shape=jax.ShapeDtypeStruct((nheads, ntok, GDN_HEAD), F32),
            input_output_aliases=aliases,
            compiler_params=_params("parallel", "parallel"),
        )(*args)
    return out


def _gdn_gate_kernel(ab_ref, alog_ref, dtb_ref, o_ref):
    c = GDN_CHUNK
    row = lax.broadcasted_iota(jnp.int32, (c, c), 0)
    col = lax.broadcasted_iota(jnp.int32, (c, c), 1)
    lower = jnp.where(row >= col, 1.0, 0.0).astype(F32)
    upper = jnp.where(row <= col, 1.0, 0.0).astype(F32)
    lane = lax.broadcasted_iota(jnp.int32, (c, 128), 1)
    for r in range(ab_ref.shape[0] // c):
        ab = ab_ref[r * c:(r + 1) * c, :]
        xs = ab + dtb_ref[...]
        softplus = jnp.maximum(xs, 0.0) + jnp.log(1.0 + jnp.exp(-jnp.abs(xs)))
        g = -jnp.exp(alog_ref[...]) * softplus
        cum_f = jnp.dot(lower, g, precision=HIGHEST, preferred_element_type=F32)
        cum_b = jnp.dot(upper, g, precision=HIGHEST, preferred_element_type=F32)
        o_ref[r * c:(r + 1) * c, :] = jnp.where(lane < GDN_V_HEADS, cum_f,
                                                jnp.where(lane < 2 * GDN_V_HEADS, cum_b, _sigmoid(ab)))


def _gdn_gates(ab, a_log, dt_bias, tm):
    ntok = ab.shape[0]
    pad = jnp.zeros((2 * GDN_V_HEADS,), F32)
    alog = jnp.concatenate([a_log.reshape(-1), pad]).reshape(1, 128)
    dtb = jnp.concatenate([dt_bias.reshape(-1), pad]).reshape(1, 128)
    return pl.pallas_call(
        _gdn_gate_kernel,
        grid=(ntok // tm,),
        in_specs=[pl.BlockSpec((tm, 128), lambda i: (i, 0)),
                  pl.BlockSpec((1, 128), lambda i: (0, 0)),
                  pl.BlockSpec((1, 128), lambda i: (0, 0))],
        out_specs=pl.BlockSpec((tm, 128), lambda i: (i, 0)),
        out_shape=jax.ShapeDtypeStruct((ntok, 128), F32),
        compiler_params=_params("parallel"),
    )(ab, alog, dtb)


def _unit_triangular_inverses(mats):
    c = mats[0].shape[0]
    row = lax.broadcasted_iota(jnp.int32, (c, c), 0)
    col = lax.broadcasted_iota(jnp.int32, (c, c), 1)
    eye = jnp.where(row == col, 1.0, 0.0).astype(F32)
    ts = [eye - a for a in mats]
    pws = list(mats)
    for _ in range(int(math.log2(c)) - 1):
        pwbs = [pw.astype(BF16) for pw in pws]
        pws = [_dot(pwb, pwb) for pwb in pwbs]
        ts = [t + _dot(t.astype(BF16), pw.astype(BF16)) for t, pw in zip(ts, pws)]
    ms = [eye + a for a in mats]
    m_his = [m.astype(BF16) for m in ms]
    m_los = [(m - m_hi.astype(F32)).astype(BF16) for m, m_hi in zip(ms, m_his)]
    t_his = [t.astype(BF16) for t in ts]
    t_los = [(t - t_hi.astype(F32)).astype(BF16) for t, t_hi in zip(ts, t_his)]
    resids = [eye - (_dot(m_hi, t_hi) + (_dot(m_hi, t_lo) + _dot(m_lo, t_hi)))
              for m_hi, m_lo, t_hi, t_lo in zip(m_his, m_los, t_his, t_los)]
    return [t + _dot(t_hi, r.astype(BF16)) for t, t_hi, r in zip(ts, t_his, resids)]


def _gdn_chunk_kernel(qf_ref, kf_ref, vf_ref, gcf_ref, grf_ref, qb_ref, kb_ref, vb_ref, gcb_ref, grb_ref,
                      of_ref, ob_ref, state_scr, *, nchunks):
    c = GDN_CHUNK

    @pl.when(pl.program_id(2) == 0)
    def _():
        state_scr[...] = jnp.zeros_like(state_scr)

    row = lax.broadcasted_iota(jnp.int32, (c, c), 0)
    col = lax.broadcasted_iota(jnp.int32, (c, c), 1)
    blocks = ((qf_ref, kf_ref, vf_ref, gcf_ref, grf_ref, of_ref), (qb_ref, kb_ref, vb_ref, gcb_ref, grb_ref, ob_ref))
    nkh = qf_ref.shape[0]
    keys, amats, part = [], [], {}
    for direction, (q_ref, k_ref, v_ref, gc_ref, gr_ref, o_ref) in enumerate(blocks):
        if direction == 0:
            incl, strict, last = row >= col, row > col, c - 1
        else:
            incl, strict, last = row <= col, row < col, 0
        for kh in range(nkh):
            for ci in range(nchunks):
                rows = slice(ci * c, (ci + 1) * c)
                q = q_ref[kh, rows, :].astype(BF16)
                k32 = k_ref[kh, rows, :]
                k = k32.astype(BF16)
                k_t = k32.T
                gates_c = gc_ref[kh, ci]
                gates_r = gr_ref[kh, ci]
                kk = _dot_nt(k, k)
                qk = _dot_nt(q, k)
                for e in range(2):
                    ch_g, ch_b = direction * 2 + e, 4 + direction * 2 + e
                    gcc, gcr = gates_c[:, ch_g:ch_g + 1], gates_r[ch_g:ch_g + 1, :]
                    beta_c, beta_r = gates_c[:, ch_b:ch_b + 1], gates_r[ch_b:ch_b + 1, :]
                    decay = jnp.where(incl, jnp.exp(jnp.where(incl, gcc - gcr, 0.0)), 0.0)
                    key = (direction, kh, ci, e)
                    keys.append(key)
                    amats.append(jnp.where(strict, kk * beta_c * decay, 0.0))
                    g_last = gcr[:, last:last + 1]
                    part[key] = dict(
                        q=q, k=k, v=v_ref[2 * kh + e, rows, :].astype(BF16), attn=(qk * decay).astype(BF16),
                        beta_r=beta_r, wscale=beta_r * jnp.exp(gcr), egc=jnp.exp(gcc),
                        kg_t=(k_t * jnp.exp(g_last - gcr)).astype(BF16), e_last=jnp.exp(g_last))
    tmats = _unit_triangular_inverses(amats)
    local = {}
    for key, t in zip(keys, tmats):
        p = part[key]
        u = _dot((t * p['beta_r']).astype(BF16), p['v'])
        w = _dot((t * p['wscale']).astype(BF16), p['k']).astype(BF16)
        local[key] = (p['q'], u, w, p['attn'], p['egc'], p['kg_t'], p['e_last'])

    nslots = state_scr.shape[0]
    states = [state_scr[slot] for slot in range(nslots)]
    streams = [(direction, kh, e) for direction in range(2) for kh in range(nkh) for e in range(2)]
    out_refs = (of_ref, ob_ref)
    for step in range(nchunks):
        cur = [local[d, kh, (step if d == 0 else nchunks - 1 - step), e] for d, kh, e in streams]
        state_bs = [s.astype(BF16) for s in states]
        ws = [_dot(p[2], sb) for p, sb in zip(cur, state_bs)]
        qs = [_dot(p[0], sb) for p, sb in zip(cur, state_bs)]
        v_news = [(p[1] - w).astype(BF16) for p, w in zip(cur, ws)]
        intra = [_dot(p[3], vn) for p, vn in zip(cur, v_news)]
        upd = [_dot(p[5], vn) for p, vn in zip(cur, v_news)]
        for (d, kh, e), p, q_s, o_in in zip(streams, cur, qs, intra):
            ci = step if d == 0 else nchunks - 1 - step
            out_refs[d][2 * kh + e, ci * c:(ci + 1) * c, :] = p[4] * q_s + o_in
        states = [s * p[6] + dlt for s, p, dlt in zip(states, cur, upd)]
    for slot in range(nslots):
        state_scr[slot] = states[slot]


def _gdn_chunks(qkvh, gates, dims):
    batch, lat_len, ctx_len, _ = dims
    c = GDN_CHUNK
    ntok = qkvh.shape[1]
    per_step = min(4, ctx_len // c)
    rows = per_step * c
    nbc, nbl = ctx_len // rows, lat_len // rows
    ctx0 = batch * lat_len // rows
    gk = gates.reshape(ntok // c, c, 4, GDN_K_HEADS, 2).transpose(3, 0, 1, 2, 4).reshape(GDN_K_HEADS, ntok // c, c, 8)
    gk_t = gk.transpose(0, 1, 3, 2)

    def fwd_blk(b, j):
        return jnp.where(j < nbc, ctx0 + b * nbc + j, b * nbl + j - nbc)

    def bwd_blk(b, j):
        return jnp.where(j < nbc, ctx0 + b * nbc + nbc - 1 - j, b * nbl + nbl - 1 - (j - nbc))

    nkh = GDN_KHEADS_PER_STEP
    hblocks = GDN_K_HEADS // nkh

    def specs(blk):
        return [pl.BlockSpec((nkh, rows, GDN_HEAD), lambda b, h, j: (h, blk(b, j), 0)),
                pl.BlockSpec((nkh, rows, GDN_HEAD), lambda b, h, j: (hblocks + h, blk(b, j), 0)),
                pl.BlockSpec((2 * nkh, rows, GDN_HEAD), lambda b, h, j: (hblocks + h, blk(b, j), 0)),
                pl.BlockSpec((nkh, per_step, c, 8), lambda b, h, j: (h, blk(b, j), 0, 0)),
                pl.BlockSpec((nkh, per_step, 8, c), lambda b, h, j: (h, blk(b, j), 0, 0))]

    out_sd = jax.ShapeDtypeStruct((GDN_V_HEADS, ntok, GDN_HEAD), F32)
    return pl.pallas_call(
        functools.partial(_gdn_chunk_kernel, nchunks=per_step),
        grid=(batch, hblocks, nbc + nbl),
        in_specs=specs(fwd_blk) + specs(bwd_blk),
        out_specs=[pl.BlockSpec((2 * nkh, rows, GDN_HEAD), lambda b, h, j: (h, fwd_blk(b, j), 0)),
                   pl.BlockSpec((2 * nkh, rows, GDN_HEAD), lambda b, h, j: (h, bwd_blk(b, j), 0))],
        out_shape=[out_sd, out_sd],
        scratch_shapes=[pltpu.VMEM((4 * nkh, GDN_HEAD, GDN_HEAD), F32)],
        compiler_params=_params("parallel", "parallel", "arbitrary"),
    )(qkvh, qkvh, qkvh, gk, gk_t, qkvh, qkvh, qkvh, gk, gk_t)


def _gdn_gated_norm_kernel(of_ref, ob_ref, z_ref, ng_ref, a_ref):
    ng = ng_ref[...]
    for h in range(of_ref.shape[0]):
        o = of_ref[h] + ob_ref[h]
        o = o * lax.rsqrt(jnp.mean(o * o, axis=-1, keepdims=True) + EPS)
        z = z_ref[:, h * GDN_HEAD:(h + 1) * GDN_HEAD]
        a_ref[:, h * GDN_HEAD:(h + 1) * GDN_HEAD] = (o * ng * _silu(z)).astype(BF16)


def _gdn_gated_norm(o_f, o_b, proj, norm_g, dims):
    tm = dims[3]
    ntok = proj.shape[0]
    hb = 8
    zblk0 = GDN_CONV_DIM // (hb * GDN_HEAD)
    return pl.pallas_call(
        _gdn_gated_norm_kernel,
        grid=(ntok // tm, GDN_V_HEADS // hb),
        in_specs=[pl.BlockSpec((hb, tm, GDN_HEAD), lambda i, h: (h, i, 0)),
                  pl.BlockSpec((hb, tm, GDN_HEAD), lambda i, h: (h, i, 0)),
                  pl.BlockSpec((tm, hb * GDN_HEAD), lambda i, h: (i, zblk0 + h)),
                  pl.BlockSpec((1, GDN_HEAD), lambda i, h: (0, 0))],
        out_specs=pl.BlockSpec((tm, hb * GDN_HEAD), lambda i, h: (i, h)),
        out_shape=jax.ShapeDtypeStruct((ntok, GDN_V), BF16),
        compiler_params=_params("parallel", "parallel"),
    )(o_f, o_b, proj, norm_g.reshape(1, GDN_HEAD))


def _gdn_layer(x, mods, layer, j, dims, p):
    in_w = p['gdn_in_w']
    g = p['norm_g'][layer, 0]
    proj = _mm_norm(x, mods, g, in_w, j, 0, GDN_MAIN, jnp.zeros((GDN_MAIN,), F32), layer, dims, 512)
    ab = _mm_norm(x, mods, g, in_w, j, GDN_MAIN, 128, jnp.zeros((128,), F32), layer, dims, 128)
    qkvh = _gdn_conv(proj, p['gdn_conv_w'][j], dims)
    gates = _gdn_gates(ab, p['gdn_a_log'][j], p['gdn_dt_bias'][j], dims[3])
    o_f, o_b = _gdn_chunks(qkvh, gates, dims)
    a = _gdn_gated_norm(o_f, o_b, proj, p['gdn_norm_g'][j], dims)
    return _mm_res(a, p['gdn_out_w'][j].astype(BF16), jnp.zeros((D_MODEL,), F32), x, mods, layer, dims, 512)


def kernel(x, c, ctx, c_ctx, ada_w, ada_b, norm_g, final_g, ffn_w_gate, ffn_w_up, ffn_w_down, s5_a_re, s5_a_im, s5_log_dt, s5_b_re, s5_b_im, s5_c_re, s5_c_im, s5_d, s5_glu_w, s5_glu_b, hy_in_w, hy_in_b, hy_conv_w, hy_conv_b, hy_f_w1, hy_f_b1, hy_f_w2, hy_f_b2, hy_f_w3, hy_f_b3, hy_f_w4, hy_f_freq, hy_bias, hy_out_w, hy_out_b, gdn_in_w, gdn_conv_w, gdn_a_log, gdn_dt_bias, gdn_norm_g, gdn_out_w):
    p = dict(locals())
    batch, lat_len, d = x.shape
    ctx_len = ctx.shape[1]
    depth = ada_w.shape[0]
    tm = min(512, batch * ctx_len)
    dims = (batch, lat_len, ctx_len, tm)
    assert d == D_MODEL and lat_len % tm == 0 and (batch * ctx_len) % tm == 0 and batch + 1 <= 8

    cvec = jnp.zeros((8, d), F32).at[:batch].set(c).at[batch].set(c_ctx)
    mods = _ada_all(cvec, ada_w, ada_b).reshape(depth * 8, 1, 6 * d)
    tok = jnp.concatenate([x.reshape(batch * lat_len, d), ctx.reshape(batch * ctx_len, d)], axis=0)
    ffn_wg, ffn_wu, ffn_wd = ffn_w_gate.astype(BF16), ffn_w_up.astype(BF16), ffn_w_down.astype(BF16)
    for i in range(depth):
        kind, j = i % N_MIXERS, i // N_MIXERS
        if kind == 0:
            tok = _s5_layer(tok, mods, i, j, dims, p)
        elif kind == 1:
            tok = _hyena_layer(tok, mods, i, j, dims, p)
        else:
            tok = _gdn_layer(tok, mods, i, j, dims, p)
        tok = _ffn(tok, mods, norm_g[i, 1], ffn_wg, ffn_wu, ffn_wd, i, dims, with_ctx=i < depth - 1)
    out = _final_norm(tok, final_g, batch * lat_len, tm)
    return out.reshape(batch, lat_len, d)
```

```python
import functools
import math

import jax
import jax.numpy as jnp
from jax import lax
from jax.experimental import pallas as pl
from jax.experimental.pallas import tpu as pltpu

F32 = jnp.float32
BF16 = jnp.bfloat16
HIGHEST = lax.Precision.HIGHEST

D_MODEL = 2048
GRID_W = 64
EPS = 1e-6
N_MIXERS = 3

S5_GROUP = 16
S5_STATE = 64
S5_GROUPS = D_MODEL // S5_GROUP
S5_CHUNK = 16
S5_ROW = S5_CHUNK * S5_GROUP
S5_SLAB_GROUPS = 128 // S5_GROUP
S5_SLAB = S5_SLAB_GROUPS * S5_ROW

HY_ORDER = 2
HY_BANDS = 16
HY_EMB = 1 + 2 * HY_BANDS
HY_EMB_PAD = 128
HY_FILTER_HIDDEN = 64
HY_DECAY_TARGET = 1e-2
HY_MAX_DECAY = math.log(HY_DECAY_TARGET) / 0.3
HY_MIN_DECAY = math.log(HY_DECAY_TARGET) / 1.5

GDN_K_HEADS = 16
GDN_V_HEADS = 32
GDN_HEAD = 128
GDN_QK = GDN_K_HEADS * GDN_HEAD
GDN_V = GDN_V_HEADS * GDN_HEAD
GDN_CONV_DIM = 2 * GDN_QK + GDN_V
GDN_MAIN = GDN_CONV_DIM + GDN_V
GDN_CHUNK = 64
GDN_KHEADS_PER_STEP = 2

VMEM_LIMIT_BYTES = 56 * 1024 * 1024


def _params(*sem):
    return pltpu.CompilerParams(dimension_semantics=sem, vmem_limit_bytes=VMEM_LIMIT_BYTES)


def _dot(a, b):
    return jnp.dot(a, b, preferred_element_type=F32)


def _dot_nt(a, b):
    return lax.dot_general(a, b, (((1,), (1,)), ((), ())), preferred_element_type=F32)


def _dot_tn(a, b):
    return lax.dot_general(a, b, (((0,), (0,)), ((), ())), preferred_element_type=F32)


def _sigmoid(x):
    return 1.0 / (1.0 + jnp.exp(-x))


def _silu(x):
    return x * _sigmoid(x)


def _norm_mod(x, g, sc, sh):
    y = x * lax.rsqrt(jnp.mean(x * x, axis=-1, keepdims=True) + EPS) * g
    return y * (1.0 + sc) + sh


def _mod_spec(layer, tm, lat_len, batch, width, col_block):
    tiles_per_seq = lat_len // tm

    def index(i, *_):
        return (layer * 8 + jnp.minimum(i // tiles_per_seq, batch), 0, col_block)

    return pl.BlockSpec((None, 1, width), index)


def _ada_kernel(c_ref, w_ref, b_ref, o_ref):
    c = c_ref[...]
    s = _silu(c).astype(BF16)
    o_ref[...] = _dot(s, w_ref[...].astype(BF16)) + b_ref[...]


def _ada_all(cvec, ada_w, ada_b):
    depth, d, n6 = ada_w.shape
    tn = 1024
    return pl.pallas_call(
        _ada_kernel,
        grid=(depth, n6 // tn),
        in_specs=[pl.BlockSpec((8, d), lambda l, j: (0, 0)),
                  pl.BlockSpec((None, d, tn), lambda l, j: (l, 0, j)),
                  pl.BlockSpec((None, 1, tn), lambda l, j: (l, 0, j))],
        out_specs=pl.BlockSpec((None, 8, tn), lambda l, j: (l, 0, j)),
        out_shape=jax.ShapeDtypeStruct((depth, 8, n6), F32),
        compiler_params=_params("parallel", "parallel"),
    )(cvec, ada_w, ada_b.reshape(depth, 1, n6))


def _prenorm_kernel(x_ref, mod_ref, g_ref, o_ref, obf_ref):
    d = D_MODEL
    h = _norm_mod(x_ref[...], g_ref[...], mod_ref[:, d:2 * d], mod_ref[:, 0:d])
    o_ref[...] = h
    obf_ref[...] = h.astype(BF16)


def _prenorm(x, mods, g, layer, dims):
    batch, lat_len, _, tm = dims
    ntok, d = x.shape
    return pl.pallas_call(
        _prenorm_kernel,
        grid=(ntok // tm,),
        in_specs=[pl.BlockSpec((tm, d), lambda i: (i, 0)),
                  _mod_spec(layer, tm, lat_len, batch, 6 * d, 0),
                  pl.BlockSpec((1, d), lambda i: (0, 0))],
        out_specs=[pl.BlockSpec((tm, d), lambda i: (i, 0)), pl.BlockSpec((tm, d), lambda i: (i, 0))],
        out_shape=[jax.ShapeDtypeStruct((ntok, d), F32), jax.ShapeDtypeStruct((ntok, d), BF16)],
        compiler_params=_params("parallel"),
    )(x, mods, g.reshape(1, d))


def _mm_norm_kernel(x_ref, mod_ref, g_ref, w_ref, b_ref, *rest):
    o_ref, h_scr = rest[-2:]
    d = D_MODEL

    @pl.when(pl.program_id(1) == 0)
    def _():
        h_scr[...] = _norm_mod(x_ref[...], g_ref[...], mod_ref[:, d:2 * d], mod_ref[:, 0:d]).astype(BF16)

    o_ref[...] = _dot(h_scr[...], w_ref[...].astype(BF16)) + b_ref[...]


def _mm_norm(x, mods, g, w_all, widx, col0, n, b, layer, dims, tn):
    batch, lat_len, ctx_len, _ = dims
    ntok, d = x.shape
    out = None
    for tm, row0, nrows in ((min(1024, lat_len), 0, batch * lat_len),
                            (batch * ctx_len, batch * lat_len, batch * ctx_len)):
        blk0 = row0 // tm
        tiles_per_seq = lat_len // tm
        mod_row = (lambda i: i // tiles_per_seq) if row0 == 0 else (lambda i: batch)
        args = [x, mods, g.reshape(1, d), w_all, b.reshape(1, n)]
        in_specs = [pl.BlockSpec((tm, d), lambda i, j: (blk0 + i, 0), pipeline_mode=pl.Buffered(1)),
                    pl.BlockSpec((None, 1, 6 * d), lambda i, j: (layer * 8 + mod_row(i), 0, 0)),
                    pl.BlockSpec((1, d), lambda i, j: (0, 0)),
                    pl.BlockSpec((None, d, tn), lambda i, j: (widx, 0, col0 // tn + j)),
                    pl.BlockSpec((1, tn), lambda i, j: (0, j))]
        aliases = {}
        if out is not None:
            args.append(out)
            in_specs.append(pl.BlockSpec(memory_space=pl.ANY))
            aliases = {5: 0}
        out = pl.pallas_call(
            _mm_norm_kernel,
            grid=(nrows // tm, n // tn),
            in_specs=in_specs,
            out_specs=pl.BlockSpec((tm, tn), lambda i, j: (blk0 + i, j)),
            out_shape=jax.ShapeDtypeStruct((ntok, n), F32),
            scratch_shapes=[pltpu.VMEM((tm, d), BF16)],
            input_output_aliases=aliases,
            compiler_params=_params("parallel", "arbitrary"),
        )(*args)
    return out


def _mm_res_kernel(a_ref, w_ref, b_ref, x_ref, gate_ref, o_ref):
    y = _dot(a_ref[...].astype(BF16), w_ref[...]) + b_ref[...]
    o_ref[...] = x_ref[...] + gate_ref[...] * y


def _mm_res(a, w, b, x, mods, layer, dims, tn):
    batch, lat_len, _, tm = dims
    ntok, d = x.shape
    k = a.shape[1]
    gate_blk = 2 * d // tn
    tiles_per_seq = lat_len // tm
    return pl.pallas_call(
        _mm_res_kernel,
        grid=(ntok // tm, d // tn),
        in_specs=[pl.BlockSpec((tm, k), lambda i, j: (i, 0)),
                  pl.BlockSpec((k, tn), lambda i, j: (0, j)),
                  pl.BlockSpec((1, tn), lambda i, j: (0, j)),
                  pl.BlockSpec((tm, tn), lambda i, j: (i, j)),
                  pl.BlockSpec((None, 1, tn), lambda i, j: (
                      layer * 8 + jnp.minimum(i // tiles_per_seq, batch), 0, gate_blk + j))],
        out_specs=pl.BlockSpec((tm, tn), lambda i, j: (i, j)),
        out_shape=jax.ShapeDtypeStruct((ntok, d), F32),
        compiler_params=_params("parallel", "parallel"),
    )(a, w, b.reshape(1, d), x, mods)


def _ffn_kernel(x_ref, mod_ref, g_ref, wg_ref, wu_ref, wd_ref, o_ref, h_scr, acc_scr):
    d = D_MODEL
    f = pl.program_id(1)

    @pl.when(f == 0)
    def _():
        h_scr[...] = _norm_mod(x_ref[...], g_ref[...], mod_ref[:, 4 * d:5 * d],
                               mod_ref[:, 3 * d:4 * d]).astype(BF16)
        acc_scr[...] = jnp.zeros_like(acc_scr)

    h = h_scr[...]
    gate = _dot(h, wg_ref[...])
    up = _dot(h, wu_ref[...])
    act = (_silu(gate) * up).astype(BF16)
    acc_scr[...] += _dot(act, wd_ref[...])

    @pl.when(f == pl.num_programs(1) - 1)
    def _():
        o_ref[...] = x_ref[...] + mod_ref[:, 5 * d:6 * d] * acc_scr[...]


def _ffn(x, mods, g, wg, wu, wd, layer, dims, with_ctx):
    batch, lat_len, _, tm = dims
    ntok, d = x.shape
    dff = wg.shape[2]
    tf = 512
    nrows = ntok if with_ctx else batch * lat_len
    return pl.pallas_call(
        _ffn_kernel,
        grid=(nrows // tm, dff // tf),
        in_specs=[pl.BlockSpec((tm, d), lambda i, f: (i, 0)),
                  _mod_spec(layer, tm, lat_len, batch, 6 * d, 0),
                  pl.BlockSpec((1, d), lambda i, f: (0, 0)),
                  pl.BlockSpec((None, d, tf), lambda i, f: (layer, 0, f)),
                  pl.BlockSpec((None, d, tf), lambda i, f: (layer, 0, f)),
                  pl.BlockSpec((None, tf, d), lambda i, f: (layer, f, 0))],
        out_specs=pl.BlockSpec((tm, d), lambda i, f: (i, 0)),
        out_shape=jax.ShapeDtypeStruct((ntok, d), F32),
        scratch_shapes=[pltpu.VMEM((tm, d), BF16), pltpu.VMEM((tm, d), F32)],
        compiler_params=_params("parallel", "arbitrary"),
    )(x, mods, g.reshape(1, d), wg, wu, wd)


def _final_norm_kernel(x_ref, g_ref, o_ref):
    x = x_ref[...]
    o_ref[...] = x * lax.rsqrt(jnp.mean(x * x, axis=-1, keepdims=True) + EPS) * g_ref[...]


def _final_norm(x, g, nrows, tm):
    d = x.shape[1]
    return pl.pallas_call(
        _final_norm_kernel,
        grid=(nrows // tm,),
        in_specs=[pl.BlockSpec((tm, d), lambda i: (i, 0)), pl.BlockSpec((1, d), lambda i: (0, 0))],
        out_specs=pl.BlockSpec((tm, d), lambda i: (i, 0)),
        out_shape=jax.ShapeDtypeStruct((nrows, d), F32),
        compiler_params=_params("parallel"),
    )(x, g.reshape(1, d))


def _s5_tables(a_re, a_im, log_dt, b_re, b_im, c_re, c_im):
    t_len = S5_CHUNK
    dt = jnp.exp(log_dt)[..., None]
    lr, li = a_re * dt, a_im * dt
    er = jnp.exp(lr)
    nr, ni = er * jnp.cos(li) - 1.0, er * jnp.sin(li)
    den = a_re * a_re + a_im * a_im
    qr, qi = (nr * a_re + ni * a_im) / den, (ni * a_re - nr * a_im) / den
    bbr = qr[..., None] * b_re - qi[..., None] * b_im
    bbi = qr[..., None] * b_im + qi[..., None] * b_re
    ngroups = a_re.shape[1]
    ti = jnp.arange(t_len, dtype=F32)

    def powers(exp_fwd, exp_bwd):
        e = jnp.stack([exp_fwd, exp_bwd], axis=1)[:, :, None, None]
        mag = jnp.exp(lr[None] * e)
        return mag * jnp.cos(li[None] * e), mag * jnp.sin(li[None] * e)

    def times_c(pr, pi):
        cr = c_re[None] * pr[:, :, :, None, :] - c_im[None] * pi[:, :, :, None, :]
        ci = c_re[None] * pi[:, :, :, None, :] + c_im[None] * pr[:, :, :, None, :]
        relay = lambda t: t.transpose(1, 2, 4, 0, 3).reshape(2, ngroups, S5_STATE, S5_ROW)
        return relay(cr), relay(-ci)

    pr, pi = powers(t_len - 1 - ti, ti)
    sr = pr[..., None] * bbr[None] - pi[..., None] * bbi[None]
    si = pr[..., None] * bbi[None] + pi[..., None] * bbr[None]
    to_rows = lambda t: t.transpose(1, 2, 0, 4, 3).reshape(2, ngroups, S5_ROW, S5_STATE)
    w_sr, w_si = to_rows(sr), to_rows(si)
    w_or, w_oi = times_c(*powers(ti + 1, t_len - ti))
    w_nr, w_ni = times_c(*powers(ti - (t_len - 1), -ti))
    full = jnp.full((t_len,), float(t_len), F32)
    lam_r, lam_i = powers(full, full)
    return (w_sr, w_si, w_nr, w_ni, w_or.astype(BF16), w_oi.astype(BF16), lam_r[0], lam_i[0])


def _s5_perm():
    src = jnp.arange(S5_SLAB, dtype=jnp.int32)
    t, g, c = src // 128, (src % 128) // S5_GROUP, src % S5_GROUP
    dst = g * S5_ROW + t * S5_GROUP + c
    return (dst[:, None] == jnp.arange(S5_SLAB, dtype=jnp.int32)[None, :]).astype(BF16)


def _s5_kernel(*refs, nparts, split_steps, batch, ncc, ncl, col_chunks):
    nin = S5_CHUNK if split_steps else nparts
    x_refs = refs[:nin]
    perm_ref, wsr_ref, wsi_ref, wnr_ref, wni_ref, wor_ref, woi_ref, lr_ref, li_ref = refs[nin:nin + 9]
    y_refs = refs[nin + 9:nin + 9 + nparts]
    sr_scr, si_scr, hr_scr, hi_scr, y_scr = refs[nin + 9 + nparts:]
    rows = y_scr.shape[0]
    groups = S5_SLAB_GROUPS

    def load(t):
        if split_steps:
            return x_refs[t][...]
        parts = [r[:, t].reshape(-1, 128) for r in x_refs]
        return (parts[0] if nparts == 1 else jnp.concatenate(parts, axis=0)).astype(BF16)

    xcat = jnp.concatenate([load(t) for t in range(S5_CHUNK)], axis=1)
    u_all = _dot(xcat, perm_ref[...]).astype(BF16)
    t_in = lax.broadcasted_iota(jnp.int32, (S5_ROW, S5_ROW), 0) // S5_GROUP
    t_out = lax.broadcasted_iota(jnp.int32, (S5_ROW, S5_ROW), 1) // S5_GROUP

    def hdot(a, b):
        a_hi, b_hi = a.astype(BF16), b.astype(BF16)
        a_lo, b_lo = (a - a_hi.astype(F32)).astype(BF16), (b - b_hi.astype(F32)).astype(BF16)
        return _dot(a_hi, b_hi) + (_dot(a_hi, b_lo) + _dot(a_lo, b_hi))

    def row_of(b, j, direction):
        if direction == 0:
            kc, kl = j, j - ncc
        else:
            kc, kl = ncc - 1 - j, ncl - 1 - (j - ncc)
        if col_chunks:
            kl = (kl % col_chunks) * GRID_W + kl // col_chunks
        return jnp.where(j < ncc, batch * ncl + b * ncc + kc, b * ncl + kl)

    for direction in range(2):
        for g in range(groups):
            u = u_all[:, g * S5_ROW:(g + 1) * S5_ROW]
            sr_scr[g * rows:(g + 1) * rows, :] = _dot(u, wsr_ref[direction, g].astype(BF16))
            si_scr[g * rows:(g + 1) * rows, :] = _dot(u, wsi_ref[direction, g].astype(BF16))
        lam_r = lr_ref[direction]
        lam_i = li_ref[direction]

        def step(j, carry, direction=direction, lam_r=lam_r, lam_i=lam_i):
            out = []
            for b in range(batch):
                h_r, h_i = carry[2 * b], carry[2 * b + 1]
                idx = pl.ds(row_of(b, j, direction), groups, stride=rows)
                hr_scr[idx, :] = h_r
                hi_scr[idx, :] = h_i
                s_r = sr_scr[idx, :]
                s_i = si_scr[idx, :]
                out += [lam_r * h_r - lam_i * h_i + s_r, lam_r * h_i + lam_i * h_r + s_i]
            return tuple(out)

        zero = jnp.zeros((groups, S5_STATE), F32)
        lax.fori_loop(0, ncc + ncl, step, (zero,) * (2 * batch))
        causal = (t_out >= t_in) if direction == 0 else (t_in >= t_out)
        for g in range(groups):
            u = u_all[:, g * S5_ROW:(g + 1) * S5_ROW]
            m_intra = jnp.where(causal, hdot(wsr_ref[direction, g], wnr_ref[direction, g])
                                + hdot(wsi_ref[direction, g], wni_ref[direction, g]), 0.0).astype(BF16)
            y = (_dot(u, m_intra)
                 + _dot(hr_scr[g * rows:(g + 1) * rows, :].astype(BF16), wor_ref[direction, g])
                 + _dot(hi_scr[g * rows:(g + 1) * rows, :].astype(BF16), woi_ref[direction, g]))
            if direction == 0:
                y_scr[:, g * S5_ROW:(g + 1) * S5_ROW] = y
            else:
                y_scr[:, g * S5_ROW:(g + 1) * S5_ROW] += y

    z = _dot_nt(y_scr[...].astype(BF16), perm_ref[...])
    for t in range(S5_CHUNK):
        zt = z[:, t * 128:(t + 1) * 128]
        r0 = 0
        for y_ref in y_refs:
            n = y_ref.shape[0] * (y_ref.shape[2] if len(y_ref.shape) == 4 else 1)
            y_ref[:, t] = zt[r0:r0 + n].reshape(y_ref.shape[:1] + y_ref.shape[2:])
            r0 += n


def _s5_core(h, hbf, tables, dims, col_major):
    batch, lat_len, ctx_len, _ = dims
    wsr, wsi, wnr, wni, wor, woi, lam_r, lam_i = tables
    d = D_MODEL
    ncc, ncl = ctx_len // S5_CHUNK, lat_len // S5_CHUNK
    rows = batch * (ncc + ncl)
    gs = S5_SLAB_GROUPS
    if col_major:
        grid_rows = lat_len // GRID_W
        assert grid_rows % S5_CHUNK == 0
        col_chunks = grid_rows // S5_CHUNK
        nlat = batch * lat_len
        xs = [h[:nlat].reshape(batch * col_chunks, S5_CHUNK, GRID_W, d),
              h[nlat:].reshape(batch * ncc, S5_CHUNK, d)]
        blocks = [pl.BlockSpec((batch * col_chunks, S5_CHUNK, GRID_W, 128), lambda q: (0, 0, 0, q)),
                  pl.BlockSpec((batch * ncc, S5_CHUNK, 128), lambda q: (0, 0, q))]
        ins, in_blocks = xs, blocks
    else:
        col_chunks = 0
        xs = [h.reshape(rows, S5_CHUNK, d)]
        blocks = [pl.BlockSpec((rows, S5_CHUNK, 128), lambda q: (0, 0, q))]
        ins = [hbf.reshape(rows, S5_CHUNK * d)] * S5_CHUNK
        in_blocks = [pl.BlockSpec((rows, 128), lambda q, t=t: (0, t * (d // 128) + q)) for t in range(S5_CHUNK)]
    kern = functools.partial(_s5_kernel, nparts=len(xs), split_steps=not col_major, batch=batch, ncc=ncc, ncl=ncl,
                             col_chunks=col_chunks)
    wspec = lambda a, b: pl.BlockSpec((2, gs, a, b), lambda q: (0, q, 0, 0))
    ys = pl.pallas_call(
        kern,
        grid=(S5_GROUPS // gs,),
        in_specs=in_blocks + [pl.BlockSpec((S5_SLAB, S5_SLAB), lambda q: (0, 0)),
                           wspec(S5_ROW, S5_STATE), wspec(S5_ROW, S5_STATE),
                           wspec(S5_STATE, S5_ROW), wspec(S5_STATE, S5_ROW),
                           wspec(S5_STATE, S5_ROW), wspec(S5_STATE, S5_ROW),
                           pl.BlockSpec((2, gs, S5_STATE), lambda q: (0, q, 0)),
                           pl.BlockSpec((2, gs, S5_STATE), lambda q: (0, q, 0))],
        out_specs=blocks,
        out_shape=[jax.ShapeDtypeStruct(x.shape, F32) for x in xs],
        scratch_shapes=[pltpu.VMEM((gs * rows, S5_STATE), F32) for _ in range(4)]
        + [pltpu.VMEM((rows, S5_SLAB), F32)],
        compiler_params=_params("parallel"),
    )(*ins, _s5_perm(), wsr, wsi, wnr, wni, wor, woi, lam_r, lam_i)
    if col_major:
        return jnp.concatenate([ys[0].reshape(-1, d), ys[1].reshape(-1, d)], axis=0)
    return ys[0].reshape(-1, d)


def _s5_glu_kernel(h_ref, y_ref, dskip_ref, w1_ref, w2_ref, b1_ref, b2_ref, x_ref, gate_ref, o_ref, z_scr):
    @pl.when(pl.program_id(1) == 0)
    def _():
        y = h_ref[...] * dskip_ref[...] + y_ref[...]
        z = 0.5 * y * (1.0 + jnp.tanh(math.sqrt(2.0 / math.pi) * (y + 0.044715 * (y * y * y))))
        z_scr[...] = z.astype(BF16)

    z = z_scr[...]
    lin = _dot(z, w1_ref[...]) + b1_ref[...]
    gat = _dot(z, w2_ref[...]) + b2_ref[...]
    o_ref[...] = x_ref[...] + gate_ref[...] * (lin * _sigmoid(gat))


def _s5_glu(h, y, dskip, w, b, x, mods, layer, dims):
    batch, lat_len, _, tm = dims
    ntok, d = x.shape
    tn = 512
    nblk = d // tn
    tiles_per_seq = lat_len // tm
    b = b.reshape(1, 2 * d)
    return pl.pallas_call(
        _s5_glu_kernel,
        grid=(ntok // tm, nblk),
        in_specs=[pl.BlockSpec((tm, d), lambda i, j: (i, 0)),
                  pl.BlockSpec((tm, d), lambda i, j: (i, 0)),
                  pl.BlockSpec((1, d), lambda i, j: (0, 0)),
                  pl.BlockSpec((d, tn), lambda i, j: (0, j)),
                  pl.BlockSpec((d, tn), lambda i, j: (0, nblk + j)),
                  pl.BlockSpec((1, tn), lambda i, j: (0, j)),
                  pl.BlockSpec((1, tn), lambda i, j: (0, nblk + j)),
                  pl.BlockSpec((tm, tn), lambda i, j: (i, j)),
                  pl.BlockSpec((None, 1, tn), lambda i, j: (
                      layer * 8 + jnp.minimum(i // tiles_per_seq, batch), 0, 2 * nblk + j))],
        out_specs=pl.BlockSpec((tm, tn), lambda i, j: (i, j)),
        out_shape=jax.ShapeDtypeStruct((ntok, d), F32),
        scratch_shapes=[pltpu.VMEM((tm, d), BF16)],
        compiler_params=_params("parallel", "arbitrary"),
    )(h, y, dskip.reshape(1, d), w, w, b, b, x, mods)


def _s5_layer(x, mods, layer, j, dims, p):
    batch, lat_len, ctx_len, _ = dims
    col_major = (j % 2) == 1
    h, hbf = _prenorm(x, mods, p['norm_g'][layer, 0], layer, dims)
    tables = _s5_tables(p['s5_a_re'][j], p['s5_a_im'][j], p['s5_log_dt'][j], p['s5_b_re'][j],
                        p['s5_b_im'][j], p['s5_c_re'][j], p['s5_c_im'][j])
    y = _s5_core(h, hbf, tables, dims, col_major)
    return _s5_glu(h, y, p['s5_d'][j], p['s5_glu_w'][j].astype(BF16), p['s5_glu_b'][j], x, mods, layer, dims)


def _seq_blocks(dims):
    batch, lat_len, ctx_len, _ = dims
    return [(lat_len, 0), (ctx_len, batch * lat_len // ctx_len)]


def _dwconv3_kernel(u_ref, w_ref, b_ref, o_ref, *maybe_bf16_ref):
    x = u_ref[...]
    n = x.shape[0]
    row = lax.broadcasted_iota(jnp.int32, x.shape, 0)
    prev = jnp.where(row == 0, 0.0, pltpu.roll(x, 1, 0))
    nxt = jnp.where(row == n - 1, 0.0, pltpu.roll(x, n - 1, 0))
    y = prev * w_ref[0:1, :] + x * w_ref[1:2, :] + nxt * w_ref[2:3, :] + b_ref[...]
    o_ref[...] = y
    for r in maybe_bf16_ref:
        r[...] = y.astype(BF16)


def _dwconv3(u, w, b, dims, col0, ncols, with_bf16):
    batch = dims[0]
    ntok = u.shape[0]
    tc = 256
    cb0 = col0 // tc
    w8 = jnp.zeros((8, w.shape[1]), F32).at[:3].set(w)
    outs = []
    for n, off in _seq_blocks(dims):
        out_shape = [jax.ShapeDtypeStruct((batch * n, ncols), F32)]
        out_specs = [pl.BlockSpec((n, tc), lambda s, j: (s, j))]
        if with_bf16:
            out_shape.append(jax.ShapeDtypeStruct((batch * n, ncols), BF16))
            out_specs.append(pl.BlockSpec((n, tc), lambda s, j: (s, j)))
        outs.append(pl.pallas_call(
            _dwconv3_kernel,
            grid=(batch, ncols // tc),
            in_specs=[pl.BlockSpec((n, tc), lambda s, j, off=off: (off + s, cb0 + j)),
                      pl.BlockSpec((8, tc), lambda s, j: (0, cb0 + j)),
                      pl.BlockSpec((1, tc), lambda s, j: (0, cb0 + j))],
            out_specs=out_specs,
            out_shape=out_shape,
            compiler_params=_params("parallel", "parallel"),
        )(u, w8, b.reshape(1, -1)))
    return outs


def _phase_table_kernel(ar_ref, ai_ref, br_ref, bi_ref, c_ref, s_ref):
    br, bi = br_ref[...], bi_ref[...]
    for r in range(ar_ref.shape[0]):
        ar, ai = ar_ref[r:r + 1, :], ai_ref[r:r + 1, :]
        c_ref[r * 16:(r + 1) * 16, :] = (ar * br - ai * bi).astype(BF16)
        s_ref[r * 16:(r + 1) * 16, :] = (ai * br + ar * bi).astype(BF16)


def _phase_tables(coarse, fine, period):
    def unit(m):
        ang = (m % period).astype(F32) * (2.0 * math.pi / period)
        return jnp.cos(ang), jnp.sin(ang)

    ar, ai = unit(coarse)
    br, bi = unit(fine)
    ntab, r1, ncol = coarse.shape
    rb = min(32, r1)
    out = jax.ShapeDtypeStruct((ntab, r1 * 16, ncol), BF16)
    cspec = pl.BlockSpec((None, rb, ncol), lambda t, i: (t, i, 0))
    fspec = pl.BlockSpec((None, 16, ncol), lambda t, i: (t, 0, 0))
    ospec = pl.BlockSpec((None, rb * 16, ncol), lambda t, i: (t, i, 0))
    return pl.pallas_call(
        _phase_table_kernel,
        grid=(ntab, r1 // rb),
        in_specs=[cspec, cspec, fspec, fspec],
        out_specs=[ospec, ospec],
        out_shape=[out, out],
        compiler_params=_params("parallel", "parallel"),
    )(ar, ai, br, bi)


def _dft_tables(n):
    m = n // 2
    idx = jnp.arange(m, dtype=jnp.int32)[None, :]
    r1 = jnp.arange(m // 16, dtype=jnp.int32)[:, None]
    r0 = jnp.arange(16, dtype=jnp.int32)[:, None]
    par = jnp.arange(2, dtype=jnp.int32)[:, None, None]
    tau = 2 * idx[None] + par
    cmat, smat = _phase_tables(32 * r1[None] * tau, (2 * r0[None] + 1) * tau, 4 * n)
    odd = 2 * idx[None] + 1
    ctm, stm = _phase_tables(jnp.broadcast_to(32 * r1[None] * odd, (2,) + (m // 16, m)), (2 * r0[None] + par) * odd, 4 * n)
    return cmat, smat, ctm, stm


def _hy_filter_kernel(feat_ref, w1_ref, b1_ref, w2_ref, b2_ref, w3_ref, b3_ref, fq_ref, tu_ref, dl_ref,
                      w4_ref, hs_ref, hd_ref, hdn_scr, split_scr):
    @pl.when(pl.program_id(0) == 0)
    def _():
        fq = fq_ref[...]
        h = jnp.sin(fq * (jnp.dot(feat_ref[...], w1_ref[...], precision=HIGHEST, preferred_element_type=F32) + b1_ref[...]))
        h = jnp.sin(fq * (jnp.dot(h, w2_ref[...], precision=HIGHEST, preferred_element_type=F32) + b2_ref[...]))
        h = jnp.sin(fq * (jnp.dot(h, w3_ref[...], precision=HIGHEST, preferred_element_type=F32) + b3_ref[...]))
        hdn_scr[...] = h

    hdn = hdn_scr[...]
    decay = jnp.exp(-tu_ref[...] * dl_ref[...])
    row = lax.broadcasted_iota(jnp.int32, decay.shape, 0)
    m = decay.shape[0] // 2
    for order in range(HY_ORDER):
        fwd = jnp.dot(hdn, w4_ref[2 * order], precision=HIGHEST, preferred_element_type=F32) * decay
        bwd = jnp.dot(hdn, w4_ref[2 * order + 1], precision=HIGHEST, preferred_element_type=F32) * decay
        bwd = jnp.where(row == 0, 0.0, bwd)
        norm = jnp.sum(jnp.abs(fwd), axis=0, keepdims=True) + jnp.sum(jnp.abs(bwd), axis=0, keepdims=True)
        for out_ref, vals in ((hs_ref, (fwd + bwd) / norm), (hd_ref, (fwd - bwd) / norm)):
            split_scr[...] = vals
            for parity in range(2):
                out_ref[order, parity] = split_scr[pl.ds(parity, m, stride=2), :].astype(BF16)


def _hy_filters(n, p, j):
    d = D_MODEL
    t = jnp.arange(n, dtype=F32)
    t_unit = t / max(n - 1, 1)
    bands = jnp.linspace(1e-4, HY_BANDS - 1, HY_BANDS, dtype=F32)
    ang = (2.0 * math.pi / n) * t[:, None] * bands[None, :]
    feats = jnp.concatenate([t_unit[:, None], jnp.cos(ang), -jnp.sin(ang)], axis=-1)
    feats = jnp.pad(feats, ((0, 0), (0, HY_EMB_PAD - HY_EMB)))
    w1 = jnp.pad(p['hy_f_w1'][j], ((0, HY_EMB_PAD - HY_EMB), (0, 0)))
    deltas = jnp.abs(jnp.linspace(HY_MIN_DECAY, HY_MAX_DECAY, d, dtype=F32)).reshape(1, d)
    w4 = p['hy_f_w4'][j].reshape(HY_FILTER_HIDDEN, 2 * HY_ORDER, d).transpose(1, 0, 2)
    hid = HY_FILTER_HIDDEN
    tc = 128
    full = lambda shape: pl.BlockSpec(shape, lambda c: tuple(0 for _ in shape))
    row = lambda v: v.reshape(1, hid)
    return pl.pallas_call(
        _hy_filter_kernel,
        grid=(d // tc,),
        in_specs=[full((n, HY_EMB_PAD)), full((HY_EMB_PAD, hid)), full((1, hid)), full((hid, hid)), full((1, hid)),
                  full((hid, hid)), full((1, hid)), full((1, hid)), full((n, 1)),
                  pl.BlockSpec((1, tc), lambda c: (0, c)),
                  pl.BlockSpec((2 * HY_ORDER, hid, tc), lambda c: (0, 0, c))],
        out_specs=[pl.BlockSpec((HY_ORDER, 2, n // 2, tc), lambda c: (0, 0, 0, c)),
                   pl.BlockSpec((HY_ORDER, 2, n // 2, tc), lambda c: (0, 0, 0, c))],
        out_shape=[jax.ShapeDtypeStruct((HY_ORDER, 2, n // 2, d), BF16),
                   jax.ShapeDtypeStruct((HY_ORDER, 2, n // 2, d), BF16)],
        scratch_shapes=[pltpu.VMEM((n, hid), F32), pltpu.VMEM((n, tc), F32)],
        compiler_params=_params("arbitrary"),
    )(feats, w1, row(p['hy_f_b1'][j]), p['hy_f_w2'][j], row(p['hy_f_b2'][j]), p['hy_f_w3'][j],
      row(p['hy_f_b3'][j]), row(p['hy_f_freq'][j]), t_unit.reshape(n, 1), deltas, w4)


def _half_spectra(c_ref, s_ref, xe_c, xo_c, xe_s, xo_s):
    a, b = _dot(c_ref[0], xe_c), _dot(c_ref[1], xo_c)
    cs, ds = _dot(s_ref[0], xe_s), _dot(s_ref[1], xo_s)
    return a + b, a - b, cs + ds, ds - cs


def _dft_filter_kernel(c_ref, s_ref, hse_ref, hso_ref, hde_ref, hdo_ref, hr_ref, hi_ref):
    pc, qc, ps, qs = _half_spectra(c_ref, s_ref, hse_ref[...], hso_ref[...], hde_ref[...], hdo_ref[...])
    hr_ref[0], hr_ref[1] = pc, qc
    hi_ref[0], hi_ref[1] = -ps, -qs


def _dft_filter(cmat, smat, hs, hd, tk, td):
    order, _, m, d = hs.shape
    nd = d // td
    tab = pl.BlockSpec((2, tk, m), lambda i, o, j: (0, i, 0))
    even = pl.BlockSpec((None, None, m, td), lambda i, o, j: (o, 0, 0, j))
    odd = pl.BlockSpec((None, None, m, td), lambda i, o, j: (o, 1, 0, j))
    out = pl.BlockSpec((None, 2, tk, td), lambda i, o, j: (o, 0, i, j))
    sds = jax.ShapeDtypeStruct((order, 2, m, d), F32)
    return pl.pallas_call(
        _dft_filter_kernel,
        grid=(m // tk, order, nd),
        in_specs=[tab, tab, even, odd, even, odd],
        out_specs=[out, out],
        out_shape=[sds, sds],
        compiler_params=_params("parallel", "parallel", "parallel"),
    )(cmat, smat, hs, hs, hd, hd)


def _lane_blocks(nrows, width, row_block, col_block):
    ncol = width // 128
    return [pl.BlockSpec((nrows, 128), lambda *g, c=c: (row_block(*g), col_block(*g) * ncol + c))
            for c in range(ncol)]


def _parity_rows(refs, parity, count):
    cols = [r[pl.ds(parity, count, stride=2), :] for r in refs]
    return cols[0] if len(cols) == 1 else jnp.concatenate(cols, axis=1)


def _dft_fwd_kernel(c_ref, s_ref, *refs):
    z_refs = refs[:-4]
    hr_ref, hi_ref, ua_ref, ub_ref = refs[-4:]
    m = z_refs[0].shape[0] // 2
    ze = _parity_rows(z_refs, 0, m).astype(BF16)
    zo = _parity_rows(z_refs, 1, m).astype(BF16)
    pc, qc, ps, qs = _half_spectra(c_ref, s_ref, ze, zo, ze, zo)
    yrp = pc * hr_ref[0] + ps * hi_ref[0]
    yip = pc * hi_ref[0] - ps * hr_ref[0]
    yrq = qc * hr_ref[1] + qs * hi_ref[1]
    yiq = qc * hi_ref[1] - qs * hr_ref[1]
    ua_ref[0] = (yrp + yrq).astype(BF16)
    ub_ref[0] = (yip - yiq).astype(BF16)
    ua_ref[1] = (yrp - yrq).astype(BF16)
    ub_ref[1] = (yip + yiq).astype(BF16)


def _dft_fwd(cmat, smat, z, zcol, d, hr, hi, order, batch, tk, td):
    m = cmat.shape[1]
    tab = pl.BlockSpec((2, tk, m), lambda i, s, j: (0, i, 0), pipeline_mode=pl.Buffered(1))
    hspec = pl.BlockSpec((None, 2, tk, td), lambda i, s, j: (order, 0, i, j))
    out = pl.BlockSpec((2, None, tk, td), lambda i, s, j: (0, s, i, j))
    sds = jax.ShapeDtypeStruct((2, batch, m, d), BF16)
    return pl.pallas_call(
        _dft_fwd_kernel,
        grid=(m // tk, batch, d // td),
        in_specs=[tab, tab] + _lane_blocks(2 * m, td, lambda i, s, j: s, lambda i, s, j: zcol // td + j)
        + [hspec, hspec],
        out_specs=[out, out],
        out_shape=[sds, sds],
        compiler_params=_params("parallel", "parallel", "parallel"),
    )(cmat, smat, *([z] * (td // 128)), hr, hi)


def _dft_inv_kernel(ct_ref, st_ref, ua_ref, ub_ref, *refs, inv_n):
    ncol = (len(refs) - 2) // 3
    xg_refs, zp_refs, bias_ref, o_ref = refs[:ncol], refs[ncol:2 * ncol], refs[2 * ncol], refs[2 * ncol + 1]
    mix_scrs = refs[2 * ncol + 2:]
    ts = ct_ref.shape[1]
    bias = bias_ref[...]
    for parity in range(2):
        y = (_dot(ct_ref[parity], ua_ref[parity]) - _dot(st_ref[parity], ub_ref[parity])) * inv_n
        out = _parity_rows(xg_refs, parity, ts) * (y + _parity_rows(zp_refs, parity, ts) * bias)
        for c, scr in enumerate(mix_scrs):
            scr[pl.ds(parity, ts, stride=2), :] = out[:, c * 128:(c + 1) * 128]
    for c, scr in enumerate(mix_scrs):
        o_ref[:, c * 128:(c + 1) * 128] = scr[...]


def _dft_inv(ctm, stm, ua, ub, xg, xg_col0, zprev, zcol, bias, batch, ts, td):
    m = ctm.shape[1]
    d = ua.shape[3]
    ncol = td // 128
    tab = pl.BlockSpec((2, ts, m), lambda i, s, j: (0, i, 0))
    spec = pl.BlockSpec((2, None, m, td), lambda i, s, j: (0, s, 0, j))
    rblk = lambda i, s, j: s * (m // ts) + i
    rows = lambda col0: _lane_blocks(2 * ts, td, rblk, lambda i, s, j: col0 // td + j)
    kern = functools.partial(_dft_inv_kernel, inv_n=1.0 / (2 * m))
    return pl.pallas_call(
        kern,
        grid=(m // ts, batch, d // td),
        in_specs=[tab, tab, spec, spec] + rows(xg_col0) + rows(zcol) + [pl.BlockSpec((1, td), lambda i, s, j: (0, j))],
        out_specs=pl.BlockSpec((2 * ts, td), lambda i, s, j: (rblk(i, s, j), j)),
        out_shape=jax.ShapeDtypeStruct((zprev.shape[0], d), F32),
        scratch_shapes=[pltpu.VMEM((2 * ts, 128), F32) for _ in range(ncol)],
        compiler_params=_params("parallel", "parallel", "parallel"),
    )(ctm, stm, ua, ub, *([xg] * ncol), *([zprev] * ncol), bias.reshape(1, d))


def _hyena_layer(x, mods, layer, j, dims, p):
    batch, lat_len, ctx_len, _ = dims
    d = D_MODEL
    u = _mm_norm(x, mods, p['norm_g'][layer, 0], p['hy_in_w'], j, 0, 3 * d, p['hy_in_b'][j], layer, dims, 512)
    parts = _dwconv3(u, p['hy_conv_w'][j], p['hy_conv_b'][j], dims, 0, 3 * d, False)
    z_out = []
    for (n, _), (conv,) in zip(_seq_blocks(dims), parts):
        m = n // 2
        cmat, smat, ctm, stm = _dft_tables(n)
        hs, hd = _hy_filters(n, p, j)
        hr, hi = _dft_filter(cmat, smat, hs, hd, min(512, m), 512)
        zprev, zcol = conv, 0
        for order in range(HY_ORDER):
            ua, ub = _dft_fwd(cmat, smat, zprev, zcol, d, hr, hi, order, batch, min(1024, m), 256)
            zprev = _dft_inv(ctm, stm, ua, ub, conv, (1 + order) * d, zprev, zcol, p['hy_bias'][j, order],
                             batch, min(512, m), 256)
            zcol = 0
        z_out.append(zprev)
    z = jnp.concatenate(z_out, axis=0)
    return _mm_res(z, p['hy_out_w'][j].astype(BF16), p['hy_out_b'][j], x, mods, layer, dims, 512)


def _gdn_conv_kernel(p_ref, w_ref, o_ref):
    x = p_ref[...]
    n = x.shape[0]
    row = lax.broadcasted_iota(jnp.int32, x.shape, 0)
    acc = x * w_ref[2:3, :]
    for s in (1, 2):
        prev = jnp.where(row < s, 0.0, pltpu.roll(x, s, 0))
        nxt = jnp.where(row >= n - s, 0.0, pltpu.roll(x, n - s, 0))
        acc = acc + prev * w_ref[2 - s:3 - s, :] + nxt * w_ref[2 + s:3 + s, :]
    y = _silu(acc)
    head = pl.program_id(1)
    inv = lax.rsqrt(jnp.sum(y * y, axis=-1, keepdims=True) + 1e-6)
    inv = inv * jnp.where(head < GDN_K_HEADS, GDN_HEAD ** -0.5, 1.0)
    o_ref[...] = y * jnp.where(head < 2 * GDN_K_HEADS, inv, 1.0)


def _gdn_conv(proj, w, dims):
    batch = dims[0]
    ntok = proj.shape[0]
    nheads = GDN_CONV_DIM // GDN_HEAD
    w8 = jnp.zeros((8, GDN_CONV_DIM), F32).at[:5].set(w)
    out = None
    for n, off in _seq_blocks(dims):
        args = [proj, w8]
        in_specs = [pl.BlockSpec((n, GDN_HEAD), lambda s, h, off=off: (off + s, h)),
                    pl.BlockSpec((8, GDN_HEAD), lambda s, h: (0, h))]
        aliases = {}
        if out is not None:
            args.append(out)
            in_specs.append(pl.BlockSpec(memory_space=pl.ANY))
            aliases = {2: 0}
        kern = _gdn_conv_kernel if out is None else (lambda p_ref, w_ref, _, o_ref: _gdn_conv_kernel(p_ref, w_ref, o_ref))
        out = pl.pallas_call(
            kern,
            grid=(batch, nheads),
            in_specs=in_specs,
            out_specs=pl.BlockSpec((None, n, GDN_HEAD), lambda s, h, off=off: (h, off + s, 0)),
            out_shape=jax.ShapeDtypeStruct((nheads, ntok, GDN_HEAD), F32),
            input_output_aliases=aliases,
            compiler_params=_params("parallel", "parallel"),
        )(*args)
    return out


def _gdn_gate_kernel(ab_ref, alog_ref, dtb_ref, o_ref):
    c = GDN_CHUNK
    row = lax.broadcasted_iota(jnp.int32, (c, c), 0)
    col = lax.broadcasted_iota(jnp.int32, (c, c), 1)
    lower = jnp.where(row >= col, 1.0, 0.0).astype(F32)
    upper = jnp.where(row <= col, 1.0, 0.0).astype(F32)
    lane = lax.broadcasted_iota(jnp.int32, (c, 128), 1)
    for r in range(ab_ref.shape[0] // c):
        ab = ab_ref[r * c:(r + 1) * c, :]
        xs = ab + dtb_ref[...]
        softplus = jnp.maximum(xs, 0.0) + jnp.log(1.0 + jnp.exp(-jnp.abs(xs)))
        g = -jnp.exp(alog_ref[...]) * softplus
        cum_f = jnp.dot(lower, g, precision=HIGHEST, preferred_element_type=F32)
        cum_b = jnp.dot(upper, g, precision=HIGHEST, preferred_element_type=F32)
        o_ref[r * c:(r + 1) * c, :] = jnp.where(lane < GDN_V_HEADS, cum_f,
                                                jnp.where(lane < 2 * GDN_V_HEADS, cum_b, _sigmoid(ab)))


def _gdn_gates(ab, a_log, dt_bias, tm):
    ntok = ab.shape[0]
    pad = jnp.zeros((2 * GDN_V_HEADS,), F32)
    alog = jnp.concatenate([a_log.reshape(-1), pad]).reshape(1, 128)
    dtb = jnp.concatenate([dt_bias.reshape(-1), pad]).reshape(1, 128)
    return pl.pallas_call(
        _gdn_gate_kernel,
        grid=(ntok // tm,),
        in_specs=[pl.BlockSpec((tm, 128), lambda i: (i, 0)),
                  pl.BlockSpec((1, 128), lambda i: (0, 0)),
                  pl.BlockSpec((1, 128), lambda i: (0, 0))],
        out_specs=pl.BlockSpec((tm, 128), lambda i: (i, 0)),
        out_shape=jax.ShapeDtypeStruct((ntok, 128), F32),
        compiler_params=_params("parallel"),
    )(ab, alog, dtb)


def _unit_triangular_inverses(mats):
    c = mats[0].shape[0]
    row = lax.broadcasted_iota(jnp.int32, (c, c), 0)
    col = lax.broadcasted_iota(jnp.int32, (c, c), 1)
    eye = jnp.where(row == col, 1.0, 0.0).astype(F32)
    ts = [eye - a for a in mats]
    pws = list(mats)
    for _ in range(int(math.log2(c)) - 1):
        pwbs = [pw.astype(BF16) for pw in pws]
        pws = [_dot(pwb, pwb) for pwb in pwbs]
        ts = [t + _dot(t.astype(BF16), pw.astype(BF16)) for t, pw in zip(ts, pws)]
    ms = [eye + a for a in mats]
    m_his = [m.astype(BF16) for m in ms]
    m_los = [(m - m_hi.astype(F32)).astype(BF16) for m, m_hi in zip(ms, m_his)]
    t_his = [t.astype(BF16) for t in ts]
    t_los = [(t - t_hi.astype(F32)).astype(BF16) for t, t_hi in zip(ts, t_his)]
    resids = [eye - (_dot(m_hi, t_hi) + (_dot(m_hi, t_lo) + _dot(m_lo, t_hi)))
              for m_hi, m_lo, t_hi, t_lo in zip(m_his, m_los, t_his, t_los)]
    return [t + _dot(t_hi, r.astype(BF16)) for t, t_hi, r in zip(ts, t_his, resids)]


def _gdn_chunk_kernel(qf_ref, kf_ref, vf_ref, gcf_ref, grf_ref, qb_ref, kb_ref, vb_ref, gcb_ref, grb_ref,
                      of_ref, ob_ref, state_scr, *, nchunks):
    c = GDN_CHUNK

    @pl.when(pl.program_id(2) == 0)
    def _():
        state_scr[...] = jnp.zeros_like(state_scr)

    row = lax.broadcasted_iota(jnp.int32, (c, c), 0)
    col = lax.broadcasted_iota(jnp.int32, (c, c), 1)
    blocks = ((qf_ref, kf_ref, vf_ref, gcf_ref, grf_ref, of_ref), (qb_ref, kb_ref, vb_ref, gcb_ref, grb_ref, ob_ref))
    nkh = qf_ref.shape[0]
    keys, amats, part = [], [], {}
    for direction, (q_ref, k_ref, v_ref, gc_ref, gr_ref, o_ref) in enumerate(blocks):
        if direction == 0:
            incl, strict, last = row >= col, row > col, c - 1
        else:
            incl, strict, last = row <= col, row < col, 0
        for kh in range(nkh):
            for ci in range(nchunks):
                rows = slice(ci * c, (ci + 1) * c)
                q = q_ref[kh, rows, :].astype(BF16)
                k32 = k_ref[kh, rows, :]
                k = k32.astype(BF16)
                k_t = k32.T
                gates_c = gc_ref[kh, ci]
                gates_r = gr_ref[kh, ci]
                kk = _dot_nt(k, k)
                qk = _dot_nt(q, k)
                for e in range(2):
                    ch_g, ch_b = direction * 2 + e, 4 + direction * 2 + e
                    gcc, gcr = gates_c[:, ch_g:ch_g + 1], gates_r[ch_g:ch_g + 1, :]
                    beta_c, beta_r = gates_c[:, ch_b:ch_b + 1], gates_r[ch_b:ch_b + 1, :]
                    decay = jnp.where(incl, jnp.exp(jnp.where(incl, gcc - gcr, 0.0)), 0.0)
                    key = (direction, kh, ci, e)
                    keys.append(key)
                    amats.append(jnp.where(strict, kk * beta_c * decay, 0.0))
                    g_last = gcr[:, last:last + 1]
                    part[key] = dict(
                        q=q, k=k, v=v_ref[2 * kh + e, rows, :].astype(BF16), attn=(qk * decay).astype(BF16),
                        beta_r=beta_r, wscale=beta_r * jnp.exp(gcr), egc=jnp.exp(gcc),
                        kg_t=(k_t * jnp.exp(g_last - gcr)).astype(BF16), e_last=jnp.exp(g_last))
    tmats = _unit_triangular_inverses(amats)
    local = {}
    for key, t in zip(keys, tmats):
        p = part[key]
        u = _dot((t * p['beta_r']).astype(BF16), p['v'])
        w = _dot((t * p['wscale']).astype(BF16), p['k']).astype(BF16)
        local[key] = (p['q'], u, w, p['attn'], p['egc'], p['kg_t'], p['e_last'])

    nslots = state_scr.shape[0]
    states = [state_scr[slot] for slot in range(nslots)]
    streams = [(direction, kh, e) for direction in range(2) for kh in range(nkh) for e in range(2)]
    out_refs = (of_ref, ob_ref)
    for step in range(nchunks):
        cur = [local[d, kh, (step if d == 0 else nchunks - 1 - step), e] for d, kh, e in streams]
        state_bs = [s.astype(BF16) for s in states]
        ws = [_dot(p[2], sb) for p, sb in zip(cur, state_bs)]
        qs = [_dot(p[0], sb) for p, sb in zip(cur, state_bs)]
        v_news = [(p[1] - w).astype(BF16) for p, w in zip(cur, ws)]
        intra = [_dot(p[3], vn) for p, vn in zip(cur, v_news)]
        upd = [_dot(p[5], vn) for p, vn in zip(cur, v_news)]
        for (d, kh, e), p, q_s, o_in in zip(streams, cur, qs, intra):
            ci = step if d == 0 else nchunks - 1 - step
            out_refs[d][2 * kh + e, ci * c:(ci + 1) * c, :] = p[4] * q_s + o_in
        states = [s * p[6] + dlt for s, p, dlt in zip(states, cur, upd)]
    for slot in range(nslots):
        state_scr[slot] = states[slot]


def _gdn_chunks(qkvh, gates, dims):
    batch, lat_len, ctx_len, _ = dims
    c = GDN_CHUNK
    ntok = qkvh.shape[1]
    per_step = min(4, ctx_len // c)
    rows = per_step * c
    nbc, nbl = ctx_len // rows, lat_len // rows
    ctx0 = batch * lat_len // rows
    gk = gates.reshape(ntok // c, c, 4, GDN_K_HEADS, 2).transpose(3, 0, 1, 2, 4).reshape(GDN_K_HEADS, ntok // c, c, 8)
    gk_t = gk.transpose(0, 1, 3, 2)

    def fwd_blk(b, j):
        return jnp.where(j < nbc, ctx0 + b * nbc + j, b * nbl + j - nbc)

    def bwd_blk(b, j):
        return jnp.where(j < nbc, ctx0 + b * nbc + nbc - 1 - j, b * nbl + nbl - 1 - (j - nbc))

    nkh = GDN_KHEADS_PER_STEP
    hblocks = GDN_K_HEADS // nkh

    def specs(blk):
        return [pl.BlockSpec((nkh, rows, GDN_HEAD), lambda b, h, j: (h, blk(b, j), 0)),
                pl.BlockSpec((nkh, rows, GDN_HEAD), lambda b, h, j: (hblocks + h, blk(b, j), 0)),
                pl.BlockSpec((2 * nkh, rows, GDN_HEAD), lambda b, h, j: (hblocks + h, blk(b, j), 0)),
                pl.BlockSpec((nkh, per_step, c, 8), lambda b, h, j: (h, blk(b, j), 0, 0)),
                pl.BlockSpec((nkh, per_step, 8, c), lambda b, h, j: (h, blk(b, j), 0, 0))]

    out_sd = jax.ShapeDtypeStruct((GDN_V_HEADS, ntok, GDN_HEAD), F32)
    return pl.pallas_call(
        functools.partial(_gdn_chunk_kernel, nchunks=per_step),
        grid=(batch, hblocks, nbc + nbl),
        in_specs=specs(fwd_blk) + specs(bwd_blk),
        out_specs=[pl.BlockSpec((2 * nkh, rows, GDN_HEAD), lambda b, h, j: (h, fwd_blk(b, j), 0)),
                   pl.BlockSpec((2 * nkh, rows, GDN_HEAD), lambda b, h, j: (h, bwd_blk(b, j), 0))],
        out_shape=[out_sd, out_sd],
        scratch_shapes=[pltpu.VMEM((4 * nkh, GDN_HEAD, GDN_HEAD), F32)],
        compiler_params=_params("parallel", "parallel", "arbitrary"),
    )(qkvh, qkvh, qkvh, gk, gk_t, qkvh, qkvh, qkvh, gk, gk_t)


def _gdn_gated_norm_kernel(of_ref, ob_ref, z_ref, ng_ref, a_ref):
    ng = ng_ref[...]
    for h in range(of_ref.shape[0]):
        o = of_ref[h] + ob_ref[h]
        o = o * lax.rsqrt(jnp.mean(o * o, axis=-1, keepdims=True) + EPS)
        z = z_ref[:, h * GDN_HEAD:(h + 1) * GDN_HEAD]
        a_ref[:, h * GDN_HEAD:(h + 1) * GDN_HEAD] = (o * ng * _silu(z)).astype(BF16)


def _gdn_gated_norm(o_f, o_b, proj, norm_g, dims):
    tm = dims[3]
    ntok = proj.shape[0]
    hb = 8
    zblk0 = GDN_CONV_DIM // (hb * GDN_HEAD)
    return pl.pallas_call(
        _gdn_gated_norm_kernel,
        grid=(ntok // tm, GDN_V_HEADS // hb),
        in_specs=[pl.BlockSpec((hb, tm, GDN_HEAD), lambda i, h: (h, i, 0)),
                  pl.BlockSpec((hb, tm, GDN_HEAD), lambda i, h: (h, i, 0)),
                  pl.BlockSpec((tm, hb * GDN_HEAD), lambda i, h: (i, zblk0 + h)),
                  pl.BlockSpec((1, GDN_HEAD), lambda i, h: (0, 0))],
        out_specs=pl.BlockSpec((tm, hb * GDN_HEAD), lambda i, h: (i, h)),
        out_shape=jax.ShapeDtypeStruct((ntok, GDN_V), BF16),
        compiler_params=_params("parallel", "parallel"),
    )(o_f, o_b, proj, norm_g.reshape(1, GDN_HEAD))


def _gdn_layer(x, mods, layer, j, dims, p):
    in_w = p['gdn_in_w']
    g = p['norm_g'][layer, 0]
    proj = _mm_norm(x, mods, g, in_w, j, 0, GDN_MAIN, jnp.zeros((GDN_MAIN,), F32), layer, dims, 512)
    ab = _mm_norm(x, mods, g, in_w, j, GDN_MAIN, 128, jnp.zeros((128,), F32), layer, dims, 128)
    qkvh = _gdn_conv(proj, p['gdn_conv_w'][j], dims)
    gates = _gdn_gates(ab, p['gdn_a_log'][j], p['gdn_dt_bias'][j], dims[3])
    o_f, o_b = _gdn_chunks(qkvh, gates, dims)
    a = _gdn_gated_norm(o_f, o_b, proj, p['gdn_norm_g'][j], dims)
    return _mm_res(a, p['gdn_out_w'][j].astype(BF16), jnp.zeros((D_MODEL,), F32), x, mods, layer, dims, 512)


def kernel(x, c, ctx, c_ctx, ada_w, ada_b, norm_g, final_g, ffn_w_gate, ffn_w_up, ffn_w_down, s5_a_re, s5_a_im, s5_log_dt, s5_b_re, s5_b_im, s5_c_re, s5_c_im, s5_d, s5_glu_w, s5_glu_b, hy_in_w, hy_in_b, hy_conv_w, hy_conv_b, hy_f_w1, hy_f_b1, hy_f_w2, hy_f_b2, hy_f_w3, hy_f_b3, hy_f_w4, hy_f_freq, hy_bias, hy_out_w, hy_out_b, gdn_in_w, gdn_conv_w, gdn_a_log, gdn_dt_bias, gdn_norm_g, gdn_out_w):
    p = dict(locals())
    batch, lat_len, d = x.shape
    ctx_len = ctx.shape[1]
    depth = ada_w.shape[0]
    tm = min(512, batch * ctx_len)
    dims = (batch, lat_len, ctx_len, tm)
    assert d == D_MODEL and lat_len % tm == 0 and (batch * ctx_len) % tm == 0 and batch + 1 <= 8

    cvec = jnp.zeros((8, d), F32).at[:batch].set(c).at[batch].set(c_ctx)
    mods = _ada_all(cvec, ada_w, ada_b).reshape(depth * 8, 1, 6 * d)
    tok = jnp.concatenate([x.reshape(batch * lat_len, d), ctx.reshape(batch * ctx_len, d)], axis=0)
    ffn_wg, ffn_wu, ffn_wd = ffn_w_gate.astype(BF16), ffn_w_up.astype(BF16), ffn_w_down.astype(BF16)
    for i in range(depth):
        kind, j = i % N_MIXERS, i // N_MIXERS
        if kind == 0:
            tok = _s5_layer(tok, mods, i, j, dims, p)
        elif kind == 1:
            tok = _hyena_layer(tok, mods, i, j, dims, p)
        else:
            tok = _gdn_layer(tok, mods, i, j, dims, p)
        tok = _ffn(tok, mods, norm_g[i, 1], ffn_wg, ffn_wu, ffn_wd, i, dims, with_ctx=i < depth - 1)
    out = _final_norm(tok, final_g, batch * lat_len, tm)
    return out.reshape(batch, lat_len, d)
```

```python
import functools
import math

import jax
import jax.numpy as jnp
from jax import lax
from jax.experimental import pallas as pl
from jax.experimental.pallas import tpu as pltpu

F32 = jnp.float32
BF16 = jnp.bfloat16
HIGHEST = lax.Precision.HIGHEST

D_MODEL = 2048
GRID_W = 64
EPS = 1e-6
N_MIXERS = 3

S5_GROUP = 16
S5_STATE = 64
S5_GROUPS = D_MODEL // S5_GROUP
S5_CHUNK = 16
S5_ROW = S5_CHUNK * S5_GROUP
S5_SLAB_GROUPS = 128 // S5_GROUP
S5_SLAB = S5_SLAB_GROUPS * S5_ROW

HY_ORDER = 2
HY_BANDS = 16
HY_EMB = 1 + 2 * HY_BANDS
HY_EMB_PAD = 128
HY_FILTER_HIDDEN = 64
HY_DECAY_TARGET = 1e-2
HY_MAX_DECAY = math.log(HY_DECAY_TARGET) / 0.3
HY_MIN_DECAY = math.log(HY_DECAY_TARGET) / 1.5

GDN_K_HEADS = 16
GDN_V_HEADS = 32
GDN_HEAD = 128
GDN_QK = GDN_K_HEADS * GDN_HEAD
GDN_V = GDN_V_HEADS * GDN_HEAD
GDN_CONV_DIM = 2 * GDN_QK + GDN_V
GDN_MAIN = GDN_CONV_DIM + GDN_V
GDN_CHUNK = 64
GDN_KHEADS_PER_STEP = 4

VMEM_LIMIT_BYTES = 56 * 1024 * 1024


def _params(*sem):
    return pltpu.CompilerParams(dimension_semantics=sem, vmem_limit_bytes=VMEM_LIMIT_BYTES)


def _dot(a, b):
    return jnp.dot(a, b, preferred_element_type=F32)


def _dot_nt(a, b):
    return lax.dot_general(a, b, (((1,), (1,)), ((), ())), preferred_element_type=F32)


def _dot_tn(a, b):
    return lax.dot_general(a, b, (((0,), (0,)), ((), ())), preferred_element_type=F32)


def _sigmoid(x):
    return 1.0 / (1.0 + jnp.exp(-x))


def _silu(x):
    return x * _sigmoid(x)


def _norm_mod(x, g, sc, sh):
    y = x * lax.rsqrt(jnp.mean(x * x, axis=-1, keepdims=True) + EPS) * g
    return y * (1.0 + sc) + sh


def _mod_spec(layer, tm, lat_len, batch, width, col_block):
    tiles_per_seq = lat_len // tm

    def index(i, *_):
        return (layer * 8 + jnp.minimum(i // tiles_per_seq, batch), 0, col_block)

    return pl.BlockSpec((None, 1, width), index)


def _ada_kernel(c_ref, w_ref, b_ref, o_ref):
    c = c_ref[...]
    s = _silu(c).astype(BF16)
    o_ref[...] = _dot(s, w_ref[...].astype(BF16)) + b_ref[...]


def _ada_all(cvec, ada_w, ada_b):
    depth, d, n6 = ada_w.shape
    tn = 1024
    return pl.pallas_call(
        _ada_kernel,
        grid=(depth, n6 // tn),
        in_specs=[pl.BlockSpec((8, d), lambda l, j: (0, 0)),
                  pl.BlockSpec((None, d, tn), lambda l, j: (l, 0, j)),
                  pl.BlockSpec((None, 1, tn), lambda l, j: (l, 0, j))],
        out_specs=pl.BlockSpec((None, 8, tn), lambda l, j: (l, 0, j)),
        out_shape=jax.ShapeDtypeStruct((depth, 8, n6), F32),
        compiler_params=_params("parallel", "parallel"),
    )(cvec, ada_w, ada_b.reshape(depth, 1, n6))


def _prenorm_kernel(x_ref, mod_ref, g_ref, o_ref, obf_ref):
    d = D_MODEL
    h = _norm_mod(x_ref[...], g_ref[...], mod_ref[:, d:2 * d], mod_ref[:, 0:d])
    o_ref[...] = h
    obf_ref[...] = h.astype(BF16)


def _prenorm(x, mods, g, layer, dims):
    batch, lat_len, _, tm = dims
    ntok, d = x.shape
    return pl.pallas_call(
        _prenorm_kernel,
        grid=(ntok // tm,),
        in_specs=[pl.BlockSpec((tm, d), lambda i: (i, 0)),
                  _mod_spec(layer, tm, lat_len, batch, 6 * d, 0),
                  pl.BlockSpec((1, d), lambda i: (0, 0))],
        out_specs=[pl.BlockSpec((tm, d), lambda i: (i, 0)), pl.BlockSpec((tm, d), lambda i: (i, 0))],
        out_shape=[jax.ShapeDtypeStruct((ntok, d), F32), jax.ShapeDtypeStruct((ntok, d), BF16)],
        compiler_params=_params("parallel"),
    )(x, mods, g.reshape(1, d))


def _mm_norm_kernel(x_ref, mod_ref, g_ref, w_ref, b_ref, *rest):
    o_ref, h_scr = rest[-2:]
    d = D_MODEL

    @pl.when(pl.program_id(1) == 0)
    def _():
        h_scr[...] = _norm_mod(x_ref[...], g_ref[...], mod_ref[:, d:2 * d], mod_ref[:, 0:d]).astype(BF16)

    o_ref[...] = _dot(h_scr[...], w_ref[...].astype(BF16)) + b_ref[...]


def _mm_norm(x, mods, g, w_all, widx, col0, n, b, layer, dims, tn):
    batch, lat_len, ctx_len, _ = dims
    ntok, d = x.shape
    out = None
    for tm, row0, nrows in ((min(1024, lat_len), 0, batch * lat_len),
                            (batch * ctx_len, batch * lat_len, batch * ctx_len)):
        blk0 = row0 // tm
        tiles_per_seq = lat_len // tm
        mod_row = (lambda i: i // tiles_per_seq) if row0 == 0 else (lambda i: batch)
        args = [x, mods, g.reshape(1, d), w_all, b.reshape(1, n)]
        in_specs = [pl.BlockSpec((tm, d), lambda i, j: (blk0 + i, 0), pipeline_mode=pl.Buffered(1)),
                    pl.BlockSpec((None, 1, 6 * d), lambda i, j: (layer * 8 + mod_row(i), 0, 0)),
                    pl.BlockSpec((1, d), lambda i, j: (0, 0)),
                    pl.BlockSpec((None, d, tn), lambda i, j: (widx, 0, col0 // tn + j)),
                    pl.BlockSpec((1, tn), lambda i, j: (0, j))]
        aliases = {}
        if out is not None:
            args.append(out)
            in_specs.append(pl.BlockSpec(memory_space=pl.ANY))
            aliases = {5: 0}
        out = pl.pallas_call(
            _mm_norm_kernel,
            grid=(nrows // tm, n // tn),
            in_specs=in_specs,
            out_specs=pl.BlockSpec((tm, tn), lambda i, j: (blk0 + i, j)),
            out_shape=jax.ShapeDtypeStruct((ntok, n), F32),
            scratch_shapes=[pltpu.VMEM((tm, d), BF16)],
            input_output_aliases=aliases,
            compiler_params=_params("parallel", "arbitrary"),
        )(*args)
    return out


def _mm_res_kernel(a_ref, w_ref, b_ref, x_ref, gate_ref, o_ref):
    y = _dot(a_ref[...].astype(BF16), w_ref[...]) + b_ref[...]
    o_ref[...] = x_ref[...] + gate_ref[...] * y


def _mm_res(a, w, b, x, mods, layer, dims, tn):
    batch, lat_len, _, tm = dims
    ntok, d = x.shape
    k = a.shape[1]
    gate_blk = 2 * d // tn
    tiles_per_seq = lat_len // tm
    return pl.pallas_call(
        _mm_res_kernel,
        grid=(ntok // tm, d // tn),
        in_specs=[pl.BlockSpec((tm, k), lambda i, j: (i, 0)),
                  pl.BlockSpec((k, tn), lambda i, j: (0, j)),
                  pl.BlockSpec((1, tn), lambda i, j: (0, j)),
                  pl.BlockSpec((tm, tn), lambda i, j: (i, j)),
                  pl.BlockSpec((None, 1, tn), lambda i, j: (
                      layer * 8 + jnp.minimum(i // tiles_per_seq, batch), 0, gate_blk + j))],
        out_specs=pl.BlockSpec((tm, tn), lambda i, j: (i, j)),
        out_shape=jax.ShapeDtypeStruct((ntok, d), F32),
        compiler_params=_params("parallel", "parallel"),
    )(a, w, b.reshape(1, d), x, mods)


def _ffn_kernel(x_ref, mod_ref, g_ref, wg_ref, wu_ref, wd_ref, o_ref, h_scr, acc_scr):
    d = D_MODEL
    f = pl.program_id(1)

    @pl.when(f == 0)
    def _():
        h_scr[...] = _norm_mod(x_ref[...], g_ref[...], mod_ref[:, 4 * d:5 * d],
                               mod_ref[:, 3 * d:4 * d]).astype(BF16)
        acc_scr[...] = jnp.zeros_like(acc_scr)

    h = h_scr[...]
    gate = _dot(h, wg_ref[...])
    up = _dot(h, wu_ref[...])
    act = (_silu(gate) * up).astype(BF16)
    acc_scr[...] += _dot(act, wd_ref[...])

    @pl.when(f == pl.num_programs(1) - 1)
    def _():
        o_ref[...] = x_ref[...] + mod_ref[:, 5 * d:6 * d] * acc_scr[...]


def _ffn(x, mods, g, wg, wu, wd, layer, dims, with_ctx):
    batch, lat_len, _, tm = dims
    ntok, d = x.shape
    dff = wg.shape[2]
    tf = 512
    nrows = ntok if with_ctx else batch * lat_len
    return pl.pallas_call(
        _ffn_kernel,
        grid=(nrows // tm, dff // tf),
        in_specs=[pl.BlockSpec((tm, d), lambda i, f: (i, 0)),
                  _mod_spec(layer, tm, lat_len, batch, 6 * d, 0),
                  pl.BlockSpec((1, d), lambda i, f: (0, 0)),
                  pl.BlockSpec((None, d, tf), lambda i, f: (layer, 0, f)),
                  pl.BlockSpec((None, d, tf), lambda i, f: (layer, 0, f)),
                  pl.BlockSpec((None, tf, d), lambda i, f: (layer, f, 0))],
        out_specs=pl.BlockSpec((tm, d), lambda i, f: (i, 0)),
        out_shape=jax.ShapeDtypeStruct((ntok, d), F32),
        scratch_shapes=[pltpu.VMEM((tm, d), BF16), pltpu.VMEM((tm, d), F32)],
        compiler_params=_params("parallel", "arbitrary"),
    )(x, mods, g.reshape(1, d), wg, wu, wd)


def _final_norm_kernel(x_ref, g_ref, o_ref):
    x = x_ref[...]
    o_ref[...] = x * lax.rsqrt(jnp.mean(x * x, axis=-1, keepdims=True) + EPS) * g_ref[...]


def _final_norm(x, g, nrows, tm):
    d = x.shape[1]
    return pl.pallas_call(
        _final_norm_kernel,
        grid=(nrows // tm,),
        in_specs=[pl.BlockSpec((tm, d), lambda i: (i, 0)), pl.BlockSpec((1, d), lambda i: (0, 0))],
        out_specs=pl.BlockSpec((tm, d), lambda i: (i, 0)),
        out_shape=jax.ShapeDtypeStruct((nrows, d), F32),
        compiler_params=_params("parallel"),
    )(x, g.reshape(1, d))


def _s5_tables(a_re, a_im, log_dt, b_re, b_im, c_re, c_im):
    t_len = S5_CHUNK
    dt = jnp.exp(log_dt)[..., None]
    lr, li = a_re * dt, a_im * dt
    er = jnp.exp(lr)
    nr, ni = er * jnp.cos(li) - 1.0, er * jnp.sin(li)
    den = a_re * a_re + a_im * a_im
    qr, qi = (nr * a_re + ni * a_im) / den, (ni * a_re - nr * a_im) / den
    bbr = qr[..., None] * b_re - qi[..., None] * b_im
    bbi = qr[..., None] * b_im + qi[..., None] * b_re
    ngroups = a_re.shape[1]
    ti = jnp.arange(t_len, dtype=F32)

    def powers(exp_fwd, exp_bwd):
        e = jnp.stack([exp_fwd, exp_bwd], axis=1)[:, :, None, None]
        mag = jnp.exp(lr[None] * e)
        return mag * jnp.cos(li[None] * e), mag * jnp.sin(li[None] * e)

    def times_c(pr, pi):
        cr = c_re[None] * pr[:, :, :, None, :] - c_im[None] * pi[:, :, :, None, :]
        ci = c_re[None] * pi[:, :, :, None, :] + c_im[None] * pr[:, :, :, None, :]
        relay = lambda t: t.transpose(1, 2, 4, 0, 3).reshape(2, ngroups, S5_STATE, S5_ROW)
        return relay(cr), relay(-ci)

    pr, pi = powers(t_len - 1 - ti, ti)
    sr = pr[..., None] * bbr[None] - pi[..., None] * bbi[None]
    si = pr[..., None] * bbi[None] + pi[..., None] * bbr[None]
    to_rows = lambda t: t.transpose(1, 2, 0, 4, 3).reshape(2, ngroups, S5_ROW, S5_STATE)
    w_sr, w_si = to_rows(sr), to_rows(si)
    w_or, w_oi = times_c(*powers(ti + 1, t_len - ti))
    w_nr, w_ni = times_c(*powers(ti - (t_len - 1), -ti))
    full = jnp.full((t_len,), float(t_len), F32)
    lam_r, lam_i = powers(full, full)
    return (w_sr, w_si, w_nr, w_ni, w_or.astype(BF16), w_oi.astype(BF16), lam_r[0], lam_i[0])


def _s5_perm():
    src = jnp.arange(S5_SLAB, dtype=jnp.int32)
    t, g, c = src // 128, (src % 128) // S5_GROUP, src % S5_GROUP
    dst = g * S5_ROW + t * S5_GROUP + c
    return (dst[:, None] == jnp.arange(S5_SLAB, dtype=jnp.int32)[None, :]).astype(BF16)


def _s5_kernel(*refs, nparts, split_steps, batch, ncc, ncl, col_chunks):
    nin = S5_CHUNK if split_steps else nparts
    x_refs = refs[:nin]
    perm_ref, wsr_ref, wsi_ref, wnr_ref, wni_ref, wor_ref, woi_ref, lr_ref, li_ref = refs[nin:nin + 9]
    y_refs = refs[nin + 9:nin + 9 + nparts]
    sr_scr, si_scr, hr_scr, hi_scr, y_scr = refs[nin + 9 + nparts:]
    rows = y_scr.shape[0]
    groups = S5_SLAB_GROUPS

    def load(t):
        if split_steps:
            return x_refs[t][...]
        parts = [r[:, t].reshape(-1, 128) for r in x_refs]
        return (parts[0] if nparts == 1 else jnp.concatenate(parts, axis=0)).astype(BF16)

    xcat = jnp.concatenate([load(t) for t in range(S5_CHUNK)], axis=1)
    u_all = _dot(xcat, perm_ref[...]).astype(BF16)
    t_in = lax.broadcasted_iota(jnp.int32, (S5_ROW, S5_ROW), 0) // S5_GROUP
    t_out = lax.broadcasted_iota(jnp.int32, (S5_ROW, S5_ROW), 1) // S5_GROUP

    def hdot(a, b):
        a_hi, b_hi = a.astype(BF16), b.astype(BF16)
        a_lo, b_lo = (a - a_hi.astype(F32)).astype(BF16), (b - b_hi.astype(F32)).astype(BF16)
        return _dot(a_hi, b_hi) + (_dot(a_hi, b_lo) + _dot(a_lo, b_hi))

    def row_of(b, j, direction):
        if direction == 0:
            kc, kl = j, j - ncc
        else:
            kc, kl = ncc - 1 - j, ncl - 1 - (j - ncc)
        if col_chunks:
            kl = (kl % col_chunks) * GRID_W + kl // col_chunks
        return jnp.where(j < ncc, batch * ncl + b * ncc + kc, b * ncl + kl)

    for direction in range(2):
        for g in range(groups):
            u = u_all[:, g * S5_ROW:(g + 1) * S5_ROW]
            sr_scr[g * rows:(g + 1) * rows, :] = _dot(u, wsr_ref[direction, g].astype(BF16))
            si_scr[g * rows:(g + 1) * rows, :] = _dot(u, wsi_ref[direction, g].astype(BF16))
        lam_r = lr_ref[direction]
        lam_i = li_ref[direction]

        def step(j, carry, direction=direction, lam_r=lam_r, lam_i=lam_i):
            out = []
            for b in range(batch):
                h_r, h_i = carry[2 * b], carry[2 * b + 1]
                idx = pl.ds(row_of(b, j, direction), groups, stride=rows)
                hr_scr[idx, :] = h_r
                hi_scr[idx, :] = h_i
                s_r = sr_scr[idx, :]
                s_i = si_scr[idx, :]
                out += [lam_r * h_r - lam_i * h_i + s_r, lam_r * h_i + lam_i * h_r + s_i]
            return tuple(out)

        zero = jnp.zeros((groups, S5_STATE), F32)
        lax.fori_loop(0, ncc + ncl, step, (zero,) * (2 * batch))
        causal = (t_out >= t_in) if direction == 0 else (t_in >= t_out)
        for g in range(groups):
            u = u_all[:, g * S5_ROW:(g + 1) * S5_ROW]
            m_intra = jnp.where(causal, hdot(wsr_ref[direction, g], wnr_ref[direction, g])
                                + hdot(wsi_ref[direction, g], wni_ref[direction, g]), 0.0).astype(BF16)
            y = (_dot(u, m_intra)
                 + _dot(hr_scr[g * rows:(g + 1) * rows, :].astype(BF16), wor_ref[direction, g])
                 + _dot(hi_scr[g * rows:(g + 1) * rows, :].astype(BF16), woi_ref[direction, g]))
            if direction == 0:
                y_scr[:, g * S5_ROW:(g + 1) * S5_ROW] = y
            else:
                y_scr[:, g * S5_ROW:(g + 1) * S5_ROW] += y

    z = _dot_nt(y_scr[...].astype(BF16), perm_ref[...])
    for t in range(S5_CHUNK):
        zt = z[:, t * 128:(t + 1) * 128]
        r0 = 0
        for y_ref in y_refs:
            n = y_ref.shape[0] * (y_ref.shape[2] if len(y_ref.shape) == 4 else 1)
            y_ref[:, t] = zt[r0:r0 + n].reshape(y_ref.shape[:1] + y_ref.shape[2:])
            r0 += n


def _s5_core(h, hbf, tables, dims, col_major):
    batch, lat_len, ctx_len, _ = dims
    wsr, wsi, wnr, wni, wor, woi, lam_r, lam_i = tables
    d = D_MODEL
    ncc, ncl = ctx_len // S5_CHUNK, lat_len // S5_CHUNK
    rows = batch * (ncc + ncl)
    gs = S5_SLAB_GROUPS
    if col_major:
        grid_rows = lat_len // GRID_W
        assert grid_rows % S5_CHUNK == 0
        col_chunks = grid_rows // S5_CHUNK
        nlat = batch * lat_len
        xs = [h[:nlat].reshape(batch * col_chunks, S5_CHUNK, GRID_W, d),
              h[nlat:].reshape(batch * ncc, S5_CHUNK, d)]
        blocks = [pl.BlockSpec((batch * col_chunks, S5_CHUNK, GRID_W, 128), lambda q: (0, 0, 0, q)),
                  pl.BlockSpec((batch * ncc, S5_CHUNK, 128), lambda q: (0, 0, q))]
        ins, in_blocks = xs, blocks
    else:
        col_chunks = 0
        xs = [h.reshape(rows, S5_CHUNK, d)]
        blocks = [pl.BlockSpec((rows, S5_CHUNK, 128), lambda q: (0, 0, q))]
        ins = [hbf.reshape(rows, S5_CHUNK * d)] * S5_CHUNK
        in_blocks = [pl.BlockSpec((rows, 128), lambda q, t=t: (0, t * (d // 128) + q)) for t in range(S5_CHUNK)]
    kern = functools.partial(_s5_kernel, nparts=len(xs), split_steps=not col_major, batch=batch, ncc=ncc, ncl=ncl,
                             col_chunks=col_chunks)
    wspec = lambda a, b: pl.BlockSpec((2, gs, a, b), lambda q: (0, q, 0, 0))
    ys = pl.pallas_call(
        kern,
        grid=(S5_GROUPS // gs,),
        in_specs=in_blocks + [pl.BlockSpec((S5_SLAB, S5_SLAB), lambda q: (0, 0)),
                           wspec(S5_ROW, S5_STATE), wspec(S5_ROW, S5_STATE),
                           wspec(S5_STATE, S5_ROW), wspec(S5_STATE, S5_ROW),
                           wspec(S5_STATE, S5_ROW), wspec(S5_STATE, S5_ROW),
                           pl.BlockSpec((2, gs, S5_STATE), lambda q: (0, q, 0)),
                           pl.BlockSpec((2, gs, S5_STATE), lambda q: (0, q, 0))],
        out_specs=blocks,
        out_shape=[jax.ShapeDtypeStruct(x.shape, F32) for x in xs],
        scratch_shapes=[pltpu.VMEM((gs * rows, S5_STATE), F32) for _ in range(4)]
        + [pltpu.VMEM((rows, S5_SLAB), F32)],
        compiler_params=_params("parallel"),
    )(*ins, _s5_perm(), wsr, wsi, wnr, wni, wor, woi, lam_r, lam_i)
    if col_major:
        return jnp.concatenate([ys[0].reshape(-1, d), ys[1].reshape(-1, d)], axis=0)
    return ys[0].reshape(-1, d)


def _s5_glu_kernel(h_ref, y_ref, dskip_ref, w1_ref, w2_ref, b1_ref, b2_ref, x_ref, gate_ref, o_ref, z_scr):
    @pl.when(pl.program_id(1) == 0)
    def _():
        y = h_ref[...] * dskip_ref[...] + y_ref[...]
        z = 0.5 * y * (1.0 + jnp.tanh(math.sqrt(2.0 / math.pi) * (y + 0.044715 * (y * y * y))))
        z_scr[...] = z.astype(BF16)

    z = z_scr[...]
    lin = _dot(z, w1_ref[...]) + b1_ref[...]
    gat = _dot(z, w2_ref[...]) + b2_ref[...]
    o_ref[...] = x_ref[...] + gate_ref[...] * (lin * _sigmoid(gat))


def _s5_glu(h, y, dskip, w, b, x, mods, layer, dims):
    batch, lat_len, _, tm = dims
    ntok, d = x.shape
    tn = 512
    nblk = d // tn
    tiles_per_seq = lat_len // tm
    b = b.reshape(1, 2 * d)
    return pl.pallas_call(
        _s5_glu_kernel,
        grid=(ntok // tm, nblk),
        in_specs=[pl.BlockSpec((tm, d), lambda i, j: (i, 0)),
                  pl.BlockSpec((tm, d), lambda i, j: (i, 0)),
                  pl.BlockSpec((1, d), lambda i, j: (0, 0)),
                  pl.BlockSpec((d, tn), lambda i, j: (0, j)),
                  pl.BlockSpec((d, tn), lambda i, j: (0, nblk + j)),
                  pl.BlockSpec((1, tn), lambda i, j: (0, j)),
                  pl.BlockSpec((1, tn), lambda i, j: (0, nblk + j)),
                  pl.BlockSpec((tm, tn), lambda i, j: (i, j)),
                  pl.BlockSpec((None, 1, tn), lambda i, j: (
                      layer * 8 + jnp.minimum(i // tiles_per_seq, batch), 0, 2 * nblk + j))],
        out_specs=pl.BlockSpec((tm, tn), lambda i, j: (i, j)),
        out_shape=jax.ShapeDtypeStruct((ntok, d), F32),
        scratch_shapes=[pltpu.VMEM((tm, d), BF16)],
        compiler_params=_params("parallel", "arbitrary"),
    )(h, y, dskip.reshape(1, d), w, w, b, b, x, mods)


def _s5_layer(x, mods, layer, j, dims, p):
    batch, lat_len, ctx_len, _ = dims
    col_major = (j % 2) == 1
    h, hbf = _prenorm(x, mods, p['norm_g'][layer, 0], layer, dims)
    tables = _s5_tables(p['s5_a_re'][j], p['s5_a_im'][j], p['s5_log_dt'][j], p['s5_b_re'][j],
                        p['s5_b_im'][j], p['s5_c_re'][j], p['s5_c_im'][j])
    y = _s5_core(h, hbf, tables, dims, col_major)
    return _s5_glu(h, y, p['s5_d'][j], p['s5_glu_w'][j].astype(BF16), p['s5_glu_b'][j], x, mods, layer, dims)


def _seq_blocks(dims):
    batch, lat_len, ctx_len, _ = dims
    return [(lat_len, 0), (ctx_len, batch * lat_len // ctx_len)]


def _dwconv3_kernel(u_ref, w_ref, b_ref, o_ref, *maybe_bf16_ref):
    x = u_ref[...]
    n = x.shape[0]
    row = lax.broadcasted_iota(jnp.int32, x.shape, 0)
    prev = jnp.where(row == 0, 0.0, pltpu.roll(x, 1, 0))
    nxt = jnp.where(row == n - 1, 0.0, pltpu.roll(x, n - 1, 0))
    y = prev * w_ref[0:1, :] + x * w_ref[1:2, :] + nxt * w_ref[2:3, :] + b_ref[...]
    o_ref[...] = y
    for r in maybe_bf16_ref:
        r[...] = y.astype(BF16)


def _dwconv3(u, w, b, dims, col0, ncols, with_bf16):
    batch = dims[0]
    ntok = u.shape[0]
    tc = 256
    cb0 = col0 // tc
    w8 = jnp.zeros((8, w.shape[1]), F32).at[:3].set(w)
    outs = []
    for n, off in _seq_blocks(dims):
        out_shape = [jax.ShapeDtypeStruct((batch * n, ncols), F32)]
        out_specs = [pl.BlockSpec((n, tc), lambda s, j: (s, j))]
        if with_bf16:
            out_shape.append(jax.ShapeDtypeStruct((batch * n, ncols), BF16))
            out_specs.append(pl.BlockSpec((n, tc), lambda s, j: (s, j)))
        outs.append(pl.pallas_call(
            _dwconv3_kernel,
            grid=(batch, ncols // tc),
            in_specs=[pl.BlockSpec((n, tc), lambda s, j, off=off: (off + s, cb0 + j)),
                      pl.BlockSpec((8, tc), lambda s, j: (0, cb0 + j)),
                      pl.BlockSpec((1, tc), lambda s, j: (0, cb0 + j))],
            out_specs=out_specs,
            out_shape=out_shape,
            compiler_params=_params("parallel", "parallel"),
        )(u, w8, b.reshape(1, -1)))
    return outs


def _phase_table_kernel(ar_ref, ai_ref, br_ref, bi_ref, c_ref, s_ref):
    br, bi = br_ref[...], bi_ref[...]
    for r in range(ar_ref.shape[0]):
        ar, ai = ar_ref[r:r + 1, :], ai_ref[r:r + 1, :]
        c_ref[r * 16:(r + 1) * 16, :] = (ar * br - ai * bi).astype(BF16)
        s_ref[r * 16:(r + 1) * 16, :] = (ai * br + ar * bi).astype(BF16)


def _phase_tables(coarse, fine, period):
    def unit(m):
        ang = (m % period).astype(F32) * (2.0 * math.pi / period)
        return jnp.cos(ang), jnp.sin(ang)

    ar, ai = unit(coarse)
    br, bi = unit(fine)
    ntab, r1, ncol = coarse.shape
    rb = min(32, r1)
    out = jax.ShapeDtypeStruct((ntab, r1 * 16, ncol), BF16)
    cspec = pl.BlockSpec((None, rb, ncol), lambda t, i: (t, i, 0))
    fspec = pl.BlockSpec((None, 16, ncol), lambda t, i: (t, 0, 0))
    ospec = pl.BlockSpec((None, rb * 16, ncol), lambda t, i: (t, i, 0))
    return pl.pallas_call(
        _phase_table_kernel,
        grid=(ntab, r1 // rb),
        in_specs=[cspec, cspec, fspec, fspec],
        out_specs=[ospec, ospec],
        out_shape=[out, out],
        compiler_params=_params("parallel", "parallel"),
    )(ar, ai, br, bi)


def _dft_tables(n):
    m = n // 2
    idx = jnp.arange(m, dtype=jnp.int32)[None, :]
    r1 = jnp.arange(m // 16, dtype=jnp.int32)[:, None]
    r0 = jnp.arange(16, dtype=jnp.int32)[:, None]
    par = jnp.arange(2, dtype=jnp.int32)[:, None, None]
    tau = 2 * idx[None] + par
    cmat, smat = _phase_tables(32 * r1[None] * tau, (2 * r0[None] + 1) * tau, 4 * n)
    odd = 2 * idx[None] + 1
    ctm, stm = _phase_tables(jnp.broadcast_to(32 * r1[None] * odd, (2,) + (m // 16, m)), (2 * r0[None] + par) * odd, 4 * n)
    return cmat, smat, ctm, stm


def _hy_filter_kernel(feat_ref, w1_ref, b1_ref, w2_ref, b2_ref, w3_ref, b3_ref, fq_ref, tu_ref, dl_ref,
                      w4_ref, hs_ref, hd_ref, hdn_scr, split_scr):
    @pl.when(pl.program_id(0) == 0)
    def _():
        fq = fq_ref[...]
        h = jnp.sin(fq * (jnp.dot(feat_ref[...], w1_ref[...], precision=HIGHEST, preferred_element_type=F32) + b1_ref[...]))
        h = jnp.sin(fq * (jnp.dot(h, w2_ref[...], precision=HIGHEST, preferred_element_type=F32) + b2_ref[...]))
        h = jnp.sin(fq * (jnp.dot(h, w3_ref[...], precision=HIGHEST, preferred_element_type=F32) + b3_ref[...]))
        hdn_scr[...] = h

    hdn = hdn_scr[...]
    decay = jnp.exp(-tu_ref[...] * dl_ref[...])
    row = lax.broadcasted_iota(jnp.int32, decay.shape, 0)
    m = decay.shape[0] // 2
    for order in range(HY_ORDER):
        fwd = jnp.dot(hdn, w4_ref[2 * order], precision=HIGHEST, preferred_element_type=F32) * decay
        bwd = jnp.dot(hdn, w4_ref[2 * order + 1], precision=HIGHEST, preferred_element_type=F32) * decay
        bwd = jnp.where(row == 0, 0.0, bwd)
        norm = jnp.sum(jnp.abs(fwd), axis=0, keepdims=True) + jnp.sum(jnp.abs(bwd), axis=0, keepdims=True)
        for out_ref, vals in ((hs_ref, (fwd + bwd) / norm), (hd_ref, (fwd - bwd) / norm)):
            split_scr[...] = vals
            for parity in range(2):
                out_ref[order, parity] = split_scr[pl.ds(parity, m, stride=2), :].astype(BF16)


def _hy_filters(n, p, j):
    d = D_MODEL
    t = jnp.arange(n, dtype=F32)
    t_unit = t / max(n - 1, 1)
    bands = jnp.linspace(1e-4, HY_BANDS - 1, HY_BANDS, dtype=F32)
    ang = (2.0 * math.pi / n) * t[:, None] * bands[None, :]
    feats = jnp.concatenate([t_unit[:, None], jnp.cos(ang), -jnp.sin(ang)], axis=-1)
    feats = jnp.pad(feats, ((0, 0), (0, HY_EMB_PAD - HY_EMB)))
    w1 = jnp.pad(p['hy_f_w1'][j], ((0, HY_EMB_PAD - HY_EMB), (0, 0)))
    deltas = jnp.abs(jnp.linspace(HY_MIN_DECAY, HY_MAX_DECAY, d, dtype=F32)).reshape(1, d)
    w4 = p['hy_f_w4'][j].reshape(HY_FILTER_HIDDEN, 2 * HY_ORDER, d).transpose(1, 0, 2)
    hid = HY_FILTER_HIDDEN
    tc = 128
    full = lambda shape: pl.BlockSpec(shape, lambda c: tuple(0 for _ in shape))
    row = lambda v: v.reshape(1, hid)
    return pl.pallas_call(
        _hy_filter_kernel,
        grid=(d // tc,),
        in_specs=[full((n, HY_EMB_PAD)), full((HY_EMB_PAD, hid)), full((1, hid)), full((hid, hid)), full((1, hid)),
                  full((hid, hid)), full((1, hid)), full((1, hid)), full((n, 1)),
                  pl.BlockSpec((1, tc), lambda c: (0, c)),
                  pl.BlockSpec((2 * HY_ORDER, hid, tc), lambda c: (0, 0, c))],
        out_specs=[pl.BlockSpec((HY_ORDER, 2, n // 2, tc), lambda c: (0, 0, 0, c)),
                   pl.BlockSpec((HY_ORDER, 2, n // 2, tc), lambda c: (0, 0, 0, c))],
        out_shape=[jax.ShapeDtypeStruct((HY_ORDER, 2, n // 2, d), BF16),
                   jax.ShapeDtypeStruct((HY_ORDER, 2, n // 2, d), BF16)],
        scratch_shapes=[pltpu.VMEM((n, hid), F32), pltpu.VMEM((n, tc), F32)],
        compiler_params=_params("arbitrary"),
    )(feats, w1, row(p['hy_f_b1'][j]), p['hy_f_w2'][j], row(p['hy_f_b2'][j]), p['hy_f_w3'][j],
      row(p['hy_f_b3'][j]), row(p['hy_f_freq'][j]), t_unit.reshape(n, 1), deltas, w4)


def _half_spectra(c_ref, s_ref, xe_c, xo_c, xe_s, xo_s):
    a, b = _dot(c_ref[0], xe_c), _dot(c_ref[1], xo_c)
    cs, ds = _dot(s_ref[0], xe_s), _dot(s_ref[1], xo_s)
    return a + b, a - b, cs + ds, ds - cs


def _dft_filter_kernel(c_ref, s_ref, hse_ref, hso_ref, hde_ref, hdo_ref, hr_ref, hi_ref):
    pc, qc, ps, qs = _half_spectra(c_ref, s_ref, hse_ref[...], hso_ref[...], hde_ref[...], hdo_ref[...])
    hr_ref[0], hr_ref[1] = pc, qc
    hi_ref[0], hi_ref[1] = -ps, -qs


def _dft_filter(cmat, smat, hs, hd, tk, td):
    order, _, m, d = hs.shape
    nd = d // td
    tab = pl.BlockSpec((2, tk, m), lambda i, o, j: (0, i, 0))
    even = pl.BlockSpec((None, None, m, td), lambda i, o, j: (o, 0, 0, j))
    odd = pl.BlockSpec((None, None, m, td), lambda i, o, j: (o, 1, 0, j))
    out = pl.BlockSpec((None, 2, tk, td), lambda i, o, j: (o, 0, i, j))
    sds = jax.ShapeDtypeStruct((order, 2, m, d), F32)
    return pl.pallas_call(
        _dft_filter_kernel,
        grid=(m // tk, order, nd),
        in_specs=[tab, tab, even, odd, even, odd],
        out_specs=[out, out],
        out_shape=[sds, sds],
        compiler_params=_params("parallel", "parallel", "parallel"),
    )(cmat, smat, hs, hs, hd, hd)


def _lane_blocks(nrows, width, row_block, col_block):
    ncol = width // 128
    return [pl.BlockSpec((nrows, 128), lambda *g, c=c: (row_block(*g), col_block(*g) * ncol + c))
            for c in range(ncol)]


def _parity_rows(refs, parity, count):
    cols = [r[pl.ds(parity, count, stride=2), :] for r in refs]
    return cols[0] if len(cols) == 1 else jnp.concatenate(cols, axis=1)


def _dft_fwd_kernel(c_ref, s_ref, *refs):
    z_refs = refs[:-4]
    hr_ref, hi_ref, ua_ref, ub_ref = refs[-4:]
    m = z_refs[0].shape[0] // 2
    ze = _parity_rows(z_refs, 0, m).astype(BF16)
    zo = _parity_rows(z_refs, 1, m).astype(BF16)
    pc, qc, ps, qs = _half_spectra(c_ref, s_ref, ze, zo, ze, zo)
    yrp = pc * hr_ref[0] + ps * hi_ref[0]
    yip = pc * hi_ref[0] - ps * hr_ref[0]
    yrq = qc * hr_ref[1] + qs * hi_ref[1]
    yiq = qc * hi_ref[1] - qs * hr_ref[1]
    ua_ref[0] = (yrp + yrq).astype(BF16)
    ub_ref[0] = (yip - yiq).astype(BF16)
    ua_ref[1] = (yrp - yrq).astype(BF16)
    ub_ref[1] = (yip + yiq).astype(BF16)


def _dft_fwd(cmat, smat, z, zcol, d, hr, hi, order, batch, tk, td):
    m = cmat.shape[1]
    tab = pl.BlockSpec((2, tk, m), lambda i, s, j: (0, i, 0), pipeline_mode=pl.Buffered(1))
    hspec = pl.BlockSpec((None, 2, tk, td), lambda i, s, j: (order, 0, i, j))
    out = pl.BlockSpec((2, None, tk, td), lambda i, s, j: (0, s, i, j))
    sds = jax.ShapeDtypeStruct((2, batch, m, d), BF16)
    return pl.pallas_call(
        _dft_fwd_kernel,
        grid=(m // tk, batch, d // td),
        in_specs=[tab, tab] + _lane_blocks(2 * m, td, lambda i, s, j: s, lambda i, s, j: zcol // td + j)
        + [hspec, hspec],
        out_specs=[out, out],
        out_shape=[sds, sds],
        compiler_params=_params("parallel", "parallel", "parallel"),
    )(cmat, smat, *([z] * (td // 128)), hr, hi)


def _dft_inv_kernel(ct_ref, st_ref, ua_ref, ub_ref, *refs, inv_n):
    ncol = (len(refs) - 2) // 3
    xg_refs, zp_refs, bias_ref, o_ref = refs[:ncol], refs[ncol:2 * ncol], refs[2 * ncol], refs[2 * ncol + 1]
    mix_scrs = refs[2 * ncol + 2:]
    ts = ct_ref.shape[1]
    bias = bias_ref[...]
    for parity in range(2):
        y = (_dot(ct_ref[parity], ua_ref[parity]) - _dot(st_ref[parity], ub_ref[parity])) * inv_n
        out = _parity_rows(xg_refs, parity, ts) * (y + _parity_rows(zp_refs, parity, ts) * bias)
        for c, scr in enumerate(mix_scrs):
            scr[pl.ds(parity, ts, stride=2), :] = out[:, c * 128:(c + 1) * 128]
    for c, scr in enumerate(mix_scrs):
        o_ref[:, c * 128:(c + 1) * 128] = scr[...]


def _dft_inv(ctm, stm, ua, ub, xg, xg_col0, zprev, zcol, bias, batch, ts, td):
    m = ctm.shape[1]
    d = ua.shape[3]
    ncol = td // 128
    tab = pl.BlockSpec((2, ts, m), lambda i, s, j: (0, i, 0))
    spec = pl.BlockSpec((2, None, m, td), lambda i, s, j: (0, s, 0, j))
    rblk = lambda i, s, j: s * (m // ts) + i
    rows = lambda col0: _lane_blocks(2 * ts, td, rblk, lambda i, s, j: col0 // td + j)
    kern = functools.partial(_dft_inv_kernel, inv_n=1.0 / (2 * m))
    return pl.pallas_call(
        kern,
        grid=(m // ts, batch, d // td),
        in_specs=[tab, tab, spec, spec] + rows(xg_col0) + rows(zcol) + [pl.BlockSpec((1, td), lambda i, s, j: (0, j))],
        out_specs=pl.BlockSpec((2 * ts, td), lambda i, s, j: (rblk(i, s, j), j)),
        out_shape=jax.ShapeDtypeStruct((zprev.shape[0], d), F32),
        scratch_shapes=[pltpu.VMEM((2 * ts, 128), F32) for _ in range(ncol)],
        compiler_params=_params("parallel", "parallel", "parallel"),
    )(ctm, stm, ua, ub, *([xg] * ncol), *([zprev] * ncol), bias.reshape(1, d))


def _hyena_layer(x, mods, layer, j, dims, p):
    batch, lat_len, ctx_len, _ = dims
    d = D_MODEL
    u = _mm_norm(x, mods, p['norm_g'][layer, 0], p['hy_in_w'], j, 0, 3 * d, p['hy_in_b'][j], layer, dims, 512)
    parts = _dwconv3(u, p['hy_conv_w'][j], p['hy_conv_b'][j], dims, 0, 3 * d, False)
    z_out = []
    for (n, _), (conv,) in zip(_seq_blocks(dims), parts):
        m = n // 2
        cmat, smat, ctm, stm = _dft_tables(n)
        hs, hd = _hy_filters(n, p, j)
        hr, hi = _dft_filter(cmat, smat, hs, hd, min(512, m), 512)
        zprev, zcol = conv, 0
        for order in range(HY_ORDER):
            ua, ub = _dft_fwd(cmat, smat, zprev, zcol, d, hr, hi, order, batch, min(1024, m), 256)
            zprev = _dft_inv(ctm, stm, ua, ub, conv, (1 + order) * d, zprev, zcol, p['hy_bias'][j, order],
                             batch, min(512, m), 256)
            zcol = 0
        z_out.append(zprev)
    z = jnp.concatenate(z_out, axis=0)
    return _mm_res(z, p['hy_out_w'][j].astype(BF16), p['hy_out_b'][j], x, mods, layer, dims, 512)


def _gdn_conv_kernel(p_ref, w_ref, o_ref):
    x = p_ref[...]
    n = x.shape[0]
    row = lax.broadcasted_iota(jnp.int32, x.shape, 0)
    acc = x * w_ref[2:3, :]
    for s in (1, 2):
        prev = jnp.where(row < s, 0.0, pltpu.roll(x, s, 0))
        nxt = jnp.where(row >= n - s, 0.0, pltpu.roll(x, n - s, 0))
        acc = acc + prev * w_ref[2 - s:3 - s, :] + nxt * w_ref[2 + s:3 + s, :]
    y = _silu(acc)
    head = pl.program_id(1)
    inv = lax.rsqrt(jnp.sum(y * y, axis=-1, keepdims=True) + 1e-6)
    inv = inv * jnp.where(head < GDN_K_HEADS, GDN_HEAD ** -0.5, 1.0)
    o_ref[...] = y * jnp.where(head < 2 * GDN_K_HEADS, inv, 1.0)


def _gdn_conv(proj, w, dims):
    batch = dims[0]
    ntok = proj.shape[0]
    nheads = GDN_CONV_DIM // GDN_HEAD
    w8 = jnp.zeros((8, GDN_CONV_DIM), F32).at[:5].set(w)
    out = None
    for n, off in _seq_blocks(dims):
        args = [proj, w8]
        in_specs = [pl.BlockSpec((n, GDN_HEAD), lambda s, h, off=off: (off + s, h)),
                    pl.BlockSpec((8, GDN_HEAD), lambda s, h: (0, h))]
        aliases = {}
        if out is not None:
            args.append(out)
            in_specs.append(pl.BlockSpec(memory_space=pl.ANY))
            aliases = {2: 0}
        kern = _gdn_conv_kernel if out is None else (lambda p_ref, w_ref, _, o_ref: _gdn_conv_kernel(p_ref, w_ref, o_ref))
        out = pl.pallas_call(
            kern,
            grid=(batch, nheads),
            in_specs=in_specs,
            out_specs=pl.BlockSpec((None, n, GDN_HEAD), lambda s, h, off=off: (h, off + s, 0)),
            out_shape=jax.ShapeDtypeStruct((nheads, ntok, GDN_HEAD), F32),
            input_output_aliases=aliases,
            compiler_params=_params("parallel", "parallel"),
        )(*args)
    return out


def _gdn_gate_kernel(ab_ref, alog_ref, dtb_ref, o_ref):
    c = GDN_CHUNK
    row = lax.broadcasted_iota(jnp.int32, (c, c), 0)
    col = lax.broadcasted_iota(jnp.int32, (c, c), 1)
    lower = jnp.where(row >= col, 1.0, 0.0).astype(F32)
    upper = jnp.where(row <= col, 1.0, 0.0).astype(F32)
    lane = lax.broadcasted_iota(jnp.int32, (c, 128), 1)
    for r in range(ab_ref.shape[0] // c):
        ab = ab_ref[r * c:(r + 1) * c, :]
        xs = ab + dtb_ref[...]
        softplus = jnp.maximum(xs, 0.0) + jnp.log(1.0 + jnp.exp(-jnp.abs(xs)))
        g = -jnp.exp(alog_ref[...]) * softplus
        cum_f = jnp.dot(lower, g, precision=HIGHEST, preferred_element_type=F32)
        cum_b = jnp.dot(upper, g, precision=HIGHEST, preferred_element_type=F32)
        o_ref[r * c:(r + 1) * c, :] = jnp.where(lane < GDN_V_HEADS, cum_f,
                                                jnp.where(lane < 2 * GDN_V_HEADS, cum_b, _sigmoid(ab)))


def _gdn_gates(ab, a_log, dt_bias, tm):
    ntok = ab.shape[0]
    pad = jnp.zeros((2 * GDN_V_HEADS,), F32)
    alog = jnp.concatenate([a_log.reshape(-1), pad]).reshape(1, 128)
    dtb = jnp.concatenate([dt_bias.reshape(-1), pad]).reshape(1, 128)
    return pl.pallas_call(
        _gdn_gate_kernel,
        grid=(ntok // tm,),
        in_specs=[pl.BlockSpec((tm, 128), lambda i: (i, 0)),
                  pl.BlockSpec((1, 128), lambda i: (0, 0)),
                  pl.BlockSpec((1, 128), lambda i: (0, 0))],
        out_specs=pl.BlockSpec((tm, 128), lambda i: (i, 0)),
        out_shape=jax.ShapeDtypeStruct((ntok, 128), F32),
        compiler_params=_params("parallel"),
    )(ab, alog, dtb)


def _unit_triangular_inverses(mats):
    c = mats[0].shape[0]
    row = lax.broadcasted_iota(jnp.int32, (c, c), 0)
    col = lax.broadcasted_iota(jnp.int32, (c, c), 1)
    eye = jnp.where(row == col, 1.0, 0.0).astype(F32)
    ts = [eye - a for a in mats]
    pws = list(mats)
    for _ in range(int(math.log2(c)) - 1):
        pwbs = [pw.astype(BF16) for pw in pws]
        pws = [_dot(pwb, pwb) for pwb in pwbs]
        ts = [t + _dot(t.astype(BF16), pw.astype(BF16)) for t, pw in zip(ts, pws)]
    ms = [eye + a for a in mats]
    m_his = [m.astype(BF16) for m in ms]
    m_los = [(m - m_hi.astype(F32)).astype(BF16) for m, m_hi in zip(ms, m_his)]
    t_his = [t.astype(BF16) for t in ts]
    t_los = [(t - t_hi.astype(F32)).astype(BF16) for t, t_hi in zip(ts, t_his)]
    resids = [eye - (_dot(m_hi, t_hi) + (_dot(m_hi, t_lo) + _dot(m_lo, t_hi)))
              for m_hi, m_lo, t_hi, t_lo in zip(m_his, m_los, t_his, t_los)]
    return [t + _dot(t_hi, r.astype(BF16)) for t, t_hi, r in zip(ts, t_his, resids)]


def _gdn_chunk_kernel(qf_ref, kf_ref, vf_ref, gcf_ref, grf_ref, qb_ref, kb_ref, vb_ref, gcb_ref, grb_ref,
                      of_ref, ob_ref, state_scr, *, nchunks):
    c = GDN_CHUNK

    @pl.when(pl.program_id(2) == 0)
    def _():
        state_scr[...] = jnp.zeros_like(state_scr)

    row = lax.broadcasted_iota(jnp.int32, (c, c), 0)
    col = lax.broadcasted_iota(jnp.int32, (c, c), 1)
    blocks = ((qf_ref, kf_ref, vf_ref, gcf_ref, grf_ref, of_ref), (qb_ref, kb_ref, vb_ref, gcb_ref, grb_ref, ob_ref))
    nkh = qf_ref.shape[0]
    keys, amats, part = [], [], {}
    for direction, (q_ref, k_ref, v_ref, gc_ref, gr_ref, o_ref) in enumerate(blocks):
        if direction == 0:
            incl, strict, last = row >= col, row > col, c - 1
        else:
            incl, strict, last = row <= col, row < col, 0
        for kh in range(nkh):
            for ci in range(nchunks):
                rows = slice(ci * c, (ci + 1) * c)
                q = q_ref[kh, rows, :].astype(BF16)
                k32 = k_ref[kh, rows, :]
                k = k32.astype(BF16)
                k_t = k32.T
                gates_c = gc_ref[kh, ci]
                gates_r = gr_ref[kh, ci]
                kk = _dot_nt(k, k)
                qk = _dot_nt(q, k)
                for e in range(2):
                    ch_g, ch_b = direction * 2 + e, 4 + direction * 2 + e
                    gcc, gcr = gates_c[:, ch_g:ch_g + 1], gates_r[ch_g:ch_g + 1, :]
                    beta_c, beta_r = gates_c[:, ch_b:ch_b + 1], gates_r[ch_b:ch_b + 1, :]
                    decay = jnp.where(incl, jnp.exp(jnp.where(incl, gcc - gcr, 0.0)), 0.0)
                    key = (direction, kh, ci, e)
                    keys.append(key)
                    amats.append(jnp.where(strict, kk * beta_c * decay, 0.0))
                    g_last = gcr[:, last:last + 1]
                    part[key] = dict(
                        q=q, k=k, v=v_ref[2 * kh + e, rows, :].astype(BF16), attn=(qk * decay).astype(BF16),
                        beta_r=beta_r, wscale=beta_r * jnp.exp(gcr), egc=jnp.exp(gcc),
                        kg_t=(k_t * jnp.exp(g_last - gcr)).astype(BF16), e_last=jnp.exp(g_last))
    tmats = _unit_triangular_inverses(amats)
    local = {}
    for key, t in zip(keys, tmats):
        p = part[key]
        u = _dot((t * p['beta_r']).astype(BF16), p['v'])
        w = _dot((t * p['wscale']).astype(BF16), p['k']).astype(BF16)
        local[key] = (p['q'], u, w, p['attn'], p['egc'], p['kg_t'], p['e_last'])

    nslots = state_scr.shape[0]
    states = [state_scr[slot] for slot in range(nslots)]
    streams = [(direction, kh, e) for direction in range(2) for kh in range(nkh) for e in range(2)]
    out_refs = (of_ref, ob_ref)
    for step in range(nchunks):
        cur = [local[d, kh, (step if d == 0 else nchunks - 1 - step), e] for d, kh, e in streams]
        state_bs = [s.astype(BF16) for s in states]
        ws = [_dot(p[2], sb) for p, sb in zip(cur, state_bs)]
        qs = [_dot(p[0], sb) for p, sb in zip(cur, state_bs)]
        v_news = [(p[1] - w).astype(BF16) for p, w in zip(cur, ws)]
        intra = [_dot(p[3], vn) for p, vn in zip(cur, v_news)]
        upd = [_dot(p[5], vn) for p, vn in zip(cur, v_news)]
        for (d, kh, e), p, q_s, o_in in zip(streams, cur, qs, intra):
            ci = step if d == 0 else nchunks - 1 - step
            out_refs[d][2 * kh + e, ci * c:(ci + 1) * c, :] = p[4] * q_s + o_in
        states = [s * p[6] + dlt for s, p, dlt in zip(states, cur, upd)]
    for slot in range(nslots):
        state_scr[slot] = states[slot]


def _gdn_chunks(qkvh, gates, dims):
    batch, lat_len, ctx_len, _ = dims
    c = GDN_CHUNK
    ntok = qkvh.shape[1]
    per_step = min(4, ctx_len // c)
    rows = per_step * c
    nbc, nbl = ctx_len // rows, lat_len // rows
    ctx0 = batch * lat_len // rows
    gk = gates.reshape(ntok // c, c, 4, GDN_K_HEADS, 2).transpose(3, 0, 1, 2, 4).reshape(GDN_K_HEADS, ntok // c, c, 8)
    gk_t = gk.transpose(0, 1, 3, 2)

    def fwd_blk(b, j):
        return jnp.where(j < nbc, ctx0 + b * nbc + j, b * nbl + j - nbc)

    def bwd_blk(b, j):
        return jnp.where(j < nbc, ctx0 + b * nbc + nbc - 1 - j, b * nbl + nbl - 1 - (j - nbc))

    nkh = GDN_KHEADS_PER_STEP
    hblocks = GDN_K_HEADS // nkh

    def specs(blk):
        return [pl.BlockSpec((nkh, rows, GDN_HEAD), lambda b, h, j: (h, blk(b, j), 0)),
                pl.BlockSpec((nkh, rows, GDN_HEAD), lambda b, h, j: (hblocks + h, blk(b, j), 0)),
                pl.BlockSpec((2 * nkh, rows, GDN_HEAD), lambda b, h, j: (hblocks + h, blk(b, j), 0)),
                pl.BlockSpec((nkh, per_step, c, 8), lambda b, h, j: (h, blk(b, j), 0, 0)),
                pl.BlockSpec((nkh, per_step, 8, c), lambda b, h, j: (h, blk(b, j), 0, 0))]

    out_sd = jax.ShapeDtypeStruct((GDN_V_HEADS, ntok, GDN_HEAD), F32)
    return pl.pallas_call(
        functools.partial(_gdn_chunk_kernel, nchunks=per_step),
        grid=(batch, hblocks, nbc + nbl),
        in_specs=specs(fwd_blk) + specs(bwd_blk),
        out_specs=[pl.BlockSpec((2 * nkh, rows, GDN_HEAD), lambda b, h, j: (h, fwd_blk(b, j), 0)),
                   pl.BlockSpec((2 * nkh, rows, GDN_HEAD), lambda b, h, j: (h, bwd_blk(b, j), 0))],
        out_shape=[out_sd, out_sd],
        scratch_shapes=[pltpu.VMEM((4 * nkh, GDN_HEAD, GDN_HEAD), F32)],
        compiler_params=_params("parallel", "parallel", "arbitrary"),
    )(qkvh, qkvh, qkvh, gk, gk_t, qkvh, qkvh, qkvh, gk, gk_t)


def _gdn_gated_norm_kernel(of_ref, ob_ref, z_ref, ng_ref, a_ref):
    ng = ng_ref[...]
    for h in range(of_ref.shape[0]):
        o = of_ref[h] + ob_ref[h]
        o = o * lax.rsqrt(jnp.mean(o * o, axis=-1, keepdims=True) + EPS)
        z = z_ref[:, h * GDN_HEAD:(h + 1) * GDN_HEAD]
        a_ref[:, h * GDN_HEAD:(h + 1) * GDN_HEAD] = (o * ng * _silu(z)).astype(BF16)


def _gdn_gated_norm(o_f, o_b, proj, norm_g, dims):
    tm = dims[3]
    ntok = proj.shape[0]
    hb = 8
    zblk0 = GDN_CONV_DIM // (hb * GDN_HEAD)
    return pl.pallas_call(
        _gdn_gated_norm_kernel,
        grid=(ntok // tm, GDN_V_HEADS // hb),
        in_specs=[pl.BlockSpec((hb, tm, GDN_HEAD), lambda i, h: (h, i, 0)),
                  pl.BlockSpec((hb, tm, GDN_HEAD), lambda i, h: (h, i, 0)),
                  pl.BlockSpec((tm, hb * GDN_HEAD), lambda i, h: (i, zblk0 + h)),
                  pl.BlockSpec((1, GDN_HEAD), lambda i, h: (0, 0))],
        out_specs=pl.BlockSpec((tm, hb * GDN_HEAD), lambda i, h: (i, h)),
        out_shape=jax.ShapeDtypeStruct((ntok, GDN_V), BF16),
        compiler_params=_params("parallel", "parallel"),
    )(o_f, o_b, proj, norm_g.reshape(1, GDN_HEAD))


def _gdn_layer(x, mods, layer, j, dims, p):
    in_w = p['gdn_in_w']
    g = p['norm_g'][layer, 0]
    proj = _mm_norm(x, mods, g, in_w, j, 0, GDN_MAIN, jnp.zeros((GDN_MAIN,), F32), layer, dims, 512)
    ab = _mm_norm(x, mods, g, in_w, j, GDN_MAIN, 128, jnp.zeros((128,), F32), layer, dims, 128)
    qkvh = _gdn_conv(proj, p['gdn_conv_w'][j], dims)
    gates = _gdn_gates(ab, p['gdn_a_log'][j], p['gdn_dt_bias'][j], dims[3])
    o_f, o_b = _gdn_chunks(qkvh, gates, dims)
    a = _gdn_gated_norm(o_f, o_b, proj, p['gdn_norm_g'][j], dims)
    return _mm_res(a, p['gdn_out_w'][j].astype(BF16), jnp.zeros((D_MODEL,), F32), x, mods, layer, dims, 512)


def kernel(x, c, ctx, c_ctx, ada_w, ada_b, norm_g, final_g, ffn_w_gate, ffn_w_up, ffn_w_down, s5_a_re, s5_a_im, s5_log_dt, s5_b_re, s5_b_im, s5_c_re, s5_c_im, s5_d, s5_glu_w, s5_glu_b, hy_in_w, hy_in_b, hy_conv_w, hy_conv_b, hy_f_w1, hy_f_b1, hy_f_w2, hy_f_b2, hy_f_w3, hy_f_b3, hy_f_w4, hy_f_freq, hy_bias, hy_out_w, hy_out_b, gdn_in_w, gdn_conv_w, gdn_a_log, gdn_dt_bias, gdn_norm_g, gdn_out_w):
    p = dict(locals())
    batch, lat_len, d = x.shape
    ctx_len = ctx.shape[1]
    depth = ada_w.shape[0]
    tm = min(512, batch * ctx_len)
    dims = (batch, lat_len, ctx_len, tm)
    assert d == D_MODEL and lat_len % tm == 0 and (batch * ctx_len) % tm == 0 and batch + 1 <= 8

    cvec = jnp.zeros((8, d), F32).at[:batch].set(c).at[batch].set(c_ctx)
    mods = _ada_all(cvec, ada_w, ada_b).reshape(depth * 8, 1, 6 * d)
    tok = jnp.concatenate([x.reshape(batch * lat_len, d), ctx.reshape(batch * ctx_len, d)], axis=0)
    ffn_wg, ffn_wu, ffn_wd = ffn_w_gate.astype(BF16), ffn_w_up.astype(BF16), ffn_w_down.astype(BF16)
    for i in range(depth):
        kind, j = i % N_MIXERS, i // N_MIXERS
        if kind == 0:
            tok = _s5_layer(tok, mods, i, j, dims, p)
        elif kind == 1:
            tok = _hyena_layer(tok, mods, i, j, dims, p)
        else:
            tok = _gdn_layer(tok, mods, i, j, dims, p)
        tok = _ffn(tok, mods, norm_g[i, 1], ffn_wg, ffn_wu, ffn_wd, i, dims, with_ctx=i < depth - 1)
    out = _final_norm(tok, final_g, batch * lat_len, tm)
    return out.reshape(batch, lat_len, d)
```
